```python
import jax
import jax.numpy as jnp
from jax import lax
import numpy as np

D_MODEL = 1024
BATCH = 16
SEQ = 2048
DEPTH = 2

CTX_LEN = 256
GRID_W = 64
EPS = 1e-6
Q_BLOCK = 128

NA_HEADS = 4
NA_DIM = 64
NA_KH = 8
NA_KW = 16
NA_QBW = 16
NA_KBW = 32
MLA_HEADS = 4
MLA_Q_RANK = 256
MLA_KV_RANK = 128
MLA_NOPE = 64
MLA_ROPE = 32
MLA_V = 128
MLA_QK = MLA_NOPE + MLA_ROPE
ROPE_AXIS = MLA_ROPE // 2
ROPE_THETA = 10000.0
GLA_HEADS = 4
GLA_DK = 32
GLA_DV = 64
GLA_GATE_RANK = 16
GLA_GATE_NORM = 16.0
GLA_CHUNK = 64
NA_WIDTH = NA_HEADS * NA_DIM
MLA_WIDTH = MLA_HEADS * MLA_V
GLA_KW = GLA_HEADS * GLA_DK
GLA_WIDTH = GLA_HEADS * GLA_DV
MIX_WIDTH = NA_WIDTH + MLA_WIDTH + GLA_WIDTH
IN_SIZES = (NA_WIDTH, NA_WIDTH, NA_WIDTH, MLA_Q_RANK, MLA_KV_RANK, MLA_ROPE, GLA_KW, GLA_KW, GLA_WIDTH, GLA_WIDTH, GLA_GATE_RANK, GLA_GATE_RANK)
IN_COLS = 3 * NA_WIDTH + MLA_Q_RANK + MLA_KV_RANK + MLA_ROPE + 2 * GLA_KW + 2 * GLA_WIDTH + 2 * GLA_GATE_RANK
N_EXPERTS = 32
TOP_K = 4
D_EXPERT = D_MODEL
SWIGLU_LIMIT = 7.0
SWIGLU_ALPHA = 1.702
MOE_BLOCK = 256

kernel_name = 'hybrid_na_mla_gla_moe_dit'


def rms_norm(x, g):
    xf = x.astype(jnp.float32)
    y = xf * lax.rsqrt(jnp.mean(xf * xf, axis=-1, keepdims=True) + EPS)
    return y.astype(x.dtype) * g


def modulation(cvec, w, b):
    mod = jax.nn.silu(cvec) @ w + b
    return [m[:, None, :] for m in jnp.split(mod, 6, axis=-1)]


def axial_rope_tables(n_tokens):
    t = jnp.arange(n_tokens)
    row = (t // GRID_W).astype(jnp.float32)[:, None]
    col = (t % GRID_W).astype(jnp.float32)[:, None]
    inv = 1.0 / (ROPE_THETA ** (jnp.arange(0, ROPE_AXIS, 2, dtype=jnp.float32) / ROPE_AXIS))
    ang = jnp.concatenate([row * inv, row * inv, col * inv, col * inv], axis=-1)
    return jnp.cos(ang), jnp.sin(ang)


def apply_axial_rope(x, cos, sin):
    xs = x.reshape(x.shape[:-1] + (2, 2, ROPE_AXIS // 2))
    rot = jnp.concatenate([-xs[..., 1:, :], xs[..., :1, :]], axis=-2).reshape(x.shape)
    return x * cos[None, :, None, :].astype(x.dtype) + rot * sin[None, :, None, :].astype(x.dtype)


def block_attention(q, k, v):
    B, T, H, d = q.shape
    qb = min(Q_BLOCK, T)
    q_blocks = jnp.moveaxis(q.reshape(B, T // qb, qb, H, d), 1, 0)
    scale = d ** -0.5

    def attend(q_i):
        s = jnp.einsum('bqhd,bkhd->bhqk', q_i, k).astype(jnp.float32) * scale
        p = jax.nn.softmax(s, axis=-1).astype(v.dtype)
        return jnp.einsum('bhqk,bkhe->bqhe', p, v)

    out = lax.map(attend, q_blocks)
    return jnp.moveaxis(out, 0, 1).reshape(B, T, H, v.shape[-1])


def neighbourhood_attention(q, k, v, k_ctx, v_ctx, rpb):
    B, T, H, d = q.shape
    rows = T // GRID_W
    kh = min(NA_KH, rows)
    n_cb = GRID_W // NA_QBW
    r = np.arange(rows)
    rs = np.clip(r - kh // 2, 0, rows - kh)
    key_rows = rs[:, None] + np.arange(kh)
    cb = np.arange(n_cb) * NA_QBW
    kb = np.clip(cb - NA_KW // 2, 0, GRID_W - NA_KBW)
    key_cols = kb[:, None] + np.arange(NA_KBW)
    q_cols = cb[:, None] + np.arange(NA_QBW)
    cs = np.clip(q_cols - NA_KW // 2, 0, GRID_W - NA_KW)
    in_win = (key_cols[:, None, :] >= cs[:, :, None]) & (key_cols[:, None, :] < cs[:, :, None] + NA_KW)
    dr = key_rows - r[:, None] + NA_KH - 1
    dc = np.clip(key_cols[:, None, :] - q_cols[:, :, None] + NA_KW - 1, 0, 2 * NA_KW - 2)
    key_pos = (key_rows[:, None, :, None] * GRID_W + key_cols[None, :, None, :]).reshape(rows, n_cb, kh * NA_KBW)
    bias = rpb[:, dr[:, None, None, :, None], dc[None, :, :, None, :]]
    bias = jnp.where(in_win[None, None, :, :, None, :], bias.astype(jnp.float32), -jnp.inf)
    bias = jnp.moveaxis(bias.reshape(H, rows, n_cb, NA_QBW, kh * NA_KBW), 1, 0)
    q_blk = jnp.moveaxis(q.reshape(B, rows, n_cb, NA_QBW, H, d), 1, 0)
    n_loc = kh * NA_KBW
    scale = d ** -0.5

    def row_block(args):
        q_r, pos_r, bias_r = args
        k_r = k[:, pos_r]
        v_r = v[:, pos_r]
        s_loc = jnp.einsum('bcqhd,bclhd->bhcql', q_r, k_r).astype(jnp.float32) * scale + bias_r
        s_ctx = jnp.einsum('bcqhd,bmhd->bhcqm', q_r, k_ctx).astype(jnp.float32) * scale
        p = jax.nn.softmax(jnp.concatenate([s_loc, s_ctx], axis=-1), axis=-1).astype(v.dtype)
        return (jnp.einsum('bhcql,bclhe->bcqhe', p[..., :n_loc], v_r)
                + jnp.einsum('bhcqm,bmhe->bcqhe', p[..., n_loc:], v_ctx))

    out = lax.map(row_block, (q_blk, jnp.asarray(key_pos, dtype=jnp.int32), bias))
    return jnp.moveaxis(out, 0, 1).reshape(B, T, H, d)


def gla_chunked(q, k, v, g, s0):
    B, H, T, dk = q.shape
    dv = v.shape[-1]
    n = T // GLA_CHUNK

    def chunks(a):
        return jnp.moveaxis(a.astype(jnp.float32).reshape(B, H, n, GLA_CHUNK, a.shape[-1]), 2, 0)

    lower = jnp.tril(jnp.ones((GLA_CHUNK, GLA_CHUNK), dtype=bool))[:, :, None]

    def step(s, inp):
        q_i, k_i, v_i, g_i = inp
        b = jnp.cumsum(g_i, axis=-2)
        rel = jnp.exp(jnp.where(lower, b[..., :, None, :] - b[..., None, :, :], -jnp.inf))
        attn = jnp.einsum('bhtd,bhtsd,bhsd->bhts', q_i, rel, k_i)
        o = jnp.einsum('bhts,bhse->bhte', attn, v_i) + jnp.einsum('bhtd,bhde->bhte', q_i * jnp.exp(b), s)
        b_end = b[..., -1:, :]
        s = jnp.exp(b_end[..., 0, :])[..., None] * s + jnp.einsum('bhsd,bhse->bhde', k_i * jnp.exp(b_end - b), v_i)
        return s, o

    s, o = lax.scan(step, s0.astype(jnp.float32), (chunks(q), chunks(k), chunks(v), chunks(g)))
    return jnp.moveaxis(o, 0, 2).reshape(B, H, T, dv).astype(v.dtype), s


def gla_final_state(k, v, g):
    b = jnp.cumsum(g.astype(jnp.float32), axis=-2)
    return jnp.einsum('bhsd,bhse->bhde', k.astype(jnp.float32) * jnp.exp(b[..., -1:, :] - b), v.astype(jnp.float32))


def gla_mixer(q, k, v, gf, gb, qc, kc, vc, gfc, gbc, ctx_out):
    def flip(a):
        return jnp.flip(a, axis=2)
    B, H, _, dk = q.shape
    o_ctx = None
    if ctx_out:
        zeros = jnp.zeros((B, H, dk, v.shape[-1]), jnp.float32)
        oc_f, s_f = gla_chunked(qc, kc, vc, gfc, zeros)
        oc_b, s_b = gla_chunked(flip(qc), flip(kc), flip(vc), flip(gbc), zeros)
        o_ctx = oc_f + flip(oc_b)
    else:
        s_f = gla_final_state(kc, vc, gfc)
        s_b = gla_final_state(flip(kc), flip(vc), flip(gbc))
    ox_f, _ = gla_chunked(q, k, v, gf, s_f)
    ox_b, _ = gla_chunked(flip(q), flip(k), flip(v), flip(gb), s_b)
    return ox_f + flip(ox_b), o_ctx


def gla_output(o, gate, o_norm):
    B, H, T, dv = o.shape
    o = rms_norm(o.transpose(0, 2, 1, 3), o_norm).reshape(B, T, H * dv)
    return o * jax.nn.silu(gate)


def project_stream(h, w_in, na_q_norm, na_k_norm, mla_q_a_norm, mla_w_q_b, mla_kv_a_norm,
                   mla_w_kv_b, mla_q_norm, mla_k_norm, gla_w_gate, gla_b_gate, rope):
    B, T, _ = h.shape
    splits = np.cumsum(IN_SIZES)[:-1].tolist()
    (na_q, na_k, na_v, m_cq, m_ckv, m_kr, g_q, g_k, g_v, g_out, g_lf, g_lb) = jnp.split(h @ w_in, splits, axis=-1)

    def heads(a, n):
        return a.reshape(B, T, n, -1)

    def bhtd(a):
        return heads(a, GLA_HEADS).transpose(0, 2, 1, 3)

    na_q = rms_norm(heads(na_q, NA_HEADS), na_q_norm)
    na_k = rms_norm(heads(na_k, NA_HEADS), na_k_norm)
    na_v = heads(na_v, NA_HEADS)
    mq = rms_norm(heads(rms_norm(m_cq, mla_q_a_norm) @ mla_w_q_b, MLA_HEADS), mla_q_norm)
    kv = heads(rms_norm(m_ckv, mla_kv_a_norm) @ mla_w_kv_b, MLA_HEADS)
    k_rope = jnp.broadcast_to(m_kr[:, :, None, :], (B, T, MLA_HEADS, MLA_ROPE))
    mk = rms_norm(jnp.concatenate([kv[..., :MLA_NOPE], k_rope], axis=-1), mla_k_norm)
    mv = kv[..., MLA_NOPE:]
    if rope is not None:
        cos, sin = rope
        mq = jnp.concatenate([mq[..., :MLA_NOPE], apply_axial_rope(mq[..., MLA_NOPE:], cos, sin)], axis=-1)
        mk = jnp.concatenate([mk[..., :MLA_NOPE], apply_axial_rope(mk[..., MLA_NOPE:], cos, sin)], axis=-1)
    gq = bhtd(g_q) * (GLA_DK ** -0.5)
    gk = bhtd(g_k)
    gv = bhtd(g_v)
    gf = bhtd(jax.nn.log_sigmoid(g_lf @ gla_w_gate[0] + gla_b_gate[0])) / GLA_GATE_NORM
    gb = bhtd(jax.nn.log_sigmoid(g_lb @ gla_w_gate[1] + gla_b_gate[1])) / GLA_GATE_NORM
    return na_q, na_k, na_v, mq, mk, mv, gq, gk, gv, gf, gb, g_out


def moe_ffn(h, w_router, b_router, w1, b1, w2, b2):
    shp = h.shape
    xt = h.reshape(-1, shp[-1])
    N, D = xt.shape
    NK = N * TOP_K
    logits = (xt @ w_router + b_router).astype(jnp.float32)
    top_val, top_idx = lax.top_k(logits, TOP_K)
    gates = jax.nn.softmax(top_val, axis=-1)
    flat_e = top_idx.reshape(-1)
    flat_tok = jnp.repeat(jnp.arange(N, dtype=jnp.int32), TOP_K)
    order = jnp.argsort(flat_e)
    e_sorted = flat_e[order]
    counts = jnp.bincount(flat_e, length=N_EXPERTS)
    padded = (counts + MOE_BLOCK - 1) // MOE_BLOCK * MOE_BLOCK
    start = jnp.cumsum(counts) - counts
    pend = jnp.cumsum(padded)
    pstart = pend - padded
    dest_sorted = pstart[e_sorted] + jnp.arange(NK, dtype=jnp.int32) - start[e_sorted]
    dest = jnp.zeros_like(dest_sorted).at[order].set(dest_sorted)
    cap = -(-NK // MOE_BLOCK) * MOE_BLOCK + N_EXPERTS * MOE_BLOCK
    n_blk = cap // MOE_BLOCK
    row_tok = jnp.full((cap,), N, jnp.int32).at[dest].set(flat_tok)
    buf = xt.at[row_tok].get(mode='fill', fill_value=0)
    blk_e = jnp.minimum(jnp.searchsorted(pend, jnp.arange(n_blk, dtype=pend.dtype) * MOE_BLOCK, side='right'), N_EXPERTS - 1)

    def expert_block(args):
        xb, e = args
        hb = xb @ w1[e] + b1[e]
        glu = jnp.minimum(hb[:, :D_EXPERT], SWIGLU_LIMIT)
        lin = jnp.clip(hb[:, D_EXPERT:], -SWIGLU_LIMIT, SWIGLU_LIMIT)
        return ((lin + 1) * (glu * jax.nn.sigmoid(SWIGLU_ALPHA * glu))) @ w2[e] + b2[e]

    y_buf = lax.map(expert_block, (buf.reshape(n_blk, MOE_BLOCK, D), blk_e)).reshape(cap, D)
    y = y_buf[dest].reshape(N, TOP_K, D)
    return jnp.einsum('nk,nkd->nd', gates.astype(y.dtype), y).reshape(shp)


def setup_inputs(seed: int = 0) -> dict:
    key = jax.random.key(seed)
    ks = iter(jax.random.split(key, 32))

    def nrm(shape, scale):
        return jax.random.normal(next(ks), shape, jnp.float32) * scale

    def gain(shape):
        return 1.0 + nrm(shape, 0.05)

    L, D, E, F = DEPTH, D_MODEL, N_EXPERTS, D_EXPERT
    return {
        'x': nrm((BATCH, SEQ, D), 1.0),
        'c': nrm((BATCH, D), 1.0),
        'ctx': nrm((BATCH, CTX_LEN, D), 1.0),
        'c_ctx': nrm((D,), 1.0),
        'w_ada': nrm((L, D, 6 * D), 0.5 * D ** -0.5),
        'b_ada': nrm((L, 6 * D), 0.02),
        'g_mix': gain((L, D)),
        'w_in': nrm((L, D, IN_COLS), D ** -0.5),
        'na_q_norm': gain((L, NA_DIM)),
        'na_k_norm': gain((L, NA_DIM)),
        'na_rpb': nrm((L, NA_HEADS, 2 * NA_KH - 1, 2 * NA_KW - 1), 0.2),
        'mla_q_a_norm': gain((L, MLA_Q_RANK)),
        'mla_w_q_b': nrm((L, MLA_Q_RANK, MLA_HEADS * MLA_QK), MLA_Q_RANK ** -0.5),
        'mla_kv_a_norm': gain((L, MLA_KV_RANK)),
        'mla_w_kv_b': nrm((L, MLA_KV_RANK, MLA_HEADS * (MLA_NOPE + MLA_V)), MLA_KV_RANK ** -0.5),
        'mla_q_norm': gain((L, MLA_QK)),
        'mla_k_norm': gain((L, MLA_QK)),
        'gla_w_gate': nrm((L, 2, GLA_GATE_RANK, GLA_KW), GLA_GATE_RANK ** -0.5),
        'gla_b_gate': nrm((L, 2, GLA_KW), 0.1),
        'gla_o_norm': gain((L, GLA_DV)),
        'w_out': nrm((L, MIX_WIDTH, D), MIX_WIDTH ** -0.5),
        'g_ffn': gain((L, D)),
        'w_router': nrm((L, D, E), D ** -0.5),
        'b_router': nrm((L, E), 0.01),
        'w_moe1': nrm((L, E, D, 2 * F), D ** -0.5),
        'b_moe1': nrm((L, E, 2 * F), 0.01),
        'w_moe2': nrm((L, E, F, D), F ** -0.5),
        'b_moe2': nrm((L, E, D), 0.01),
    }


def reference(x, c, ctx, c_ctx, w_ada, b_ada, g_mix, w_in, na_q_norm, na_k_norm, na_rpb,
              mla_q_a_norm, mla_w_q_b, mla_kv_a_norm, mla_w_kv_b, mla_q_norm, mla_k_norm,
              gla_w_gate, gla_b_gate, gla_o_norm, w_out, g_ffn, w_router, b_router,
              w_moe1, b_moe1, w_moe2, b_moe2):
    B, T, _ = x.shape
    rope = axial_rope_tables(T)
    h_ctx = ctx
    for l in range(DEPTH):
        ctx_out = l < DEPTH - 1
        sh1, sc1, gt1, sh2, sc2, gt2 = modulation(c, w_ada[l], b_ada[l])
        csh1, csc1, cgt1, csh2, csc2, cgt2 = modulation(c_ctx[None, :], w_ada[l], b_ada[l])
        stream_w = (w_in[l], na_q_norm[l], na_k_norm[l], mla_q_a_norm[l], mla_w_q_b[l], mla_kv_a_norm[l],
                    mla_w_kv_b[l], mla_q_norm[l], mla_k_norm[l], gla_w_gate[l], gla_b_gate[l])
        moe_w = (w_router[l], b_router[l], w_moe1[l], b_moe1[l], w_moe2[l], b_moe2[l])
        (na_q, na_k, na_v, mq, mk, mv, gq, gk, gv, gf, gb, g_out) = project_stream(
            rms_norm(x, g_mix[l]) * (1 + sc1) + sh1, *stream_w, rope=rope)
        (cna_q, cna_k, cna_v, cmq, cmk, cmv, cgq, cgk, cgv, cgf, cgb, cg_out) = project_stream(
            rms_norm(h_ctx, g_mix[l]) * (1 + csc1) + csh1, *stream_w, rope=None)

        a_x = neighbourhood_attention(na_q, na_k, na_v, cna_k, cna_v, na_rpb[l])
        b_x = block_attention(mq, jnp.concatenate([mk, cmk], axis=1), jnp.concatenate([mv, cmv], axis=1))
        gla_x, gla_c = gla_mixer(gq, gk, gv, gf, gb, cgq, cgk, cgv, cgf, cgb, ctx_out)
        c_x = gla_output(gla_x, g_out, gla_o_norm[l])
        mix_x = jnp.concatenate([a_x.reshape(B, T, NA_WIDTH), b_x.reshape(B, T, MLA_WIDTH), c_x], axis=-1) @ w_out[l]
        x_new = x + gt1 * mix_x
        x_new = x_new + gt2 * moe_ffn(rms_norm(x_new, g_ffn[l]) * (1 + sc2) + sh2, *moe_w)

        if ctx_out:
            Tc = h_ctx.shape[1]
            a_c = block_attention(cna_q, cna_k, cna_v)
            b_c = block_attention(cmq, cmk, cmv)
            c_c = gla_output(gla_c, cg_out, gla_o_norm[l])
            mix_c = jnp.concatenate([a_c.reshape(B, Tc, NA_WIDTH), b_c.reshape(B, Tc, MLA_WIDTH), c_c], axis=-1) @ w_out[l]
            h_ctx = h_ctx + cgt1 * mix_c
            h_ctx = h_ctx + cgt2 * moe_ffn(rms_norm(h_ctx, g_ffn[l]) * (1 + csc2) + csh2, *moe_w)
        x = x_new
    return x
```

```python
import functools

import numpy as np
import jax
import jax.numpy as jnp
from jax import lax
from jax.experimental import pallas as pl
from jax.experimental.pallas import tpu as pltpu

F32 = jnp.float32
BF16 = jnp.bfloat16

V7X_LANES = 128
V7X_VMEM_BYTES = 64 * 1024 * 1024

EPS = 1e-6
GRID_W = 64
NA_HEADS, NA_DIM, NA_KH, NA_KW = 4, 64, 8, 16
MLA_HEADS, MLA_NOPE, MLA_ROPE, MLA_V = 4, 64, 32, 128
MLA_QK = MLA_NOPE + MLA_ROPE
MLA_Q_RANK, MLA_KV_RANK = 256, 128
ROPE_AXIS = MLA_ROPE // 2
ROPE_THETA = 10000.0
GLA_HEADS, GLA_DK, GLA_DV = 4, 32, 64
GLA_GATE_RANK, GLA_GATE_NORM, GLA_CHUNK = 16, 16.0, 64
NA_WIDTH = NA_HEADS * NA_DIM
MLA_WIDTH = MLA_HEADS * MLA_V
MLA_PAD = MLA_HEADS * V7X_LANES
GLA_KW = GLA_HEADS * GLA_DK
GLA_WIDTH = GLA_HEADS * GLA_DV
N_EXPERTS, TOP_K = 32, 4
SWIGLU_LIMIT, SWIGLU_ALPHA = 7.0, 1.702
MOE_BLOCK = 512

C_NAQ, C_NAK, C_NAV, C_CQ, C_CKV = 0, 256, 512, 768, 1024
C_GQ, C_GK, C_GV, C_GOUT, C_SMALL = 1152, 1280, 1408, 1664, 1920
IN_PACKED = 2048
S_LF, S_LB, S_KR = 0, 16, 64


def _vmem_limit(mib):
    return min(mib * 1024 * 1024, V7X_VMEM_BYTES - 4 * 1024 * 1024)


def _params(sem, mib=48):
    return pltpu.CompilerParams(dimension_semantics=sem, vmem_limit_bytes=_vmem_limit(mib))


def _dot(a, b):
    return jnp.dot(a, b, preferred_element_type=F32)


def _dot_nt(a, b):
    return lax.dot_general(a, b, (((1,), (1,)), ((), ())), preferred_element_type=F32)


def _dot_tn(a, b):
    return lax.dot_general(a, b, (((0,), (0,)), ((), ())), preferred_element_type=F32)


def _split(x):
    hi = x.astype(BF16)
    lo = (x - hi.astype(F32)).astype(BF16)
    return hi, lo


def _rms(x):
    return x * lax.rsqrt(jnp.mean(x * x, axis=-1, keepdims=True) + EPS)


def _ada_kernel(c_ref, w_ref, b_ref, o_ref):
    c = c_ref[...]
    s = (c * jax.nn.sigmoid(c)).astype(BF16)
    o_ref[...] = _dot(s, w_ref[...].astype(BF16)) + b_ref[...]


def _ada(cvec, w_ada, b_ada):
    depth, d, n6 = w_ada.shape
    r = cvec.shape[0]
    tn = 1024
    return pl.pallas_call(
        _ada_kernel,
        grid=(depth, n6 // tn),
        in_specs=[
            pl.BlockSpec((r, d), lambda l, j: (0, 0)),
            pl.BlockSpec((None, d, tn), lambda l, j: (l, 0, j)),
            pl.BlockSpec((None, 1, tn), lambda l, j: (l, 0, j)),
        ],
        out_specs=pl.BlockSpec((None, r, tn), lambda l, j: (l, 0, j)),
        out_shape=jax.ShapeDtypeStruct((depth, r, n6), F32),
        compiler_params=_params(("parallel", "parallel")),
        name="ada_modulation",
    )(cvec, w_ada, b_ada.reshape(depth, 1, n6))


def _rope(x, cos, sin):
    lane = lax.broadcasted_iota(jnp.int32, (x.shape[0], V7X_LANES), 1)
    first = (lane & (ROPE_AXIS - 1)) < (ROPE_AXIS // 2)
    outs = []
    for h in range(MLA_HEADS):
        xs = x[:, V7X_LANES * h:V7X_LANES * (h + 1)]
        rot = jnp.where(first, -pltpu.roll(xs, V7X_LANES - ROPE_AXIS // 2, 1), pltpu.roll(xs, ROPE_AXIS // 2, 1))
        outs.append(xs * cos + rot * sin)
    return jnp.concatenate(outs, axis=1)


def _head_rms_padded(x, n_real):
    outs = []
    for h in range(MLA_HEADS):
        xs = x[:, V7X_LANES * h:V7X_LANES * (h + 1)]
        ms = jnp.sum(xs * xs, axis=-1, keepdims=True) * (1.0 / n_real)
        outs.append(xs * lax.rsqrt(ms + EPS))
    return jnp.concatenate(outs, axis=1)


def _log_sigmoid(x):
    return jnp.minimum(x, 0.0) - jnp.log1p(jnp.exp(-jnp.abs(x)))


def _inproj_kernel(x_ref, sc_ref, sh_ref, gmix_ref, w_ref, cos_ref, sin_ref,
                   naqn_ref, nakn_ref, qan_ref, wqb_ref, kvan_ref, wkk_ref, wkv_ref, mqn_ref, mkn_ref,
                   wgh_ref, wgl_ref, bg_ref, seg_ref,
                   naq_o, nak_o, nav_o, mq_o, mk_o, mv_o, gq_o, gk_o, gv_o, gout_o, gf_o, gb_o, *, use_rope):
    x = x_ref[...]
    h = _rms(x) * gmix_ref[...]
    h = h * (1.0 + sc_ref[...]) + sh_ref[...]
    p = _dot(h.astype(BF16), w_ref[...])

    seg = seg_ref[...]
    q = p[:, C_NAQ:C_NAQ + NA_WIDTH]
    k = p[:, C_NAK:C_NAK + NA_WIDTH]
    qss = _dot((q * q).astype(BF16), seg) * (1.0 / NA_DIM)
    kss = _dot((k * k).astype(BF16), seg) * (1.0 / NA_DIM)
    naq_o[...] = (q * lax.rsqrt(qss + EPS) * naqn_ref[...] * (NA_DIM ** -0.5)).astype(BF16)
    nak_o[...] = (k * lax.rsqrt(kss + EPS) * nakn_ref[...]).astype(BF16)
    nav_o[...] = p[:, C_NAV:C_NAV + NA_WIDTH].astype(BF16)

    small = p[:, C_SMALL:C_SMALL + V7X_LANES]
    cq = _rms(p[:, C_CQ:C_CQ + MLA_Q_RANK]) * qan_ref[...]
    mq = _head_rms_padded(_dot(cq.astype(BF16), wqb_ref[...]), MLA_QK) * mqn_ref[...]
    ckv = (_rms(p[:, C_CKV:C_CKV + MLA_KV_RANK]) * kvan_ref[...]).astype(BF16)
    lane = lax.broadcasted_iota(jnp.int32, small.shape, 1)
    kr = jnp.where((lane >= S_KR) & (lane < S_KR + MLA_ROPE), small, 0.0)
    mk = _dot(ckv, wkk_ref[...]) + jnp.concatenate([kr] * MLA_HEADS, axis=1)
    mk = _head_rms_padded(mk, MLA_QK) * mkn_ref[...]
    if use_rope:
        cos, sin = cos_ref[...], sin_ref[...]
        mq = _rope(mq, cos, sin)
        mk = _rope(mk, cos, sin)
    mq_o[...] = (mq * (MLA_QK ** -0.5)).astype(BF16)
    mk_o[...] = mk.astype(BF16)
    mv_o[...] = _dot(ckv, wkv_ref[...]).astype(BF16)

    gq_o[...] = p[:, C_GQ:C_GQ + GLA_KW] * (GLA_DK ** -0.5)
    gk_o[...] = p[:, C_GK:C_GK + GLA_KW]
    gv_o[...] = p[:, C_GV:C_GV + GLA_WIDTH]
    gout_o[...] = p[:, C_GOUT:C_GOUT + GLA_WIDTH]
    s_hi, s_lo = _split(small)
    wgh = wgh_ref[...]
    pre = _dot(s_hi, wgh) + _dot(s_lo, wgh) + _dot(s_hi, wgl_ref[...]) + bg_ref[...]
    ls = _log_sigmoid(pre) * (1.0 / GLA_GATE_NORM)
    gf_o[...] = ls[:, :GLA_KW]
    gb_o[...] = ls[:, GLA_KW:]


def _inproj(x, sc, sh, lw, cos, sin, use_rope):
    b, t, d = x.shape
    tm = min(512, t)
    full = lambda a: pl.BlockSpec(a.shape, lambda i, j: (0,) * a.ndim)
    row = lambda w: pl.BlockSpec((None, tm, w), lambda i, j: (i, j, 0))
    consts = [lw["gmix"], lw["w_in"]]
    tail = [lw["naqn"], lw["nakn"], lw["qan"], lw["wqb"], lw["kvan"], lw["wkk"], lw["wkv"], lw["mqn"], lw["mkn"],
            lw["wg_hi"], lw["wg_lo"], lw["bg"], lw["seg64"]]
    widths = [(NA_WIDTH, BF16)] * 3 + [(MLA_PAD, BF16), (MLA_PAD, BF16), (MLA_WIDTH, BF16),
                                       (GLA_KW, F32), (GLA_KW, F32), (GLA_WIDTH, F32), (GLA_WIDTH, F32),
                                       (GLA_KW, F32), (GLA_KW, F32)]
    return pl.pallas_call(
        functools.partial(_inproj_kernel, use_rope=use_rope),
        grid=(b, t // tm),
        in_specs=[row(d),
                  pl.BlockSpec((None, 1, d), lambda i, j: (i, 0, 0)),
                  pl.BlockSpec((None, 1, d), lambda i, j: (i, 0, 0))]
                 + [full(a) for a in consts]
                 + [pl.BlockSpec((tm, V7X_LANES), lambda i, j: (j, 0))] * 2
                 + [full(a) for a in tail],
        out_specs=[row(w) for w, _ in widths],
        out_shape=[jax.ShapeDtypeStruct((b, t, w), dt) for w, dt in widths],
        compiler_params=_params(("parallel", "parallel")),
        name="inproj_rope" if use_rope else "inproj_ctx",
    )(x, sc, sh, *consts, cos, sin, *tail)


def _stack_heads(q, width):
    lane = lax.broadcasted_iota(jnp.int32, q.shape, 1)
    zero = jnp.zeros_like(q)
    return jnp.concatenate([jnp.where(lane // width == h, q, zero) for h in range(NA_HEADS)], axis=0)


def _unstack_heads(o4, n, width):
    lane = lax.broadcasted_iota(jnp.int32, (n, o4.shape[1]), 1)
    out = jnp.zeros((n, o4.shape[1]), F32)
    for h in range(NA_HEADS):
        out = out + jnp.where(lane // width == h, o4[h * n:(h + 1) * n], 0.0)
    return out


def _na_kernel(q_ref, k_ref, v_ref, kc_ref, vc_ref, bias_ref, o_ref, *, rows):
    kc = kc_ref[...]
    vc = vc_ref[...]
    n_loc = NA_KH * GRID_W

    def body(r, carry):
        rs = jnp.clip(r - NA_KH // 2, 0, rows - NA_KH)
        off = r - rs
        q = q_ref[pl.ds(pl.multiple_of(r * GRID_W, GRID_W), GRID_W), :]
        ks = k_ref[pl.ds(pl.multiple_of(rs * GRID_W, GRID_W), n_loc), :]
        vs = v_ref[pl.ds(pl.multiple_of(rs * GRID_W, GRID_W), n_loc), :]
        q4 = _stack_heads(q, NA_DIM)
        s_loc = _dot_nt(q4, ks) + bias_ref[off]
        s_ctx = _dot_nt(q4, kc)
        m = jnp.maximum(jnp.max(s_loc, axis=-1, keepdims=True), jnp.max(s_ctx, axis=-1, keepdims=True))
        p_loc = jnp.exp(s_loc - m)
        p_ctx = jnp.exp(s_ctx - m)
        denom = jnp.sum(p_loc, axis=-1, keepdims=True) + jnp.sum(p_ctx, axis=-1, keepdims=True)
        o4 = (_dot(p_loc.astype(BF16), vs) + _dot(p_ctx.astype(BF16), vc)) * (1.0 / denom)
        o_ref[pl.ds(pl.multiple_of(r * GRID_W, GRID_W), GRID_W), :] = _unstack_heads(o4, GRID_W, NA_DIM).astype(BF16)
        return carry

    lax.fori_loop(0, rows, body, 0)


def _na_bias_table(rpb, rows):
    kh = NA_KH
    r_all = np.arange(rows)
    rs_all = np.clip(r_all - kh // 2, 0, rows - kh)
    offs = r_all - rs_all
    n_off = int(offs.max()) + 1
    qc = np.arange(GRID_W)
    kcol = np.arange(GRID_W)
    cs = np.clip(qc - NA_KW // 2, 0, GRID_W - NA_KW)
    in_win = (kcol[None, :] >= cs[:, None]) & (kcol[None, :] < cs[:, None] + NA_KW)
    dc = np.clip(kcol[None, :] - qc[:, None] + NA_KW - 1, 0, 2 * NA_KW - 2)
    j = np.arange(kh)
    dr = j[None, :] - np.arange(n_off)[:, None] + NA_KH - 1
    bias = rpb[:, dr[:, None, :, None], dc[None, :, None, :]]
    bias = jnp.where(in_win[None, None, :, None, :], bias.astype(F32), -jnp.inf)
    bias = jnp.moveaxis(bias, 1, 0)
    return bias.reshape(n_off, NA_HEADS * GRID_W, kh * GRID_W)


def _na_attention(q, k, v, kc, vc, bias):
    b, t, w = q.shape
    tc = kc.shape[1]
    rows = t // GRID_W
    assert rows >= NA_KH and t % GRID_W == 0
    seq = lambda n: pl.BlockSpec((None, n, w), lambda i: (i, 0, 0))
    return pl.pallas_call(
        functools.partial(_na_kernel, rows=rows),
        grid=(b,),
        in_specs=[seq(t), seq(t), seq(t), seq(tc), seq(tc),
                  pl.BlockSpec(bias.shape, lambda i: (0, 0, 0))],
        out_specs=seq(t),
        out_shape=jax.ShapeDtypeStruct((b, t, w), BF16),
        compiler_params=_params(("parallel",)),
        name="na_attention",
    )(q, k, v, kc, vc, bias)


def _softmax_pv(scores, values):
    m = functools.reduce(jnp.maximum, [jnp.max(s, axis=-1, keepdims=True) for s in scores])
    ps = [jnp.exp(s - m) for s in scores]
    denom = functools.reduce(jnp.add, [jnp.sum(p, axis=-1, keepdims=True) for p in ps])
    o = functools.reduce(jnp.add, [_dot(p.astype(BF16), v) for p, v in zip(ps, values)])
    return o * (1.0 / denom)


def _attn_sliced_kernel(*refs, n_parts, heads, dh, dv):
    q_ref, o_ref = refs[0], refs[-1]
    for h in range(heads):
        q = q_ref[:, dh * h:dh * (h + 1)]
        scores = [_dot_nt(q, refs[1 + 2 * p][:, dh * h:dh * (h + 1)]) for p in range(n_parts)]
        values = [refs[2 + 2 * p][:, dv * h:dv * (h + 1)] for p in range(n_parts)]
        o_ref[:, dv * h:dv * (h + 1)] = _softmax_pv(scores, values).astype(o_ref.dtype)


def _attn_masked_kernel(q_ref, k_ref, v_ref, o_ref, *, width):
    n = q_ref.shape[0]
    q4 = _stack_heads(q_ref[...], width)
    o4 = _softmax_pv([_dot_nt(q4, k_ref[...])], [v_ref[...]])
    o_ref[...] = _unstack_heads(o4, n, width).astype(o_ref.dtype)


def _attention(q, parts, heads, dh, dv, tq):
    b, t, wq = q.shape
    tq = min(tq, t)
    flat = [a for kv in parts for a in kv]
    kv_spec = lambda a: pl.BlockSpec((None,) + a.shape[1:], lambda i, j: (i, 0, 0))
    return pl.pallas_call(
        functools.partial(_attn_sliced_kernel, n_parts=len(parts), heads=heads, dh=dh, dv=dv),
        grid=(b, t // tq),
        in_specs=[pl.BlockSpec((None, tq, wq), lambda i, j: (i, j, 0))] + [kv_spec(a) for a in flat],
        out_specs=pl.BlockSpec((None, tq, heads * dv), lambda i, j: (i, j, 0)),
        out_shape=jax.ShapeDtypeStruct((b, t, heads * dv), BF16),
        compiler_params=_params(("parallel", "parallel")),
        name="mla_attention_%d" % len(parts),
    )(q, *flat)


def _attention_masked(q, k, v, width):
    b, t, w = q.shape
    spec = lambda a: pl.BlockSpec((None,) + a.shape[1:], lambda i: (i, 0, 0))
    return pl.pallas_call(
        functools.partial(_attn_masked_kernel, width=width),
        grid=(b,),
        in_specs=[spec(q), spec(k), spec(v)],
        out_specs=spec(q),
        out_shape=jax.ShapeDtypeStruct((b, t, w), BF16),
        compiler_params=_params(("parallel",)),
        name="ctx_na_attention",
    )(q, k, v)


def _gla_chunk(q, k, v, g, st_ref, tri, allowed, hm_k, hm_v, hm_s, end_row):
    g_hi, g_lo = _split(g)
    bcum = _dot(tri, g_hi) + _dot(tri, g_lo)
    bend = bcum[end_row:end_row + 1, :]
    qe = (q * jnp.exp(bcum)).astype(BF16)
    ke = (k * jnp.exp(bend - bcum)).astype(BF16)
    kt = k * jnp.exp(-bcum)
    st = st_ref[...]
    o = _dot_nt(qe, st.astype(BF16))
    kst = jnp.concatenate([jnp.where(m, kt, 0.0) for m in hm_k], axis=0).astype(BF16)
    a = jnp.where(allowed, _dot_nt(qe, kst), 0.0)
    vbd = jnp.concatenate([jnp.where(m, v, 0.0) for m in hm_v], axis=0).astype(BF16)
    o = o + _dot(a.astype(BF16), vbd)
    upd = _dot_tn(v.astype(BF16), ke)
    st_ref[...] = st * jnp.exp(bend) + jnp.where(hm_s, upd, 0.0)
    return o


def _gla_kernel(q_ref, k_ref, v_ref, gf_ref, gb_ref, go_ref,
                qc_ref, kc_ref, vc_ref, gfc_ref, gbc_ref, goc_ref, onorm_ref, seg_ref,
                *rest, ctx_out):
    if ctx_out:
        ox_ref, oc_ref, acc_ref, accc_ref, sf_ref, sb_ref = rest
    else:
        ox_ref, acc_ref, sf_ref, sb_ref = rest
        oc_ref = accc_ref = None
    c = GLA_CHUNK
    t, tc = q_ref.shape[0], qc_ref.shape[0]
    n, nc = t // c, tc // c

    ri = lax.broadcasted_iota(jnp.int32, (c, c), 0)
    ci = lax.broadcasted_iota(jnp.int32, (c, c), 1)
    tri_f = (ci <= ri).astype(BF16)
    tri_b = (ci >= ri).astype(BF16)
    ti = lax.broadcasted_iota(jnp.int32, (c, GLA_HEADS * c), 0)
    si = lax.broadcasted_iota(jnp.int32, (c, GLA_HEADS * c), 1) % c
    allow_f = si <= ti
    allow_b = si >= ti
    lk = lax.broadcasted_iota(jnp.int32, (c, GLA_KW), 1) // GLA_DK
    lv = lax.broadcasted_iota(jnp.int32, (c, GLA_WIDTH), 1) // GLA_DV
    hm_k = [lk == h for h in range(GLA_HEADS)]
    hm_v = [lv == h for h in range(GLA_HEADS)]
    hm_s = (lax.broadcasted_iota(jnp.int32, (GLA_WIDTH, GLA_KW), 0) // GLA_DV
            == lax.broadcasted_iota(jnp.int32, (GLA_WIDTH, GLA_KW), 1) // GLA_DK)

    sf_ref[...] = jnp.zeros_like(sf_ref)
    sb_ref[...] = jnp.zeros_like(sb_ref)
    acc_ref[...] = jnp.zeros_like(acc_ref)
    if ctx_out:
        accc_ref[...] = jnp.zeros_like(accc_ref)

    def sweep(qr, kr, vr, gfr, gbr, dst, count):
        def body(j, carry):
            lo_f = pl.multiple_of(j * c, c)
            lo_b = pl.multiple_of((count - 1 - j) * c, c)
            of = _gla_chunk(qr[pl.ds(lo_f, c), :], kr[pl.ds(lo_f, c), :], vr[pl.ds(lo_f, c), :],
                            gfr[pl.ds(lo_f, c), :], sf_ref, tri_f, allow_f, hm_k, hm_v, hm_s, c - 1)
            ob = _gla_chunk(qr[pl.ds(lo_b, c), :], kr[pl.ds(lo_b, c), :], vr[pl.ds(lo_b, c), :],
                            gbr[pl.ds(lo_b, c), :], sb_ref, tri_b, allow_b, hm_k, hm_v, hm_s, 0)
            if dst is not None:
                dst[pl.ds(lo_f, c), :] += of
                dst[pl.ds(lo_b, c), :] += ob
            return carry
        lax.fori_loop(0, count, body, 0)

    sweep(qc_ref, kc_ref, vc_ref, gfc_ref, gbc_ref, accc_ref, nc)
    sweep(q_ref, k_ref, v_ref, gf_ref, gb_ref, acc_ref, n)

    def finish(acc, gate, out, total):
        tr = min(256, total)

        def body(i, carry):
            lo = pl.multiple_of(i * tr, tr)
            o = acc[pl.ds(lo, tr), :]
            ms = _dot((o * o).astype(BF16), seg_ref[...]) * (1.0 / GLA_DV)
            gt = gate[pl.ds(lo, tr), :]
            y = o * lax.rsqrt(ms + EPS) * onorm_ref[...] * (gt * jax.nn.sigmoid(gt))
            out[pl.ds(lo, tr), :] = y.astype(out.dtype)
            return carry
        lax.fori_loop(0, total // tr, body, 0)

    finish(acc_ref, go_ref, ox_ref, t)
    if ctx_out:
        finish(accc_ref, goc_ref, oc_ref, tc)


def _gla(lat, ctx, onorm, seg64, ctx_out):
    b, t, _ = lat[0].shape
    tc = ctx[0].shape[1]
    spec = lambda a: pl.BlockSpec((None,) + a.shape[1:], lambda i: (i, 0, 0))
    full = lambda a: pl.BlockSpec(a.shape, lambda i: (0,) * a.ndim)
    out_shape = [jax.ShapeDtypeStruct((b, t, GLA_WIDTH), BF16)]
    scratch = [pltpu.VMEM((t, GLA_WIDTH), F32)]
    if ctx_out:
        out_shape.append(jax.ShapeDtypeStruct((b, tc, GLA_WIDTH), BF16))
        scratch.append(pltpu.VMEM((tc, GLA_WIDTH), F32))
    scratch += [pltpu.VMEM((GLA_WIDTH, GLA_KW), F32)] * 2
    outs = pl.pallas_call(
        functools.partial(_gla_kernel, ctx_out=ctx_out),
        grid=(b,),
        in_specs=[spec(a) for a in lat] + [spec(a) for a in ctx] + [full(onorm), full(seg64)],
        out_specs=[spec(s) for s in out_shape],
        out_shape=out_shape,
        scratch_shapes=scratch,
        compiler_params=_params(("parallel",)),
        name="gla_ctx_out" if ctx_out else "gla_last",
    )(*lat, *ctx, onorm, seg64)
    return outs if ctx_out else (outs[0], None)


def _outproj_kernel(a_ref, b_ref, c_ref, x_ref, gt_ref, sc_ref, sh_ref, gffn_ref,
                    wa_ref, wb_ref, wc_ref, wrh_ref, wrl_ref, br_ref, xo_ref, h_ref, lg_ref):
    mix = _dot(a_ref[...], wa_ref[...]) + _dot(b_ref[...], wb_ref[...]) + _dot(c_ref[...], wc_ref[...])
    xn = x_ref[...] + gt_ref[...] * mix
    xo_ref[...] = xn
    h = _rms(xn) * gffn_ref[...]
    h = h * (1.0 + sc_ref[...]) + sh_ref[...]
    h_hi, h_lo = _split(h)
    h_ref[...] = h_hi
    wrh = wrh_ref[...]
    lg_ref[...] = _dot(h_hi, wrh) + _dot(h_lo, wrh) + _dot(h_hi, wrl_ref[...]) + br_ref[...]


def _outproj(a, bm, c, x, gt, sc, sh, lw):
    b, t, d = x.shape
    tm = min(512, t)
    row = lambda w: pl.BlockSpec((None, tm, w), lambda i, j: (i, j, 0))
    vec = pl.BlockSpec((None, 1, d), lambda i, j: (i, 0, 0))
    full = lambda arr: pl.BlockSpec(arr.shape, lambda i, j: (0,) * arr.ndim)
    consts = [lw["gffn"], lw["wo_a"], lw["wo_b"], lw["wo_c"], lw["wr_hi"], lw["wr_lo"], lw["br"]]
    return pl.pallas_call(
        _outproj_kernel,
        grid=(b, t // tm),
        in_specs=[row(NA_WIDTH), row(MLA_WIDTH), row(GLA_WIDTH), row(d), vec, vec, vec] + [full(w) for w in consts],
        out_specs=[row(d), row(d), row(V7X_LANES)],
        out_shape=[jax.ShapeDtypeStruct((b, t, d), F32), jax.ShapeDtypeStruct((b, t, d), BF16),
                   jax.ShapeDtypeStruct((b, t, V7X_LANES), F32)],
        compiler_params=_params(("parallel", "parallel")),
        name="outproj_router",
    )(a, bm, c, x, gt, sc, sh, *consts)


def _moe_kernel(be_ref, nu_ref, x_ref, w1_ref, b1_ref, w2_ref, b2_ref, y_ref, w1b_ref, w2b_ref):
    i = pl.program_id(0)
    e = be_ref[i]
    prev = be_ref[jnp.maximum(i - 1, 0)]
    f = w2_ref.shape[0]

    @pl.when((i == 0) | (e != prev))
    def _():
        rows = 128

        def cast1(j, carry):
            lo = pl.multiple_of(j * rows, rows)
            w1b_ref[pl.ds(lo, rows), :] = w1_ref[pl.ds(lo, rows), :].astype(BF16)
            return carry

        def cast2(j, carry):
            lo = pl.multiple_of(j * rows, rows)
            w2b_ref[pl.ds(lo, rows), :] = w2_ref[pl.ds(lo, rows), :].astype(BF16)
            return carry
        lax.fori_loop(0, w1_ref.shape[0] // rows, cast1, 0)
        lax.fori_loop(0, w2_ref.shape[0] // rows, cast2, 0)

    @pl.when(i < nu_ref[0])
    def _():
        hb = _dot(x_ref[...], w1b_ref[...]) + b1_ref[...]
        glu = jnp.minimum(hb[:, :f], SWIGLU_LIMIT)
        lin = jnp.clip(hb[:, f:], -SWIGLU_LIMIT, SWIGLU_LIMIT)
        act = (lin + 1.0) * (glu * jax.nn.sigmoid(SWIGLU_ALPHA * glu))
        y_ref[...] = (_dot(act.astype(BF16), w2b_ref[...]) + b2_ref[...]).astype(y_ref.dtype)

    @pl.when(i >= nu_ref[0])
    def _():
        y_ref[...] = jnp.zeros_like(y_ref)


def _moe_blocks(buf, blk_e, n_used, w1, b1, w2, b2, layer):
    cap, d = buf.shape
    n_blk = cap // MOE_BLOCK
    f2 = w1.shape[-1]
    f = w2.shape[-2]
    grid_spec = pltpu.PrefetchScalarGridSpec(
        num_scalar_prefetch=2,
        grid=(n_blk,),
        in_specs=[
            pl.BlockSpec((MOE_BLOCK, d), lambda i, be, nu: (i, 0)),
            pl.BlockSpec((None, None, d, f2), lambda i, be, nu: (layer, be[i], 0, 0)),
            pl.BlockSpec((None, None, 1, f2), lambda i, be, nu: (layer, be[i], 0, 0)),
            pl.BlockSpec((None, None, f, d), lambda i, be, nu: (layer, be[i], 0, 0)),
            pl.BlockSpec((None, None, 1, d), lambda i, be, nu: (layer, be[i], 0, 0)),
        ],
        out_specs=pl.BlockSpec((MOE_BLOCK, d), lambda i, be, nu: (i, 0)),
        scratch_shapes=[pltpu.VMEM((d, f2), BF16), pltpu.VMEM((f, d), BF16)],
    )
    return pl.pallas_call(
        _moe_kernel,
        grid_spec=grid_spec,
        out_shape=jax.ShapeDtypeStruct((cap, d), BF16),
        compiler_params=_params(("arbitrary",), mib=56),
        name="moe_experts",
    )(blk_e, n_used, buf, w1, b1.reshape(b1.shape[0], b1.shape[1], 1, f2), w2, b2.reshape(b2.shape[0], b2.shape[1], 1, d))


def _moe(h, logits, w1, b1, w2, b2, layer):
    n, d = h.shape
    nk = n * TOP_K
    top_val, top_idx = lax.top_k(logits, TOP_K)
    gates = jax.nn.softmax(top_val, axis=-1)
    flat_e = top_idx.reshape(-1)
    flat_tok = jnp.repeat(jnp.arange(n, dtype=jnp.int32), TOP_K)
    order = jnp.argsort(flat_e)
    e_sorted = flat_e[order]
    counts = jnp.bincount(flat_e, length=N_EXPERTS)
    padded = (counts + MOE_BLOCK - 1) // MOE_BLOCK * MOE_BLOCK
    start = jnp.cumsum(counts) - counts
    pend = jnp.cumsum(padded)
    pstart = pend - padded
    dest_sorted = pstart[e_sorted] + jnp.arange(nk, dtype=jnp.int32) - start[e_sorted]
    dest = jnp.zeros_like(dest_sorted).at[order].set(dest_sorted)
    cap = -(-nk // MOE_BLOCK) * MOE_BLOCK + N_EXPERTS * MOE_BLOCK
    n_blk = cap // MOE_BLOCK
    row_tok = jnp.full((cap,), n, jnp.int32).at[dest].set(flat_tok)
    buf = h.at[row_tok].get(mode="fill", fill_value=0)
    blk_e = jnp.minimum(jnp.searchsorted(pend, jnp.arange(n_blk, dtype=pend.dtype) * MOE_BLOCK, side="right"),
                        N_EXPERTS - 1).astype(jnp.int32)
    n_used = (pend[-1] // MOE_BLOCK).astype(jnp.int32).reshape(1)
    y_buf = _moe_blocks(buf, blk_e, n_used, w1, b1, w2, b2, layer)
    y = y_buf[dest].reshape(n, TOP_K, d)
    return jnp.einsum("nk,nkd->nd", gates, y.astype(F32))


def _rope_tables(t):
    tok = jnp.arange(t)
    row = (tok // GRID_W).astype(F32)[:, None]
    col = (tok % GRID_W).astype(F32)[:, None]
    inv = 1.0 / (ROPE_THETA ** (jnp.arange(0, ROPE_AXIS, 2, dtype=F32) / ROPE_AXIS))
    ang = jnp.concatenate([row * inv, row * inv, col * inv, col * inv], axis=-1)
    cos = jnp.ones((t, V7X_LANES), F32).at[:, MLA_NOPE:MLA_QK].set(jnp.cos(ang))
    sin = jnp.zeros((t, V7X_LANES), F32).at[:, MLA_NOPE:MLA_QK].set(jnp.sin(ang))
    return cos, sin


def _pad_heads(w, per_head, offset=0):
    lead = w.shape[:-1]
    w = w.reshape(lead + (MLA_HEADS, per_head))
    out = jnp.zeros(lead + (MLA_HEADS, V7X_LANES), w.dtype).at[..., offset:offset + per_head].set(w)
    return out.reshape(lead + (MLA_PAD,))


def _layer_weights(l, g_mix, w_in, na_q_norm, na_k_norm, mla_q_a_norm, mla_w_q_b, mla_kv_a_norm, mla_w_kv_b,
                   mla_q_norm, mla_k_norm, gla_w_gate, gla_b_gate, gla_o_norm, w_out, g_ffn, w_router, b_router):
    d = w_in.shape[1]
    sizes = (NA_WIDTH, NA_WIDTH, NA_WIDTH, MLA_Q_RANK, MLA_KV_RANK, MLA_ROPE, GLA_KW, GLA_KW, GLA_WIDTH, GLA_WIDTH,
             GLA_GATE_RANK, GLA_GATE_RANK)
    (wq, wk, wv, wcq, wckv, wkr, wgq, wgk, wgv, wgo, wlf, wlb) = jnp.split(w_in[l], np.cumsum(sizes)[:-1].tolist(), axis=-1)
    small = jnp.zeros((d, V7X_LANES), F32)
    small = small.at[:, S_LF:S_LF + GLA_GATE_RANK].set(wlf).at[:, S_LB:S_LB + GLA_GATE_RANK].set(wlb)
    small = small.at[:, S_KR:S_KR + MLA_ROPE].set(wkr)
    w_packed = jnp.concatenate([wq, wk, wv, wcq, wckv, wgq, wgk, wgv, wgo, small], axis=-1).astype(BF16)
    kvb = mla_w_kv_b[l].reshape(MLA_KV_RANK, MLA_HEADS, MLA_NOPE + MLA_V)
    wg = jnp.zeros((V7X_LANES, 2 * GLA_KW), F32)
    wg = wg.at[S_LF:S_LF + GLA_GATE_RANK, :GLA_KW].set(gla_w_gate[l, 0])
    wg = wg.at[S_LB:S_LB + GLA_GATE_RANK, GLA_KW:].set(gla_w_gate[l, 1])
    wg_hi, wg_lo = _split(wg)
    seg = (np.arange(NA_WIDTH)[:, None] // NA_DIM == np.arange(NA_WIDTH)[None, :] // NA_DIM)
    wr = jnp.zeros((d, V7X_LANES), F32).at[:, :N_EXPERTS].set(w_router[l])
    wr_hi, wr_lo = _split(wr)
    wo = w_out[l].astype(BF16)
    return {
        "gmix": g_mix[l][None, :],
        "w_in": w_packed,
        "naqn": jnp.tile(na_q_norm[l], NA_HEADS)[None, :],
        "nakn": jnp.tile(na_k_norm[l], NA_HEADS)[None, :],
        "qan": mla_q_a_norm[l][None, :],
        "wqb": _pad_heads(mla_w_q_b[l], MLA_QK).astype(BF16),
        "kvan": mla_kv_a_norm[l][None, :],
        "wkk": _pad_heads(kvb[:, :, :MLA_NOPE].reshape(MLA_KV_RANK, -1), MLA_NOPE).astype(BF16),
        "wkv": kvb[:, :, MLA_NOPE:].reshape(MLA_KV_RANK, MLA_WIDTH).astype(BF16),
        "mqn": _pad_heads(jnp.tile(mla_q_norm[l], MLA_HEADS), MLA_QK)[None, :],
        "mkn": _pad_heads(jnp.tile(mla_k_norm[l], MLA_HEADS), MLA_QK)[None, :],
        "wg_hi": wg_hi, "wg_lo": wg_lo,
        "bg": jnp.concatenate([gla_b_gate[l, 0], gla_b_gate[l, 1]])[None, :],
        "seg64": jnp.asarray(seg, BF16),
        "onorm": jnp.tile(gla_o_norm[l], GLA_HEADS)[None, :],
        "gffn": g_ffn[l][None, :],
        "wo_a": wo[:NA_WIDTH], "wo_b": wo[NA_WIDTH:NA_WIDTH + MLA_WIDTH], "wo_c": wo[NA_WIDTH + MLA_WIDTH:],
        "wr_hi": wr_hi, "wr_lo": wr_lo,
        "br": jnp.zeros((1, V7X_LANES), F32).at[0, :N_EXPERTS].set(b_router[l]),
    }


def kernel(x, c, ctx, c_ctx, w_ada, b_ada, g_mix, w_in, na_q_norm, na_k_norm, na_rpb, mla_q_a_norm, mla_w_q_b,
           mla_kv_a_norm, mla_w_kv_b, mla_q_norm, mla_k_norm, gla_w_gate, gla_b_gate, gla_o_norm, w_out, g_ffn,
           w_router, b_router, w_moe1, b_moe1, w_moe2, b_moe2):
    bsz, t, d = x.shape
    tc = ctx.shape[1]
    depth = w_ada.shape[0]
    rows = t // GRID_W
    cos, sin = _rope_tables(t)

    pad = (-(bsz + 1)) % 8
    cvec = jnp.concatenate([c, c_ctx[None, :], jnp.zeros((pad, d), F32)], axis=0)
    mod = _ada(cvec, w_ada, b_ada)

    h_ctx = ctx
    for l in range(depth):
        ctx_out = l < depth - 1
        lw = _layer_weights(l, g_mix, w_in, na_q_norm, na_k_norm, mla_q_a_norm, mla_w_q_b, mla_kv_a_norm,
                            mla_w_kv_b, mla_q_norm, mla_k_norm, gla_w_gate, gla_b_gate, gla_o_norm, w_out, g_ffn,
                            w_router, b_router)
        m_lat = [m[:, None, :] for m in jnp.split(mod[l, :bsz], 6, axis=-1)]
        m_ctx = [jnp.broadcast_to(m[None, :, :], (bsz, 1, d)) for m in jnp.split(mod[l, bsz:bsz + 1], 6, axis=-1)]
        sh1, sc1, gt1, sh2, sc2, gt2 = m_lat
        csh1, csc1, cgt1, csh2, csc2, cgt2 = m_ctx

        (naq, nak, nav, mq, mk, mv, gq, gk, gv, gout, gf, gb) = _inproj(x, sc1, sh1, lw, cos, sin, True)
        (cnaq, cnak, cnav, cmq, cmk, cmv, cgq, cgk, cgv, cgout, cgf, cgb) = _inproj(
            h_ctx, csc1, csh1, lw, cos[:tc], sin[:tc], False)

        bias = _na_bias_table(na_rpb[l], rows)
        a_x = _na_attention(naq, nak, nav, cnak, cnav, bias)
        b_x = _attention(mq, [(mk, mv), (cmk, cmv)], MLA_HEADS, V7X_LANES, MLA_V, 256)
        c_x, c_c = _gla((gq, gk, gv, gf, gb, gout), (cgq, cgk, cgv, cgf, cgb, cgout), lw["onorm"], lw["seg64"], ctx_out)
        x_new, h2, logits = _outproj(a_x, b_x, c_x, x, gt1, sc2, sh2, lw)

        toks = [h2.reshape(bsz * t, d)]
        lgs = [logits.reshape(bsz * t, V7X_LANES)]
        if ctx_out:
            a_c = _attention_masked(cnaq, cnak, cnav, NA_DIM)
            b_c = _attention(cmq, [(cmk, cmv)], MLA_HEADS, V7X_LANES, MLA_V, 256)
            hc_new, hc2, clogits = _outproj(a_c, b_c, c_c, h_ctx, cgt1, csc2, csh2, lw)
            toks.append(hc2.reshape(bsz * tc, d))
            lgs.append(clogits.reshape(bsz * tc, V7X_LANES))
        moe = _moe(jnp.concatenate(toks, axis=0), jnp.concatenate(lgs, axis=0)[:, :N_EXPERTS],
                   w_moe1, b_moe1, w_moe2, b_moe2, l)
        x = x_new + gt2 * moe[:bsz * t].reshape(bsz, t, d)
        if ctx_out:
            h_ctx = hc_new + cgt2 * moe[bsz * t:].reshape(bsz, tc, d)
    return x
```

```python
import functools

import numpy as np
import jax
import jax.numpy as jnp
from jax import lax
from jax.experimental import pallas as pl
from jax.experimental.pallas import tpu as pltpu

F32 = jnp.float32
BF16 = jnp.bfloat16

V7X_LANES = 128
V7X_VMEM_BYTES = 64 * 1024 * 1024

EPS = 1e-6
GRID_W = 64
NA_HEADS, NA_DIM, NA_KH, NA_KW = 4, 64, 8, 16
MLA_HEADS, MLA_NOPE, MLA_ROPE, MLA_V = 4, 64, 32, 128
MLA_QK = MLA_NOPE + MLA_ROPE
MLA_Q_RANK, MLA_KV_RANK = 256, 128
ROPE_AXIS = MLA_ROPE // 2
ROPE_THETA = 10000.0
GLA_HEADS, GLA_DK, GLA_DV = 4, 32, 64
GLA_GATE_RANK, GLA_GATE_NORM, GLA_CHUNK = 16, 16.0, 64
NA_WIDTH = NA_HEADS * NA_DIM
MLA_WIDTH = MLA_HEADS * MLA_V
MLA_PAD = MLA_HEADS * V7X_LANES
GLA_KW = GLA_HEADS * GLA_DK
GLA_WIDTH = GLA_HEADS * GLA_DV
N_EXPERTS, TOP_K = 32, 4
SWIGLU_LIMIT, SWIGLU_ALPHA = 7.0, 1.702
MOE_BLOCK = 512

C_NAQ, C_NAK, C_NAV, C_CQ, C_CKV = 0, 256, 512, 768, 1024
C_GQ, C_GK, C_GV, C_GOUT, C_SMALL = 1152, 1280, 1408, 1664, 1920
IN_PACKED = 2048
S_LF, S_LB, S_KR = 0, 16, 64


def _vmem_limit(mib):
    return min(mib * 1024 * 1024, V7X_VMEM_BYTES - 4 * 1024 * 1024)


def _params(sem, mib=48):
    return pltpu.CompilerParams(dimension_semantics=sem, vmem_limit_bytes=_vmem_limit(mib))


def _dot(a, b):
    return jnp.dot(a, b, preferred_element_type=F32)


def _dot_nt(a, b):
    return lax.dot_general(a, b, (((1,), (1,)), ((), ())), preferred_element_type=F32)


def _dot_tn(a, b):
    return lax.dot_general(a, b, (((0,), (0,)), ((), ())), preferred_element_type=F32)


def _split(x):
    hi = x.astype(BF16)
    lo = (x - hi.astype(F32)).astype(BF16)
    return hi, lo


def _rms(x):
    return x * lax.rsqrt(jnp.mean(x * x, axis=-1, keepdims=True) + EPS)


def _ada_kernel(c_ref, w_ref, b_ref, o_ref):
    c = c_ref[...]
    s = (c * jax.nn.sigmoid(c)).astype(BF16)
    o_ref[...] = _dot(s, w_ref[...].astype(BF16)) + b_ref[...]


def _ada(cvec, w_ada, b_ada):
    depth, d, n6 = w_ada.shape
    r = cvec.shape[0]
    tn = 1024
    return pl.pallas_call(
        _ada_kernel,
        grid=(depth, n6 // tn),
        in_specs=[
            pl.BlockSpec((r, d), lambda l, j: (0, 0)),
            pl.BlockSpec((None, d, tn), lambda l, j: (l, 0, j)),
            pl.BlockSpec((None, 1, tn), lambda l, j: (l, 0, j)),
        ],
        out_specs=pl.BlockSpec((None, r, tn), lambda l, j: (l, 0, j)),
        out_shape=jax.ShapeDtypeStruct((depth, r, n6), F32),
        compiler_params=_params(("parallel", "parallel")),
        name="ada_modulation",
    )(cvec, w_ada, b_ada.reshape(depth, 1, n6))


def _rope(x, cos, sin):
    lane = lax.broadcasted_iota(jnp.int32, (x.shape[0], V7X_LANES), 1)
    first = (lane & (ROPE_AXIS - 1)) < (ROPE_AXIS // 2)
    outs = []
    for h in range(MLA_HEADS):
        xs = x[:, V7X_LANES * h:V7X_LANES * (h + 1)]
        rot = jnp.where(first, -pltpu.roll(xs, V7X_LANES - ROPE_AXIS // 2, 1), pltpu.roll(xs, ROPE_AXIS // 2, 1))
        outs.append(xs * cos + rot * sin)
    return jnp.concatenate(outs, axis=1)


def _head_rms_padded(x, n_real):
    outs = []
    for h in range(MLA_HEADS):
        xs = x[:, V7X_LANES * h:V7X_LANES * (h + 1)]
        ms = jnp.sum(xs * xs, axis=-1, keepdims=True) * (1.0 / n_real)
        outs.append(xs * lax.rsqrt(ms + EPS))
    return jnp.concatenate(outs, axis=1)


def _log_sigmoid(x):
    return jnp.minimum(x, 0.0) - jnp.log1p(jnp.exp(-jnp.abs(x)))


def _inproj_kernel(x_ref, sc_ref, sh_ref, gmix_ref, w_ref, cos_ref, sin_ref,
                   naqn_ref, nakn_ref, qan_ref, wqb_ref, kvan_ref, wkk_ref, wkv_ref, mqn_ref, mkn_ref,
                   wgh_ref, wgl_ref, bg_ref, seg_ref,
                   naq_o, nak_o, nav_o, mq_o, mk_o, mv_o, gq_o, gk_o, gv_o, gout_o, gf_o, gb_o, *, use_rope):
    x = x_ref[...]
    h = _rms(x) * gmix_ref[...]
    h = h * (1.0 + sc_ref[...]) + sh_ref[...]
    p = _dot(h.astype(BF16), w_ref[...])

    seg = seg_ref[...]
    q = p[:, C_NAQ:C_NAQ + NA_WIDTH]
    k = p[:, C_NAK:C_NAK + NA_WIDTH]
    qss = _dot((q * q).astype(BF16), seg) * (1.0 / NA_DIM)
    kss = _dot((k * k).astype(BF16), seg) * (1.0 / NA_DIM)
    naq_o[...] = (q * lax.rsqrt(qss + EPS) * naqn_ref[...] * (NA_DIM ** -0.5)).astype(BF16)
    nak_o[...] = (k * lax.rsqrt(kss + EPS) * nakn_ref[...]).astype(BF16)
    nav_o[...] = p[:, C_NAV:C_NAV + NA_WIDTH].astype(BF16)

    small = p[:, C_SMALL:C_SMALL + V7X_LANES]
    cq = _rms(p[:, C_CQ:C_CQ + MLA_Q_RANK]) * qan_ref[...]
    mq = _head_rms_padded(_dot(cq.astype(BF16), wqb_ref[...]), MLA_QK) * mqn_ref[...]
    ckv = (_rms(p[:, C_CKV:C_CKV + MLA_KV_RANK]) * kvan_ref[...]).astype(BF16)
    lane = lax.broadcasted_iota(jnp.int32, small.shape, 1)
    kr = jnp.where((lane >= S_KR) & (lane < S_KR + MLA_ROPE), small, 0.0)
    mk = _dot(ckv, wkk_ref[...]) + jnp.concatenate([kr] * MLA_HEADS, axis=1)
    mk = _head_rms_padded(mk, MLA_QK) * mkn_ref[...]
    if use_rope:
        cos, sin = cos_ref[...], sin_ref[...]
        mq = _rope(mq, cos, sin)
        mk = _rope(mk, cos, sin)
    mq_o[...] = (mq * (MLA_QK ** -0.5)).astype(BF16)
    mk_o[...] = mk.astype(BF16)
    mv_o[...] = _dot(ckv, wkv_ref[...]).astype(BF16)

    gq_o[...] = p[:, C_GQ:C_GQ + GLA_KW] * (GLA_DK ** -0.5)
    gk_o[...] = p[:, C_GK:C_GK + GLA_KW]
    gv_o[...] = p[:, C_GV:C_GV + GLA_WIDTH]
    gout_o[...] = p[:, C_GOUT:C_GOUT + GLA_WIDTH]
    s_hi, s_lo = _split(small)
    wgh = wgh_ref[...]
    pre = _dot(s_hi, wgh) + _dot(s_lo, wgh) + _dot(s_hi, wgl_ref[...]) + bg_ref[...]
    ls = _log_sigmoid(pre) * (1.0 / GLA_GATE_NORM)
    gf_o[...] = ls[:, :GLA_KW]
    gb_o[...] = ls[:, GLA_KW:]


def _inproj(x, sc, sh, lw, cos, sin, use_rope):
    b, t, d = x.shape
    tm = min(512, t)
    full = lambda a: pl.BlockSpec(a.shape, lambda i, j: (0,) * a.ndim)
    row = lambda w: pl.BlockSpec((None, tm, w), lambda i, j: (i, j, 0))
    consts = [lw["gmix"], lw["w_in"]]
    tail = [lw["naqn"], lw["nakn"], lw["qan"], lw["wqb"], lw["kvan"], lw["wkk"], lw["wkv"], lw["mqn"], lw["mkn"],
            lw["wg_hi"], lw["wg_lo"], lw["bg"], lw["seg64"]]
    widths = [(NA_WIDTH, BF16)] * 3 + [(MLA_PAD, BF16), (MLA_PAD, BF16), (MLA_WIDTH, BF16),
                                       (GLA_KW, F32), (GLA_KW, F32), (GLA_WIDTH, F32), (GLA_WIDTH, F32),
                                       (GLA_KW, F32), (GLA_KW, F32)]
    return pl.pallas_call(
        functools.partial(_inproj_kernel, use_rope=use_rope),
        grid=(b, t // tm),
        in_specs=[row(d),
                  pl.BlockSpec((None, 1, d), lambda i, j: (i, 0, 0)),
                  pl.BlockSpec((None, 1, d), lambda i, j: (i, 0, 0))]
                 + [full(a) for a in consts]
                 + [pl.BlockSpec((tm, V7X_LANES), lambda i, j: (j, 0))] * 2
                 + [full(a) for a in tail],
        out_specs=[row(w) for w, _ in widths],
        out_shape=[jax.ShapeDtypeStruct((b, t, w), dt) for w, dt in widths],
        compiler_params=_params(("parallel", "parallel")),
        name="inproj_rope" if use_rope else "inproj_ctx",
    )(x, sc, sh, *consts, cos, sin, *tail)


def _stack_heads(q, width):
    lane = lax.broadcasted_iota(jnp.int32, q.shape, 1)
    zero = jnp.zeros_like(q)
    return jnp.concatenate([jnp.where(lane // width == h, q, zero) for h in range(NA_HEADS)], axis=0)


def _unstack_heads(o4, n, width):
    lane = lax.broadcasted_iota(jnp.int32, (n, o4.shape[1]), 1)
    out = jnp.zeros((n, o4.shape[1]), F32)
    for h in range(NA_HEADS):
        out = out + jnp.where(lane // width == h, o4[h * n:(h + 1) * n], 0.0)
    return out


def _na_kernel(q_ref, k_ref, v_ref, kc_ref, vc_ref, bias_ref, o_ref, *, rows):
    kc = kc_ref[...]
    vc = vc_ref[...]
    n_loc = NA_KH * GRID_W

    def body(r, carry):
        rs = jnp.clip(r - NA_KH // 2, 0, rows - NA_KH)
        off = r - rs
        q = q_ref[pl.ds(pl.multiple_of(r * GRID_W, GRID_W), GRID_W), :]
        ks = k_ref[pl.ds(pl.multiple_of(rs * GRID_W, GRID_W), n_loc), :]
        vs = v_ref[pl.ds(pl.multiple_of(rs * GRID_W, GRID_W), n_loc), :]
        q4 = _stack_heads(q, NA_DIM)
        s_loc = _dot_nt(q4, ks) + bias_ref[off]
        s_ctx = _dot_nt(q4, kc)
        m = jnp.maximum(jnp.max(s_loc, axis=-1, keepdims=True), jnp.max(s_ctx, axis=-1, keepdims=True))
        p_loc = jnp.exp(s_loc - m)
        p_ctx = jnp.exp(s_ctx - m)
        denom = jnp.sum(p_loc, axis=-1, keepdims=True) + jnp.sum(p_ctx, axis=-1, keepdims=True)
        o4 = (_dot(p_loc.astype(BF16), vs) + _dot(p_ctx.astype(BF16), vc)) * (1.0 / denom)
        o_ref[pl.ds(pl.multiple_of(r * GRID_W, GRID_W), GRID_W), :] = _unstack_heads(o4, GRID_W, NA_DIM).astype(BF16)
        return carry

    lax.fori_loop(0, rows, body, 0)


def _na_bias_table(rpb, rows):
    kh = NA_KH
    r_all = np.arange(rows)
    offs = r_all - np.clip(r_all - kh // 2, 0, rows - kh)
    n_off = int(offs.max()) + 1
    qc = np.arange(GRID_W)
    kcol = np.arange(GRID_W)
    cs = np.clip(qc - NA_KW // 2, 0, GRID_W - NA_KW)
    in_win = (kcol[None, :] >= cs[:, None]) & (kcol[None, :] < cs[:, None] + NA_KW)
    dc = np.clip(kcol[None, :] - qc[:, None] + NA_KW - 1, 0, 2 * NA_KW - 2)
    onehot = (dc[None] == np.arange(2 * NA_KW - 1)[:, None, None]).astype(np.float32)
    toep = jnp.einsum("hdc,cqk->hdqk", rpb.astype(F32), jnp.asarray(onehot), precision=lax.Precision.HIGHEST)
    toep = jnp.where(in_win[None, None], toep, -jnp.inf)
    per_off = [toep[:, NA_KH - 1 - off:2 * NA_KH - 1 - off] for off in range(n_off)]
    bias = jnp.stack(per_off, axis=0)
    return bias.transpose(0, 1, 3, 2, 4).reshape(n_off, NA_HEADS * GRID_W, kh * GRID_W)


def _na_attention(q, k, v, kc, vc, bias):
    b, t, w = q.shape
    tc = kc.shape[1]
    rows = t // GRID_W
    assert rows >= NA_KH and t % GRID_W == 0
    seq = lambda n: pl.BlockSpec((None, n, w), lambda i: (i, 0, 0))
    return pl.pallas_call(
        functools.partial(_na_kernel, rows=rows),
        grid=(b,),
        in_specs=[seq(t), seq(t), seq(t), seq(tc), seq(tc),
                  pl.BlockSpec(bias.shape, lambda i: (0, 0, 0))],
        out_specs=seq(t),
        out_shape=jax.ShapeDtypeStruct((b, t, w), BF16),
        compiler_params=_params(("parallel",)),
        name="na_attention",
    )(q, k, v, kc, vc, bias)


def _softmax_pv(scores, values):
    m = functools.reduce(jnp.maximum, [jnp.max(s, axis=-1, keepdims=True) for s in scores])
    ps = [jnp.exp(s - m) for s in scores]
    denom = functools.reduce(jnp.add, [jnp.sum(p, axis=-1, keepdims=True) for p in ps])
    o = functools.reduce(jnp.add, [_dot(p.astype(BF16), v) for p, v in zip(ps, values)])
    return o * (1.0 / denom)


def _attn_sliced_kernel(*refs, n_parts, heads, dh, dv):
    q_ref, o_ref = refs[0], refs[-1]
    for h in range(heads):
        q = q_ref[:, dh * h:dh * (h + 1)]
        scores = [_dot_nt(q, refs[1 + 2 * p][:, dh * h:dh * (h + 1)]) for p in range(n_parts)]
        values = [refs[2 + 2 * p][:, dv * h:dv * (h + 1)] for p in range(n_parts)]
        o_ref[:, dv * h:dv * (h + 1)] = _softmax_pv(scores, values).astype(o_ref.dtype)


def _attn_masked_kernel(q_ref, k_ref, v_ref, o_ref, *, width):
    n = q_ref.shape[0]
    q4 = _stack_heads(q_ref[...], width)
    o4 = _softmax_pv([_dot_nt(q4, k_ref[...])], [v_ref[...]])
    o_ref[...] = _unstack_heads(o4, n, width).astype(o_ref.dtype)


def _attention(q, parts, heads, dh, dv, tq):
    b, t, wq = q.shape
    tq = min(tq, t)
    flat = [a for kv in parts for a in kv]
    kv_spec = lambda a: pl.BlockSpec((None,) + a.shape[1:], lambda i, j: (i, 0, 0))
    return pl.pallas_call(
        functools.partial(_attn_sliced_kernel, n_parts=len(parts), heads=heads, dh=dh, dv=dv),
        grid=(b, t // tq),
        in_specs=[pl.BlockSpec((None, tq, wq), lambda i, j: (i, j, 0))] + [kv_spec(a) for a in flat],
        out_specs=pl.BlockSpec((None, tq, heads * dv), lambda i, j: (i, j, 0)),
        out_shape=jax.ShapeDtypeStruct((b, t, heads * dv), BF16),
        compiler_params=_params(("parallel", "parallel")),
        name="mla_attention_%d" % len(parts),
    )(q, *flat)


def _attention_masked(q, k, v, width):
    b, t, w = q.shape
    spec = lambda a: pl.BlockSpec((None,) + a.shape[1:], lambda i: (i, 0, 0))
    return pl.pallas_call(
        functools.partial(_attn_masked_kernel, width=width),
        grid=(b,),
        in_specs=[spec(q), spec(k), spec(v)],
        out_specs=spec(q),
        out_shape=jax.ShapeDtypeStruct((b, t, w), BF16),
        compiler_params=_params(("parallel",)),
        name="ctx_na_attention",
    )(q, k, v)


def _gla_chunk(q, k, v, g, st_ref, tri, allowed, hm_k, hm_v, hm_s, end_row):
    g_hi, g_lo = _split(g)
    bcum = _dot(tri, g_hi) + _dot(tri, g_lo)
    bend = bcum[end_row:end_row + 1, :]
    qe = (q * jnp.exp(bcum)).astype(BF16)
    ke = (k * jnp.exp(bend - bcum)).astype(BF16)
    kt = k * jnp.exp(-bcum)
    st = st_ref[...]
    o = _dot_nt(qe, st.astype(BF16))
    kst = jnp.concatenate([jnp.where(m, kt, 0.0) for m in hm_k], axis=0).astype(BF16)
    a = jnp.where(allowed, _dot_nt(qe, kst), 0.0)
    vbd = jnp.concatenate([jnp.where(m, v, 0.0) for m in hm_v], axis=0).astype(BF16)
    o = o + _dot(a.astype(BF16), vbd)
    upd = _dot_tn(v.astype(BF16), ke)
    st_ref[...] = st * jnp.exp(bend) + jnp.where(hm_s, upd, 0.0)
    return o


def _gla_kernel(q_ref, k_ref, v_ref, gf_ref, gb_ref, go_ref,
                qc_ref, kc_ref, vc_ref, gfc_ref, gbc_ref, goc_ref, onorm_ref, seg_ref,
                *rest, ctx_out):
    if ctx_out:
        ox_ref, oc_ref, acc_ref, accc_ref, sf_ref, sb_ref = rest
    else:
        ox_ref, acc_ref, sf_ref, sb_ref = rest
        oc_ref = accc_ref = None
    c = GLA_CHUNK
    t, tc = q_ref.shape[0], qc_ref.shape[0]
    n, nc = t // c, tc // c

    ri = lax.broadcasted_iota(jnp.int32, (c, c), 0)
    ci = lax.broadcasted_iota(jnp.int32, (c, c), 1)
    tri_f = (ci <= ri).astype(BF16)
    tri_b = (ci >= ri).astype(BF16)
    ti = lax.broadcasted_iota(jnp.int32, (c, GLA_HEADS * c), 0)
    si = lax.broadcasted_iota(jnp.int32, (c, GLA_HEADS * c), 1) % c
    allow_f = si <= ti
    allow_b = si >= ti
    lk = lax.broadcasted_iota(jnp.int32, (c, GLA_KW), 1) // GLA_DK
    lv = lax.broadcasted_iota(jnp.int32, (c, GLA_WIDTH), 1) // GLA_DV
    hm_k = [lk == h for h in range(GLA_HEADS)]
    hm_v = [lv == h for h in range(GLA_HEADS)]
    hm_s = (lax.broadcasted_iota(jnp.int32, (GLA_WIDTH, GLA_KW), 0) // GLA_DV
            == lax.broadcasted_iota(jnp.int32, (GLA_WIDTH, GLA_KW), 1) // GLA_DK)

    sf_ref[...] = jnp.zeros_like(sf_ref)
    sb_ref[...] = jnp.zeros_like(sb_ref)
    acc_ref[...] = jnp.zeros_like(acc_ref)
    if ctx_out:
        accc_ref[...] = jnp.zeros_like(accc_ref)

    def sweep(qr, kr, vr, gfr, gbr, dst, count):
        def body(j, carry):
            lo_f = pl.multiple_of(j * c, c)
            lo_b = pl.multiple_of((count - 1 - j) * c, c)
            of = _gla_chunk(qr[pl.ds(lo_f, c), :], kr[pl.ds(lo_f, c), :], vr[pl.ds(lo_f, c), :],
                            gfr[pl.ds(lo_f, c), :], sf_ref, tri_f, allow_f, hm_k, hm_v, hm_s, c - 1)
            ob = _gla_chunk(qr[pl.ds(lo_b, c), :], kr[pl.ds(lo_b, c), :], vr[pl.ds(lo_b, c), :],
                            gbr[pl.ds(lo_b, c), :], sb_ref, tri_b, allow_b, hm_k, hm_v, hm_s, 0)
            if dst is not None:
                dst[pl.ds(lo_f, c), :] += of
                dst[pl.ds(lo_b, c), :] += ob
            return carry
        lax.fori_loop(0, count, body, 0)

    sweep(qc_ref, kc_ref, vc_ref, gfc_ref, gbc_ref, accc_ref, nc)
    sweep(q_ref, k_ref, v_ref, gf_ref, gb_ref, acc_ref, n)

    def finish(acc, gate, out, total):
        tr = min(256, total)

        def body(i, carry):
            lo = pl.multiple_of(i * tr, tr)
            o = acc[pl.ds(lo, tr), :]
            ms = _dot((o * o).astype(BF16), seg_ref[...]) * (1.0 / GLA_DV)
            gt = gate[pl.ds(lo, tr), :]
            y = o * lax.rsqrt(ms + EPS) * onorm_ref[...] * (gt * jax.nn.sigmoid(gt))
            out[pl.ds(lo, tr), :] = y.astype(out.dtype)
            return carry
        lax.fori_loop(0, total // tr, body, 0)

    finish(acc_ref, go_ref, ox_ref, t)
    if ctx_out:
        finish(accc_ref, goc_ref, oc_ref, tc)


def _gla(lat, ctx, onorm, seg64, ctx_out):
    b, t, _ = lat[0].shape
    tc = ctx[0].shape[1]
    spec = lambda a: pl.BlockSpec((None,) + a.shape[1:], lambda i: (i, 0, 0))
    full = lambda a: pl.BlockSpec(a.shape, lambda i: (0,) * a.ndim)
    out_shape = [jax.ShapeDtypeStruct((b, t, GLA_WIDTH), BF16)]
    scratch = [pltpu.VMEM((t, GLA_WIDTH), F32)]
    if ctx_out:
        out_shape.append(jax.ShapeDtypeStruct((b, tc, GLA_WIDTH), BF16))
        scratch.append(pltpu.VMEM((tc, GLA_WIDTH), F32))
    scratch += [pltpu.VMEM((GLA_WIDTH, GLA_KW), F32)] * 2
    outs = pl.pallas_call(
        functools.partial(_gla_kernel, ctx_out=ctx_out),
        grid=(b,),
        in_specs=[spec(a) for a in lat] + [spec(a) for a in ctx] + [full(onorm), full(seg64)],
        out_specs=[spec(s) for s in out_shape],
        out_shape=out_shape,
        scratch_shapes=scratch,
        compiler_params=_params(("parallel",)),
        name="gla_ctx_out" if ctx_out else "gla_last",
    )(*lat, *ctx, onorm, seg64)
    return outs if ctx_out else (outs[0], None)


def _outproj_kernel(a_ref, b_ref, c_ref, x_ref, gt_ref, sc_ref, sh_ref, gffn_ref,
                    wa_ref, wb_ref, wc_ref, wrh_ref, wrl_ref, br_ref, xo_ref, h_ref, lg_ref):
    mix = _dot(a_ref[...], wa_ref[...]) + _dot(b_ref[...], wb_ref[...]) + _dot(c_ref[...], wc_ref[...])
    xn = x_ref[...] + gt_ref[...] * mix
    xo_ref[...] = xn
    h = _rms(xn) * gffn_ref[...]
    h = h * (1.0 + sc_ref[...]) + sh_ref[...]
    h_ref[...] = h
    h_hi, h_lo = _split(h)
    wrh = wrh_ref[...]
    lg_ref[...] = _dot(h_hi, wrh) + _dot(h_lo, wrh) + _dot(h_hi, wrl_ref[...]) + br_ref[...]


def _outproj(a, bm, c, x, gt, sc, sh, lw):
    b, t, d = x.shape
    tm = min(512, t)
    row = lambda w: pl.BlockSpec((None, tm, w), lambda i, j: (i, j, 0))
    vec = pl.BlockSpec((None, 1, d), lambda i, j: (i, 0, 0))
    full = lambda arr: pl.BlockSpec(arr.shape, lambda i, j: (0,) * arr.ndim)
    consts = [lw["gffn"], lw["wo_a"], lw["wo_b"], lw["wo_c"], lw["wr_hi"], lw["wr_lo"], lw["br"]]
    return pl.pallas_call(
        _outproj_kernel,
        grid=(b, t // tm),
        in_specs=[row(NA_WIDTH), row(MLA_WIDTH), row(GLA_WIDTH), row(d), vec, vec, vec] + [full(w) for w in consts],
        out_specs=[row(d), row(d), row(V7X_LANES)],
        out_shape=[jax.ShapeDtypeStruct((b, t, d), F32), jax.ShapeDtypeStruct((b, t, d), F32),
                   jax.ShapeDtypeStruct((b, t, V7X_LANES), F32)],
        compiler_params=_params(("parallel", "parallel")),
        name="outproj_router",
    )(a, bm, c, x, gt, sc, sh, *consts)


IDX_RING = 3
ROW_UNROLL = 8


def _moe_kernel(be_ref, nu_ref, idx_hbm, h_hbm, w1_ref, b1_ref, w2_ref, b2_ref, y_hbm,
                idx_s, xbuf, ybuf, w1b_ref, w2b_ref, isem, gsem, ssem):
    i = pl.program_id(0)
    last = pl.num_programs(0) - 1
    nu = nu_ref[0]
    blk = xbuf.shape[1]
    n_tok = h_hbm.shape[0]
    slot = i % 2
    other = 1 - slot
    f = w2_ref.shape[0]

    def idx_copy(b):
        s = b % IDX_RING
        return pltpu.make_async_copy(idx_hbm.at[pl.ds(pl.multiple_of(b * blk, blk), blk)], idx_s.at[s], isem.at[s])

    def gather_copy(b, r, s):
        tok = jnp.minimum(lax.shift_right_logical(idx_s[b % IDX_RING, r], 2), n_tok - 1)
        return pltpu.make_async_copy(h_hbm.at[tok], xbuf.at[s, r], gsem.at[s])

    def scatter_copy(b, r, s):
        return pltpu.make_async_copy(ybuf.at[s, r], y_hbm.at[idx_s[b % IDX_RING, r]], ssem.at[s])

    def start_gathers(b, s):
        def body(r, carry):
            gather_copy(b, r, s).start()
            return carry
        lax.fori_loop(0, blk, body, 0, unroll=ROW_UNROLL)

    def wait_gathers(s):
        def body(r, carry):
            pltpu.make_async_copy(h_hbm.at[0], xbuf.at[s, r], gsem.at[s]).wait()
            return carry
        lax.fori_loop(0, blk, body, 0, unroll=ROW_UNROLL)

    def wait_scatters(s):
        def body(r, carry):
            pltpu.make_async_copy(ybuf.at[s, r], y_hbm.at[0], ssem.at[s]).wait()
            return carry
        lax.fori_loop(0, blk, body, 0, unroll=ROW_UNROLL)

    @pl.when(i == 0)
    def _():
        ybuf[...] = jnp.zeros_like(ybuf)
        first_spare = y_hbm.shape[0] - 2 * blk
        for s in range(2):
            spare = pltpu.make_async_copy(ybuf.at[s], y_hbm.at[pl.ds(first_spare + s * blk, blk)], ssem.at[s])
            spare.start()
            spare.wait()
        idx_copy(0).start()
        idx_copy(0).wait()
        start_gathers(0, 0)

        @pl.when(nu > 1)
        def _():
            idx_copy(1).start()

    @pl.when(i + 1 < nu)
    def _():
        idx_copy(i + 1).wait()
        start_gathers(i + 1, other)

    @pl.when(i + 2 < nu)
    def _():
        idx_copy(i + 2).start()

    e = be_ref[i]
    prev = be_ref[jnp.maximum(i - 1, 0)]

    @pl.when((i == 0) | (e != prev))
    def _():
        rows = 128

        def cast1(j, carry):
            lo = pl.multiple_of(j * rows, rows)
            w1b_ref[pl.ds(lo, rows), :] = w1_ref[pl.ds(lo, rows), :].astype(BF16)
            return carry

        def cast2(j, carry):
            lo = pl.multiple_of(j * rows, rows)
            w2b_ref[pl.ds(lo, rows), :] = w2_ref[pl.ds(lo, rows), :].astype(BF16)
            return carry
        lax.fori_loop(0, w1_ref.shape[0] // rows, cast1, 0)
        lax.fori_loop(0, w2_ref.shape[0] // rows, cast2, 0)

    @pl.when((i >= 2) & (i - 2 < nu))
    def _():
        wait_scatters(slot)

    @pl.when(i < nu)
    def _():
        wait_gathers(slot)
        hb = _dot(xbuf[slot].astype(BF16), w1b_ref[...]) + b1_ref[...]
        glu = jnp.minimum(hb[:, :f], SWIGLU_LIMIT)
        lin = jnp.clip(hb[:, f:], -SWIGLU_LIMIT, SWIGLU_LIMIT)
        act = (lin + 1.0) * (glu * jax.nn.sigmoid(SWIGLU_ALPHA * glu))
        ybuf[slot] = _dot(act.astype(BF16), w2b_ref[...]) + b2_ref[...]

        def body(r, carry):
            scatter_copy(i, r, slot).start()
            return carry
        lax.fori_loop(0, blk, body, 0, unroll=ROW_UNROLL)

    @pl.when(i == last)
    def _():
        @pl.when(i < nu)
        def _():
            wait_scatters(slot)

        @pl.when((i >= 1) & (i - 1 < nu))
        def _():
            wait_scatters(other)


def _moe_blocks(h, idx_tab, blk_e, n_used, n_rows_out, w1, b1, w2, b2, layer):
    n_tok, d = h.shape
    n_blk = blk_e.shape[0]
    f2 = w1.shape[-1]
    f = w2.shape[-2]
    grid_spec = pltpu.PrefetchScalarGridSpec(
        num_scalar_prefetch=2,
        grid=(n_blk,),
        in_specs=[
            pl.BlockSpec(memory_space=pl.ANY),
            pl.BlockSpec(memory_space=pl.ANY),
            pl.BlockSpec((None, None, d, f2), lambda i, be, nu: (layer, be[i], 0, 0)),
            pl.BlockSpec((None, None, 1, f2), lambda i, be, nu: (layer, be[i], 0, 0)),
            pl.BlockSpec((None, None, f, d), lambda i, be, nu: (layer, be[i], 0, 0)),
            pl.BlockSpec((None, None, 1, d), lambda i, be, nu: (layer, be[i], 0, 0)),
        ],
        out_specs=pl.BlockSpec(memory_space=pl.ANY),
        scratch_shapes=[
            pltpu.SMEM((IDX_RING, MOE_BLOCK), jnp.int32),
            pltpu.VMEM((2, MOE_BLOCK, d), F32),
            pltpu.VMEM((2, MOE_BLOCK, d), F32),
            pltpu.VMEM((d, f2), BF16),
            pltpu.VMEM((f, d), BF16),
            pltpu.SemaphoreType.DMA((IDX_RING,)),
            pltpu.SemaphoreType.DMA((2,)),
            pltpu.SemaphoreType.DMA((2,)),
        ],
    )
    return pl.pallas_call(
        _moe_kernel,
        grid_spec=grid_spec,
        out_shape=jax.ShapeDtypeStruct((n_rows_out, d), F32),
        compiler_params=_params(("arbitrary",), mib=56),
        name="moe_experts",
    )(blk_e, n_used, idx_tab, h, w1, b1.reshape(b1.shape[0], b1.shape[1], 1, f2), w2, b2.reshape(b2.shape[0], b2.shape[1], 1, d))


def _moe(h, logits, w1, b1, w2, b2, layer):
    n, d = h.shape
    nk = n * TOP_K
    top_val, top_idx = lax.top_k(logits, TOP_K)
    gates = jax.nn.softmax(top_val, axis=-1)
    flat_e = top_idx.reshape(-1)
    order = jnp.argsort(flat_e).astype(jnp.int32)
    counts = jnp.bincount(flat_e, length=N_EXPERTS).astype(jnp.int32)
    padded = (counts + MOE_BLOCK - 1) // MOE_BLOCK * MOE_BLOCK
    start = jnp.cumsum(counts) - counts
    pend = jnp.cumsum(padded)
    pstart = pend - padded
    n_blk = -(-nk // MOE_BLOCK) + N_EXPERTS
    blk_lo = jnp.arange(n_blk, dtype=jnp.int32) * MOE_BLOCK
    blk_e = jnp.minimum(jnp.sum((pend[None, :] <= blk_lo[:, None]).astype(jnp.int32), axis=1), N_EXPERTS - 1)
    n_used = (pend[-1] // MOE_BLOCK).astype(jnp.int32).reshape(1)
    s0 = start[blk_e] + blk_lo - pstart[blk_e]
    left = start[blk_e] + counts[blk_e] - s0
    order_pad = jnp.concatenate([order, jnp.zeros((MOE_BLOCK,), jnp.int32)])
    win = jax.vmap(lambda s: lax.dynamic_slice(order_pad, (s,), (MOE_BLOCK,)))(jnp.clip(s0, 0, nk))
    r = jnp.arange(MOE_BLOCK, dtype=jnp.int32)[None, :]
    spare = nk + (jnp.arange(n_blk, dtype=jnp.int32)[:, None] % 2) * MOE_BLOCK + r
    idx_tab = jnp.where(r < left[:, None], win, spare).reshape(-1)
    y4 = _moe_blocks(h, idx_tab, blk_e, n_used, nk + 2 * MOE_BLOCK, w1, b1, w2, b2, layer)
    return jnp.einsum("nk,nkd->nd", gates, y4[:nk].reshape(n, TOP_K, d))


def _rope_tables(t):
    tok = jnp.arange(t)
    row = (tok // GRID_W).astype(F32)[:, None]
    col = (tok % GRID_W).astype(F32)[:, None]
    inv = 1.0 / (ROPE_THETA ** (jnp.arange(0, ROPE_AXIS, 2, dtype=F32) / ROPE_AXIS))
    ang = jnp.concatenate([row * inv, row * inv, col * inv, col * inv], axis=-1)
    cos = jnp.ones((t, V7X_LANES), F32).at[:, MLA_NOPE:MLA_QK].set(jnp.cos(ang))
    sin = jnp.zeros((t, V7X_LANES), F32).at[:, MLA_NOPE:MLA_QK].set(jnp.sin(ang))
    return cos, sin


def _pad_heads(w, per_head, offset=0):
    lead = w.shape[:-1]
    w = w.reshape(lead + (MLA_HEADS, per_head))
    out = jnp.zeros(lead + (MLA_HEADS, V7X_LANES), w.dtype).at[..., offset:offset + per_head].set(w)
    return out.reshape(lead + (MLA_PAD,))


def _layer_weights(l, g_mix, w_in, na_q_norm, na_k_norm, mla_q_a_norm, mla_w_q_b, mla_kv_a_norm, mla_w_kv_b,
                   mla_q_norm, mla_k_norm, gla_w_gate, gla_b_gate, gla_o_norm, w_out, g_ffn, w_router, b_router):
    d = w_in.shape[1]
    sizes = (NA_WIDTH, NA_WIDTH, NA_WIDTH, MLA_Q_RANK, MLA_KV_RANK, MLA_ROPE, GLA_KW, GLA_KW, GLA_WIDTH, GLA_WIDTH,
             GLA_GATE_RANK, GLA_GATE_RANK)
    (wq, wk, wv, wcq, wckv, wkr, wgq, wgk, wgv, wgo, wlf, wlb) = jnp.split(w_in[l], np.cumsum(sizes)[:-1].tolist(), axis=-1)
    small = jnp.zeros((d, V7X_LANES), F32)
    small = small.at[:, S_LF:S_LF + GLA_GATE_RANK].set(wlf).at[:, S_LB:S_LB + GLA_GATE_RANK].set(wlb)
    small = small.at[:, S_KR:S_KR + MLA_ROPE].set(wkr)
    w_packed = jnp.concatenate([wq, wk, wv, wcq, wckv, wgq, wgk, wgv, wgo, small], axis=-1).astype(BF16)
    kvb = mla_w_kv_b[l].reshape(MLA_KV_RANK, MLA_HEADS, MLA_NOPE + MLA_V)
    wg = jnp.zeros((V7X_LANES, 2 * GLA_KW), F32)
    wg = wg.at[S_LF:S_LF + GLA_GATE_RANK, :GLA_KW].set(gla_w_gate[l, 0])
    wg = wg.at[S_LB:S_LB + GLA_GATE_RANK, GLA_KW:].set(gla_w_gate[l, 1])
    wg_hi, wg_lo = _split(wg)
    seg = (np.arange(NA_WIDTH)[:, None] // NA_DIM == np.arange(NA_WIDTH)[None, :] // NA_DIM)
    wr = jnp.zeros((d, V7X_LANES), F32).at[:, :N_EXPERTS].set(w_router[l])
    wr_hi, wr_lo = _split(wr)
    wo = w_out[l].astype(BF16)
    return {
        "gmix": g_mix[l][None, :],
        "w_in": w_packed,
        "naqn": jnp.tile(na_q_norm[l], NA_HEADS)[None, :],
        "nakn": jnp.tile(na_k_norm[l], NA_HEADS)[None, :],
        "qan": mla_q_a_norm[l][None, :],
        "wqb": _pad_heads(mla_w_q_b[l], MLA_QK).astype(BF16),
        "kvan": mla_kv_a_norm[l][None, :],
        "wkk": _pad_heads(kvb[:, :, :MLA_NOPE].reshape(MLA_KV_RANK, -1), MLA_NOPE).astype(BF16),
        "wkv": kvb[:, :, MLA_NOPE:].reshape(MLA_KV_RANK, MLA_WIDTH).astype(BF16),
        "mqn": _pad_heads(jnp.tile(mla_q_norm[l], MLA_HEADS), MLA_QK)[None, :],
        "mkn": _pad_heads(jnp.tile(mla_k_norm[l], MLA_HEADS), MLA_QK)[None, :],
        "wg_hi": wg_hi, "wg_lo": wg_lo,
        "bg": jnp.concatenate([gla_b_gate[l, 0], gla_b_gate[l, 1]])[None, :],
        "seg64": jnp.asarray(seg, BF16),
        "onorm": jnp.tile(gla_o_norm[l], GLA_HEADS)[None, :],
        "gffn": g_ffn[l][None, :],
        "wo_a": wo[:NA_WIDTH], "wo_b": wo[NA_WIDTH:NA_WIDTH + MLA_WIDTH], "wo_c": wo[NA_WIDTH + MLA_WIDTH:],
        "wr_hi": wr_hi, "wr_lo": wr_lo,
        "br": jnp.zeros((1, V7X_LANES), F32).at[0, :N_EXPERTS].set(b_router[l]),
    }


def kernel(x, c, ctx, c_ctx, w_ada, b_ada, g_mix, w_in, na_q_norm, na_k_norm, na_rpb, mla_q_a_norm, mla_w_q_b,
           mla_kv_a_norm, mla_w_kv_b, mla_q_norm, mla_k_norm, gla_w_gate, gla_b_gate, gla_o_norm, w_out, g_ffn,
           w_router, b_router, w_moe1, b_moe1, w_moe2, b_moe2):
    bsz, t, d = x.shape
    tc = ctx.shape[1]
    depth = w_ada.shape[0]
    rows = t // GRID_W
    cos, sin = _rope_tables(t)

    pad = (-(bsz + 1)) % 8
    cvec = jnp.concatenate([c, c_ctx[None, :], jnp.zeros((pad, d), F32)], axis=0)
    mod = _ada(cvec, w_ada, b_ada)

    h_ctx = ctx
    for l in range(depth):
        ctx_out = l < depth - 1
        lw = _layer_weights(l, g_mix, w_in, na_q_norm, na_k_norm, mla_q_a_norm, mla_w_q_b, mla_kv_a_norm,
                            mla_w_kv_b, mla_q_norm, mla_k_norm, gla_w_gate, gla_b_gate, gla_o_norm, w_out, g_ffn,
                            w_router, b_router)
        m_lat = [m[:, None, :] for m in jnp.split(mod[l, :bsz], 6, axis=-1)]
        m_ctx = [jnp.broadcast_to(m[None, :, :], (bsz, 1, d)) for m in jnp.split(mod[l, bsz:bsz + 1], 6, axis=-1)]
        sh1, sc1, gt1, sh2, sc2, gt2 = m_lat
        csh1, csc1, cgt1, csh2, csc2, cgt2 = m_ctx

        (naq, nak, nav, mq, mk, mv, gq, gk, gv, gout, gf, gb) = _inproj(x, sc1, sh1, lw, cos, sin, True)
        (cnaq, cnak, cnav, cmq, cmk, cmv, cgq, cgk, cgv, cgout, cgf, cgb) = _inproj(
            h_ctx, csc1, csh1, lw, cos[:tc], sin[:tc], False)

        bias = _na_bias_table(na_rpb[l], rows)
        a_x = _na_attention(naq, nak, nav, cnak, cnav, bias)
        b_x = _attention(mq, [(mk, mv), (cmk, cmv)], MLA_HEADS, V7X_LANES, MLA_V, 256)
        c_x, c_c = _gla((gq, gk, gv, gf, gb, gout), (cgq, cgk, cgv, cgf, cgb, cgout), lw["onorm"], lw["seg64"], ctx_out)
        x_new, h2, logits = _outproj(a_x, b_x, c_x, x, gt1, sc2, sh2, lw)

        toks = [h2.reshape(bsz * t, d)]
        lgs = [logits.reshape(bsz * t, V7X_LANES)]
        if ctx_out:
            a_c = _attention_masked(cnaq, cnak, cnav, NA_DIM)
            b_c = _attention(cmq, [(cmk, cmv)], MLA_HEADS, V7X_LANES, MLA_V, 256)
            hc_new, hc2, clogits = _outproj(a_c, b_c, c_c, h_ctx, cgt1, csc2, csh2, lw)
            toks.append(hc2.reshape(bsz * tc, d))
            lgs.append(clogits.reshape(bsz * tc, V7X_LANES))
        moe = _moe(jnp.concatenate(toks, axis=0), jnp.concatenate(lgs, axis=0)[:, :N_EXPERTS],
                   w_moe1, b_moe1, w_moe2, b_moe2, l)
        x = x_new + gt2 * moe[:bsz * t].reshape(bsz, t, d)
        if ctx_out:
            h_ctx = hc_new + cgt2 * moe[bsz * t:].reshape(bsz, tc, d)
    return x
```

```python
import functools

import numpy as np
import jax
import jax.numpy as jnp
from jax import lax
from jax.experimental import pallas as pl
from jax.experimental.pallas import tpu as pltpu

F32 = jnp.float32
BF16 = jnp.bfloat16

V7X_LANES = 128
V7X_VMEM_BYTES = 64 * 1024 * 1024

EPS = 1e-6
GRID_W = 64
NA_HEADS, NA_DIM, NA_KH, NA_KW = 4, 64, 8, 16
MLA_HEADS, MLA_NOPE, MLA_ROPE, MLA_V = 4, 64, 32, 128
MLA_QK = MLA_NOPE + MLA_ROPE
MLA_Q_RANK, MLA_KV_RANK = 256, 128
ROPE_AXIS = MLA_ROPE // 2
ROPE_THETA = 10000.0
GLA_HEADS, GLA_DK, GLA_DV = 4, 32, 64
GLA_GATE_RANK, GLA_GATE_NORM, GLA_CHUNK = 16, 16.0, 64
NA_WIDTH = NA_HEADS * NA_DIM
MLA_WIDTH = MLA_HEADS * MLA_V
MLA_PAD = MLA_HEADS * V7X_LANES
GLA_KW = GLA_HEADS * GLA_DK
GLA_WIDTH = GLA_HEADS * GLA_DV
N_EXPERTS, TOP_K = 32, 4
SWIGLU_LIMIT, SWIGLU_ALPHA = 7.0, 1.702
MOE_BLOCK = 512

C_NAQ, C_NAK, C_NAV, C_CQ, C_CKV = 0, 256, 512, 768, 1024
C_GQ, C_GK, C_GV, C_GOUT, C_SMALL = 1152, 1280, 1408, 1664, 1920
IN_PACKED = 2048
S_LF, S_LB, S_KR = 0, 16, 64


def _vmem_limit(mib):
    return min(mib * 1024 * 1024, V7X_VMEM_BYTES - 4 * 1024 * 1024)


def _params(sem, mib=48):
    return pltpu.CompilerParams(dimension_semantics=sem, vmem_limit_bytes=_vmem_limit(mib))


def _dot(a, b):
    return jnp.dot(a, b, preferred_element_type=F32)


def _dot_nt(a, b):
    return lax.dot_general(a, b, (((1,), (1,)), ((), ())), preferred_element_type=F32)


def _dot_tn(a, b):
    return lax.dot_general(a, b, (((0,), (0,)), ((), ())), preferred_element_type=F32)


def _split(x):
    hi = x.astype(BF16)
    lo = (x - hi.astype(F32)).astype(BF16)
    return hi, lo


def _rms(x):
    return x * lax.rsqrt(jnp.mean(x * x, axis=-1, keepdims=True) + EPS)


def _ada_kernel(c_ref, w_ref, b_ref, o_ref):
    c = c_ref[...]
    s = (c * jax.nn.sigmoid(c)).astype(BF16)
    o_ref[...] = _dot(s, w_ref[...].astype(BF16)) + b_ref[...]


def _ada(cvec, w_ada, b_ada):
    depth, d, n6 = w_ada.shape
    r = cvec.shape[0]
    tn = 1024
    return pl.pallas_call(
        _ada_kernel,
        grid=(depth, n6 // tn),
        in_specs=[
            pl.BlockSpec((r, d), lambda l, j: (0, 0)),
            pl.BlockSpec((None, d, tn), lambda l, j: (l, 0, j)),
            pl.BlockSpec((None, 1, tn), lambda l, j: (l, 0, j)),
        ],
        out_specs=pl.BlockSpec((None, r, tn), lambda l, j: (l, 0, j)),
        out_shape=jax.ShapeDtypeStruct((depth, r, n6), F32),
        compiler_params=_params(("parallel", "parallel")),
        name="ada_modulation",
    )(cvec, w_ada, b_ada.reshape(depth, 1, n6))


def _rope(x, cos, sin):
    lane = lax.broadcasted_iota(jnp.int32, (x.shape[0], V7X_LANES), 1)
    first = (lane & (ROPE_AXIS - 1)) < (ROPE_AXIS // 2)
    outs = []
    for h in range(MLA_HEADS):
        xs = x[:, V7X_LANES * h:V7X_LANES * (h + 1)]
        rot = jnp.where(first, -pltpu.roll(xs, V7X_LANES - ROPE_AXIS // 2, 1), pltpu.roll(xs, ROPE_AXIS // 2, 1))
        outs.append(xs * cos + rot * sin)
    return jnp.concatenate(outs, axis=1)


def _head_rms_padded(x, n_real):
    outs = []
    for h in range(MLA_HEADS):
        xs = x[:, V7X_LANES * h:V7X_LANES * (h + 1)]
        ms = jnp.sum(xs * xs, axis=-1, keepdims=True) * (1.0 / n_real)
        outs.append(xs * lax.rsqrt(ms + EPS))
    return jnp.concatenate(outs, axis=1)


def _log_sigmoid(x):
    return jnp.minimum(x, 0.0) - jnp.log1p(jnp.exp(-jnp.abs(x)))


def _inproj_kernel(x_ref, sc_ref, sh_ref, gmix_ref, w_ref, cos_ref, sin_ref,
                   naqn_ref, nakn_ref, qan_ref, wqb_ref, kvan_ref, wkk_ref, wkv_ref, mqn_ref, mkn_ref,
                   wgh_ref, wgl_ref, bg_ref, seg_ref,
                   naq_o, nak_o, nav_o, mq_o, mk_o, mv_o, gq_o, gk_o, gv_o, gout_o, gf_o, gb_o, *, use_rope):
    x = x_ref[...]
    h = _rms(x) * gmix_ref[...]
    h = h * (1.0 + sc_ref[...]) + sh_ref[...]
    p = _dot(h.astype(BF16), w_ref[...])

    seg = seg_ref[...]
    q = p[:, C_NAQ:C_NAQ + NA_WIDTH]
    k = p[:, C_NAK:C_NAK + NA_WIDTH]
    qss = _dot((q * q).astype(BF16), seg) * (1.0 / NA_DIM)
    kss = _dot((k * k).astype(BF16), seg) * (1.0 / NA_DIM)
    naq_o[...] = (q * lax.rsqrt(qss + EPS) * naqn_ref[...] * (NA_DIM ** -0.5)).astype(BF16)
    nak_o[...] = (k * lax.rsqrt(kss + EPS) * nakn_ref[...]).astype(BF16)
    nav_o[...] = p[:, C_NAV:C_NAV + NA_WIDTH].astype(BF16)

    small = p[:, C_SMALL:C_SMALL + V7X_LANES]
    cq = _rms(p[:, C_CQ:C_CQ + MLA_Q_RANK]) * qan_ref[...]
    mq = _head_rms_padded(_dot(cq.astype(BF16), wqb_ref[...]), MLA_QK) * mqn_ref[...]
    ckv = (_rms(p[:, C_CKV:C_CKV + MLA_KV_RANK]) * kvan_ref[...]).astype(BF16)
    lane = lax.broadcasted_iota(jnp.int32, small.shape, 1)
    kr = jnp.where((lane >= S_KR) & (lane < S_KR + MLA_ROPE), small, 0.0)
    mk = _dot(ckv, wkk_ref[...]) + jnp.concatenate([kr] * MLA_HEADS, axis=1)
    mk = _head_rms_padded(mk, MLA_QK) * mkn_ref[...]
    if use_rope:
        cos, sin = cos_ref[...], sin_ref[...]
        mq = _rope(mq, cos, sin)
        mk = _rope(mk, cos, sin)
    mq_o[...] = (mq * (MLA_QK ** -0.5)).astype(BF16)
    mk_o[...] = mk.astype(BF16)
    mv_o[...] = _dot(ckv, wkv_ref[...]).astype(BF16)

    gq_o[...] = p[:, C_GQ:C_GQ + GLA_KW] * (GLA_DK ** -0.5)
    gk_o[...] = p[:, C_GK:C_GK + GLA_KW]
    gv_o[...] = p[:, C_GV:C_GV + GLA_WIDTH]
    gout_o[...] = p[:, C_GOUT:C_GOUT + GLA_WIDTH]
    s_hi, s_lo = _split(small)
    wgh = wgh_ref[...]
    pre = _dot(s_hi, wgh) + _dot(s_lo, wgh) + _dot(s_hi, wgl_ref[...]) + bg_ref[...]
    ls = _log_sigmoid(pre) * (1.0 / GLA_GATE_NORM)
    gf_o[...] = ls[:, :GLA_KW]
    gb_o[...] = ls[:, GLA_KW:]


def _inproj(x, sc, sh, lw, cos, sin, use_rope):
    b, t, d = x.shape
    tm = min(512, t)
    full = lambda a: pl.BlockSpec(a.shape, lambda i, j: (0,) * a.ndim)
    row = lambda w: pl.BlockSpec((None, tm, w), lambda i, j: (i, j, 0))
    consts = [lw["gmix"], lw["w_in"]]
    tail = [lw["naqn"], lw["nakn"], lw["qan"], lw["wqb"], lw["kvan"], lw["wkk"], lw["wkv"], lw["mqn"], lw["mkn"],
            lw["wg_hi"], lw["wg_lo"], lw["bg"], lw["seg64"]]
    widths = [(NA_WIDTH, BF16)] * 3 + [(MLA_PAD, BF16), (MLA_PAD, BF16), (MLA_WIDTH, BF16),
                                       (GLA_KW, F32), (GLA_KW, F32), (GLA_WIDTH, F32), (GLA_WIDTH, F32),
                                       (GLA_KW, F32), (GLA_KW, F32)]
    return pl.pallas_call(
        functools.partial(_inproj_kernel, use_rope=use_rope),
        grid=(b, t // tm),
        in_specs=[row(d),
                  pl.BlockSpec((None, 1, d), lambda i, j: (i, 0, 0)),
                  pl.BlockSpec((None, 1, d), lambda i, j: (i, 0, 0))]
                 + [full(a) for a in consts]
                 + [pl.BlockSpec((tm, V7X_LANES), lambda i, j: (j, 0))] * 2
                 + [full(a) for a in tail],
        out_specs=[row(w) for w, _ in widths],
        out_shape=[jax.ShapeDtypeStruct((b, t, w), dt) for w, dt in widths],
        compiler_params=_params(("parallel", "parallel")),
        name="inproj_rope" if use_rope else "inproj_ctx",
    )(x, sc, sh, *consts, cos, sin, *tail)


def _stack_heads(q, width):
    lane = lax.broadcasted_iota(jnp.int32, q.shape, 1)
    zero = jnp.zeros_like(q)
    return jnp.concatenate([jnp.where(lane // width == h, q, zero) for h in range(NA_HEADS)], axis=0)


def _unstack_heads(o4, n, width):
    lane = lax.broadcasted_iota(jnp.int32, (n, o4.shape[1]), 1)
    out = jnp.zeros((n, o4.shape[1]), F32)
    for h in range(NA_HEADS):
        out = out + jnp.where(lane // width == h, o4[h * n:(h + 1) * n], 0.0)
    return out


def _na_kernel(q_ref, k_ref, v_ref, kc_ref, vc_ref, bias_ref, o_ref, *, rows):
    kc = kc_ref[...]
    vc = vc_ref[...]
    n_loc = NA_KH * GRID_W

    def body(r, carry):
        rs = jnp.clip(r - NA_KH // 2, 0, rows - NA_KH)
        off = r - rs
        q = q_ref[pl.ds(pl.multiple_of(r * GRID_W, GRID_W), GRID_W), :]
        ks = k_ref[pl.ds(pl.multiple_of(rs * GRID_W, GRID_W), n_loc), :]
        vs = v_ref[pl.ds(pl.multiple_of(rs * GRID_W, GRID_W), n_loc), :]
        q4 = _stack_heads(q, NA_DIM)
        s_loc = _dot_nt(q4, ks) + bias_ref[off]
        s_ctx = _dot_nt(q4, kc)
        m = jnp.maximum(jnp.max(s_loc, axis=-1, keepdims=True), jnp.max(s_ctx, axis=-1, keepdims=True))
        p_loc = jnp.exp(s_loc - m)
        p_ctx = jnp.exp(s_ctx - m)
        denom = jnp.sum(p_loc, axis=-1, keepdims=True) + jnp.sum(p_ctx, axis=-1, keepdims=True)
        o4 = (_dot(p_loc.astype(BF16), vs) + _dot(p_ctx.astype(BF16), vc)) * (1.0 / denom)
        o_ref[pl.ds(pl.multiple_of(r * GRID_W, GRID_W), GRID_W), :] = _unstack_heads(o4, GRID_W, NA_DIM).astype(BF16)
        return carry

    lax.fori_loop(0, rows, body, 0)


def _na_bias_table(rpb, rows):
    kh = NA_KH
    r_all = np.arange(rows)
    offs = r_all - np.clip(r_all - kh // 2, 0, rows - kh)
    n_off = int(offs.max()) + 1
    qc = np.arange(GRID_W)
    kcol = np.arange(GRID_W)
    cs = np.clip(qc - NA_KW // 2, 0, GRID_W - NA_KW)
    in_win = (kcol[None, :] >= cs[:, None]) & (kcol[None, :] < cs[:, None] + NA_KW)
    dc = np.clip(kcol[None, :] - qc[:, None] + NA_KW - 1, 0, 2 * NA_KW - 2)
    onehot = (dc[None] == np.arange(2 * NA_KW - 1)[:, None, None]).astype(np.float32)
    toep = jnp.einsum("hdc,cqk->hdqk", rpb.astype(F32), jnp.asarray(onehot), precision=lax.Precision.HIGHEST)
    toep = jnp.where(in_win[None, None], toep, -jnp.inf)
    per_off = [toep[:, NA_KH - 1 - off:2 * NA_KH - 1 - off] for off in range(n_off)]
    bias = jnp.stack(per_off, axis=0)
    return bias.transpose(0, 1, 3, 2, 4).reshape(n_off, NA_HEADS * GRID_W, kh * GRID_W)


def _na_attention(q, k, v, kc, vc, bias):
    b, t, w = q.shape
    tc = kc.shape[1]
    rows = t // GRID_W
    assert rows >= NA_KH and t % GRID_W == 0
    seq = lambda n: pl.BlockSpec((None, n, w), lambda i: (i, 0, 0))
    return pl.pallas_call(
        functools.partial(_na_kernel, rows=rows),
        grid=(b,),
        in_specs=[seq(t), seq(t), seq(t), seq(tc), seq(tc),
                  pl.BlockSpec(bias.shape, lambda i: (0, 0, 0))],
        out_specs=seq(t),
        out_shape=jax.ShapeDtypeStruct((b, t, w), BF16),
        compiler_params=_params(("parallel",)),
        name="na_attention",
    )(q, k, v, kc, vc, bias)


def _softmax_pv(scores, values):
    m = functools.reduce(jnp.maximum, [jnp.max(s, axis=-1, keepdims=True) for s in scores])
    ps = [jnp.exp(s - m) for s in scores]
    denom = functools.reduce(jnp.add, [jnp.sum(p, axis=-1, keepdims=True) for p in ps])
    o = functools.reduce(jnp.add, [_dot(p.astype(BF16), v) for p, v in zip(ps, values)])
    return o * (1.0 / denom)


def _attn_sliced_kernel(*refs, n_parts, heads, dh, dv):
    q_ref, o_ref = refs[0], refs[-1]
    for h in range(heads):
        q = q_ref[:, dh * h:dh * (h + 1)]
        scores = [_dot_nt(q, refs[1 + 2 * p][:, dh * h:dh * (h + 1)]) for p in range(n_parts)]
        values = [refs[2 + 2 * p][:, dv * h:dv * (h + 1)] for p in range(n_parts)]
        o_ref[:, dv * h:dv * (h + 1)] = _softmax_pv(scores, values).astype(o_ref.dtype)


def _attn_masked_kernel(q_ref, k_ref, v_ref, o_ref, *, width):
    n = q_ref.shape[0]
    q4 = _stack_heads(q_ref[...], width)
    o4 = _softmax_pv([_dot_nt(q4, k_ref[...])], [v_ref[...]])
    o_ref[...] = _unstack_heads(o4, n, width).astype(o_ref.dtype)


def _attention(q, parts, heads, dh, dv, tq):
    b, t, wq = q.shape
    tq = min(tq, t)
    flat = [a for kv in parts for a in kv]
    kv_spec = lambda a: pl.BlockSpec((None,) + a.shape[1:], lambda i, j: (i, 0, 0))
    return pl.pallas_call(
        functools.partial(_attn_sliced_kernel, n_parts=len(parts), heads=heads, dh=dh, dv=dv),
        grid=(b, t // tq),
        in_specs=[pl.BlockSpec((None, tq, wq), lambda i, j: (i, j, 0))] + [kv_spec(a) for a in flat],
        out_specs=pl.BlockSpec((None, tq, heads * dv), lambda i, j: (i, j, 0)),
        out_shape=jax.ShapeDtypeStruct((b, t, heads * dv), BF16),
        compiler_params=_params(("parallel", "parallel")),
        name="mla_attention_%d" % len(parts),
    )(q, *flat)


def _attention_masked(q, k, v, width):
    b, t, w = q.shape
    spec = lambda a: pl.BlockSpec((None,) + a.shape[1:], lambda i: (i, 0, 0))
    return pl.pallas_call(
        functools.partial(_attn_masked_kernel, width=width),
        grid=(b,),
        in_specs=[spec(q), spec(k), spec(v)],
        out_specs=spec(q),
        out_shape=jax.ShapeDtypeStruct((b, t, w), BF16),
        compiler_params=_params(("parallel",)),
        name="ctx_na_attention",
    )(q, k, v)


def _gla_chunk(q, k, v, g, st_ref, tri, allowed, hm_k, hm_v, hm_s, end_row):
    g_hi, g_lo = _split(g)
    bcum = _dot(tri, g_hi) + _dot(tri, g_lo)
    bend = bcum[end_row:end_row + 1, :]
    qe = (q * jnp.exp(bcum)).astype(BF16)
    ke = (k * jnp.exp(bend - bcum)).astype(BF16)
    kt = k * jnp.exp(-bcum)
    st = st_ref[...]
    o = _dot_nt(qe, st.astype(BF16))
    kst = jnp.concatenate([jnp.where(m, kt, 0.0) for m in hm_k], axis=0).astype(BF16)
    a = jnp.where(allowed, _dot_nt(qe, kst), 0.0)
    vbd = jnp.concatenate([jnp.where(m, v, 0.0) for m in hm_v], axis=0).astype(BF16)
    o = o + _dot(a.astype(BF16), vbd)
    upd = _dot_tn(v.astype(BF16), ke)
    st_ref[...] = st * jnp.exp(bend) + jnp.where(hm_s, upd, 0.0)
    return o


def _gla_kernel(q_ref, k_ref, v_ref, gf_ref, gb_ref, go_ref,
                qc_ref, kc_ref, vc_ref, gfc_ref, gbc_ref, goc_ref, onorm_ref, seg_ref,
                *rest, ctx_out):
    if ctx_out:
        ox_ref, oc_ref, acc_ref, accc_ref, sf_ref, sb_ref = rest
    else:
        ox_ref, acc_ref, sf_ref, sb_ref = rest
        oc_ref = accc_ref = None
    c = GLA_CHUNK
    t, tc = q_ref.shape[0], qc_ref.shape[0]
    n, nc = t // c, tc // c

    ri = lax.broadcasted_iota(jnp.int32, (c, c), 0)
    ci = lax.broadcasted_iota(jnp.int32, (c, c), 1)
    tri_f = (ci <= ri).astype(BF16)
    tri_b = (ci >= ri).astype(BF16)
    ti = lax.broadcasted_iota(jnp.int32, (c, GLA_HEADS * c), 0)
    si = lax.broadcasted_iota(jnp.int32, (c, GLA_HEADS * c), 1) % c
    allow_f = si <= ti
    allow_b = si >= ti
    lk = lax.broadcasted_iota(jnp.int32, (c, GLA_KW), 1) // GLA_DK
    lv = lax.broadcasted_iota(jnp.int32, (c, GLA_WIDTH), 1) // GLA_DV
    hm_k = [lk == h for h in range(GLA_HEADS)]
    hm_v = [lv == h for h in range(GLA_HEADS)]
    hm_s = (lax.broadcasted_iota(jnp.int32, (GLA_WIDTH, GLA_KW), 0) // GLA_DV
            == lax.broadcasted_iota(jnp.int32, (GLA_WIDTH, GLA_KW), 1) // GLA_DK)

    sf_ref[...] = jnp.zeros_like(sf_ref)
    sb_ref[...] = jnp.zeros_like(sb_ref)
    acc_ref[...] = jnp.zeros_like(acc_ref)
    if ctx_out:
        accc_ref[...] = jnp.zeros_like(accc_ref)

    def sweep(qr, kr, vr, gfr, gbr, dst, count):
        def body(j, carry):
            lo_f = pl.multiple_of(j * c, c)
            lo_b = pl.multiple_of((count - 1 - j) * c, c)
            of = _gla_chunk(qr[pl.ds(lo_f, c), :], kr[pl.ds(lo_f, c), :], vr[pl.ds(lo_f, c), :],
                            gfr[pl.ds(lo_f, c), :], sf_ref, tri_f, allow_f, hm_k, hm_v, hm_s, c - 1)
            ob = _gla_chunk(qr[pl.ds(lo_b, c), :], kr[pl.ds(lo_b, c), :], vr[pl.ds(lo_b, c), :],
                            gbr[pl.ds(lo_b, c), :], sb_ref, tri_b, allow_b, hm_k, hm_v, hm_s, 0)
            if dst is not None:
                dst[pl.ds(lo_f, c), :] += of
                dst[pl.ds(lo_b, c), :] += ob
            return carry
        lax.fori_loop(0, count, body, 0)

    sweep(qc_ref, kc_ref, vc_ref, gfc_ref, gbc_ref, accc_ref, nc)
    sweep(q_ref, k_ref, v_ref, gf_ref, gb_ref, acc_ref, n)

    def finish(acc, gate, out, total):
        tr = min(256, total)

        def body(i, carry):
            lo = pl.multiple_of(i * tr, tr)
            o = acc[pl.ds(lo, tr), :]
            ms = _dot((o * o).astype(BF16), seg_ref[...]) * (1.0 / GLA_DV)
            gt = gate[pl.ds(lo, tr), :]
            y = o * lax.rsqrt(ms + EPS) * onorm_ref[...] * (gt * jax.nn.sigmoid(gt))
            out[pl.ds(lo, tr), :] = y.astype(out.dtype)
            return carry
        lax.fori_loop(0, total // tr, body, 0)

    finish(acc_ref, go_ref, ox_ref, t)
    if ctx_out:
        finish(accc_ref, goc_ref, oc_ref, tc)


def _gla(lat, ctx, onorm, seg64, ctx_out):
    b, t, _ = lat[0].shape
    tc = ctx[0].shape[1]
    spec = lambda a: pl.BlockSpec((None,) + a.shape[1:], lambda i: (i, 0, 0))
    full = lambda a: pl.BlockSpec(a.shape, lambda i: (0,) * a.ndim)
    out_shape = [jax.ShapeDtypeStruct((b, t, GLA_WIDTH), BF16)]
    scratch = [pltpu.VMEM((t, GLA_WIDTH), F32)]
    if ctx_out:
        out_shape.append(jax.ShapeDtypeStruct((b, tc, GLA_WIDTH), BF16))
        scratch.append(pltpu.VMEM((tc, GLA_WIDTH), F32))
    scratch += [pltpu.VMEM((GLA_WIDTH, GLA_KW), F32)] * 2
    outs = pl.pallas_call(
        functools.partial(_gla_kernel, ctx_out=ctx_out),
        grid=(b,),
        in_specs=[spec(a) for a in lat] + [spec(a) for a in ctx] + [full(onorm), full(seg64)],
        out_specs=[spec(s) for s in out_shape],
        out_shape=out_shape,
        scratch_shapes=scratch,
        compiler_params=_params(("parallel",)),
        name="gla_ctx_out" if ctx_out else "gla_last",
    )(*lat, *ctx, onorm, seg64)
    return outs if ctx_out else (outs[0], None)


def _outproj_kernel(a_ref, b_ref, c_ref, x_ref, gt_ref, sc_ref, sh_ref, gffn_ref,
                    wa_ref, wb_ref, wc_ref, wrh_ref, wrl_ref, br_ref, xo_ref, h_ref, lg_ref):
    mix = _dot(a_ref[...], wa_ref[...]) + _dot(b_ref[...], wb_ref[...]) + _dot(c_ref[...], wc_ref[...])
    xn = x_ref[...] + gt_ref[...] * mix
    xo_ref[...] = xn
    h = _rms(xn) * gffn_ref[...]
    h = h * (1.0 + sc_ref[...]) + sh_ref[...]
    h_ref[...] = h
    h_hi, h_lo = _split(h)
    wrh = wrh_ref[...]
    lg_ref[...] = _dot(h_hi, wrh) + _dot(h_lo, wrh) + _dot(h_hi, wrl_ref[...]) + br_ref[...]


def _outproj(a, bm, c, x, gt, sc, sh, lw):
    b, t, d = x.shape
    tm = min(512, t)
    row = lambda w: pl.BlockSpec((None, tm, w), lambda i, j: (i, j, 0))
    vec = pl.BlockSpec((None, 1, d), lambda i, j: (i, 0, 0))
    full = lambda arr: pl.BlockSpec(arr.shape, lambda i, j: (0,) * arr.ndim)
    consts = [lw["gffn"], lw["wo_a"], lw["wo_b"], lw["wo_c"], lw["wr_hi"], lw["wr_lo"], lw["br"]]
    return pl.pallas_call(
        _outproj_kernel,
        grid=(b, t // tm),
        in_specs=[row(NA_WIDTH), row(MLA_WIDTH), row(GLA_WIDTH), row(d), vec, vec, vec] + [full(w) for w in consts],
        out_specs=[row(d), row(d), row(V7X_LANES)],
        out_shape=[jax.ShapeDtypeStruct((b, t, d), F32), jax.ShapeDtypeStruct((b, t, d), F32),
                   jax.ShapeDtypeStruct((b, t, V7X_LANES), F32)],
        compiler_params=_params(("parallel", "parallel")),
        name="outproj_router",
    )(a, bm, c, x, gt, sc, sh, *consts)


def _moe_kernel(be_ref, tok_hbm, pair_hbm, h_hbm, w1_ref, b1_ref, w2_ref, b2_ref, y_hbm,
                g0, g1, s0, s1, x0, x1, y0, y1, w1b_ref, w2b_ref, gisem, sisem, gsem, ssem):
    i = pl.program_id(0)
    last = pl.num_programs(0) - 1
    blk = x0.shape[0]
    f = w2_ref.shape[0]
    gbuf, sbuf, xbuf, ybuf = (g0, g1), (s0, s1), (x0, x1), (y0, y1)

    def load_tok(t, s):
        return pltpu.make_async_copy(tok_hbm.at[pl.ds(pl.multiple_of(t * blk, blk), blk)], gbuf[s], gisem.at[s])

    def load_pair(t, s):
        return pltpu.make_async_copy(pair_hbm.at[pl.ds(pl.multiple_of(t * blk, blk), blk)], sbuf[s], sisem.at[s])

    def start_gathers(s):
        for r in range(blk):
            pltpu.make_async_copy(h_hbm.at[gbuf[s][r]], xbuf[s].at[r], gsem.at[s]).start()

    def wait_gathers(s):
        for r in range(blk):
            pltpu.make_async_copy(h_hbm.at[0], xbuf[s].at[r], gsem.at[s]).wait()

    def start_scatters(s):
        for r in range(blk):
            pltpu.make_async_copy(ybuf[s].at[r], y_hbm.at[sbuf[s][r]], ssem.at[s]).start()

    def wait_scatters(s):
        for r in range(blk):
            pltpu.make_async_copy(ybuf[s].at[r], y_hbm.at[0], ssem.at[s]).wait()

    @pl.when(i == 0)
    def _():
        y0[...] = jnp.zeros_like(y0)
        y1[...] = jnp.zeros_like(y1)
        first_spare = y_hbm.shape[0] - 2 * blk
        for r in range(blk):
            pltpu.make_async_copy(y0.at[r], y_hbm.at[first_spare + r], ssem.at[0]).start()
        load_tok(1, 0).start()
        load_tok(2, 1).start()
        load_pair(0, 1).start()
        load_tok(1, 0).wait()
        start_gathers(0)

    e = be_ref[i]
    prev = be_ref[jnp.maximum(i - 1, 0)]

    @pl.when((i == 0) | (e != prev))
    def _():
        rows = 128

        def cast1(j, carry):
            lo = pl.multiple_of(j * rows, rows)
            w1b_ref[pl.ds(lo, rows), :] = w1_ref[pl.ds(lo, rows), :].astype(BF16)
            return carry

        def cast2(j, carry):
            lo = pl.multiple_of(j * rows, rows)
            w2b_ref[pl.ds(lo, rows), :] = w2_ref[pl.ds(lo, rows), :].astype(BF16)
            return carry
        lax.fori_loop(0, w1_ref.shape[0] // rows, cast1, 0)
        lax.fori_loop(0, w2_ref.shape[0] // rows, cast2, 0)

    def step(p):
        q = 1 - p

        load_tok(i + 2, q).wait()
        load_pair(i, q).wait()
        load_tok(i + 3, p).start()
        load_pair(i + 1, p).start()
        wait_gathers(p)
        wait_scatters(p)
        start_gathers(q)
        start_scatters(q)

        hb = _dot(xbuf[p][...].astype(BF16), w1b_ref[...]) + b1_ref[...]
        glu = jnp.minimum(hb[:, :f], SWIGLU_LIMIT)
        lin = jnp.clip(hb[:, f:], -SWIGLU_LIMIT, SWIGLU_LIMIT)
        act = (lin + 1.0) * (glu * jax.nn.sigmoid(SWIGLU_ALPHA * glu))
        ybuf[p][...] = _dot(act.astype(BF16), w2b_ref[...]) + b2_ref[...]

        @pl.when(i == last)
        def _():
            load_tok(i + 3, p).wait()
            load_pair(i + 1, p).wait()
            wait_gathers(q)
            wait_scatters(q)
            start_scatters(p)
            wait_scatters(p)

    for p in range(2):
        pl.when(i % 2 == p)(functools.partial(step, p))


def _moe_blocks(h, tok_tab, pair_tab, blk_e, n_rows_out, w1, b1, w2, b2, layer):
    n_tok, d = h.shape
    n_blk = blk_e.shape[0]
    f2 = w1.shape[-1]
    f = w2.shape[-2]
    grid_spec = pltpu.PrefetchScalarGridSpec(
        num_scalar_prefetch=1,
        grid=(n_blk,),
        in_specs=[
            pl.BlockSpec(memory_space=pl.ANY),
            pl.BlockSpec(memory_space=pl.ANY),
            pl.BlockSpec(memory_space=pl.ANY),
            pl.BlockSpec((None, None, d, f2), lambda i, be: (layer, be[i], 0, 0)),
            pl.BlockSpec((None, None, 1, f2), lambda i, be: (layer, be[i], 0, 0)),
            pl.BlockSpec((None, None, f, d), lambda i, be: (layer, be[i], 0, 0)),
            pl.BlockSpec((None, None, 1, d), lambda i, be: (layer, be[i], 0, 0)),
        ],
        out_specs=pl.BlockSpec(memory_space=pl.ANY),
        scratch_shapes=[pltpu.SMEM((MOE_BLOCK,), jnp.int32)] * 4 + [pltpu.VMEM((MOE_BLOCK, d), F32)] * 4 + [
            pltpu.VMEM((d, f2), BF16),
            pltpu.VMEM((f, d), BF16),
        ] + [pltpu.SemaphoreType.DMA((2,))] * 4,
    )
    return pl.pallas_call(
        _moe_kernel,
        grid_spec=grid_spec,
        out_shape=jax.ShapeDtypeStruct((n_rows_out, d), F32),
        compiler_params=_params(("arbitrary",), mib=56),
        name="moe_experts",
    )(blk_e, tok_tab, pair_tab, h, w1, b1.reshape(b1.shape[0], b1.shape[1], 1, f2), w2,
      b2.reshape(b2.shape[0], b2.shape[1], 1, d))


def _moe(h, logits, w1, b1, w2, b2, layer):
    n, d = h.shape
    nk = n * TOP_K
    top_val, top_idx = lax.top_k(logits, TOP_K)
    gates = jax.nn.softmax(top_val, axis=-1)
    flat_e = top_idx.reshape(-1)
    order = jnp.argsort(flat_e).astype(jnp.int32)
    counts = jnp.bincount(flat_e, length=N_EXPERTS).astype(jnp.int32)
    padded = (counts + MOE_BLOCK - 1) // MOE_BLOCK * MOE_BLOCK
    start = jnp.cumsum(counts) - counts
    pend = jnp.cumsum(padded)
    pstart = pend - padded
    n_blk = -(-nk // MOE_BLOCK) + N_EXPERTS
    blk_id = jnp.arange(-1, n_blk + 2, dtype=jnp.int32)
    blk_lo = blk_id * MOE_BLOCK
    blk_e = jnp.minimum(jnp.sum((pend[None, :] <= blk_lo[:, None]).astype(jnp.int32), axis=1), N_EXPERTS - 1)
    s0 = start[blk_e] + blk_lo - pstart[blk_e]
    left = jnp.where(blk_id >= 0, start[blk_e] + counts[blk_e] - s0, 0)
    r = jnp.arange(MOE_BLOCK, dtype=jnp.int32)[None, :]
    valid = r < left[:, None]
    win = order[jnp.clip(s0[:, None] + r, 0, nk - 1)]
    tok = lax.shift_right_logical(win, 2)
    dst = (win & (TOP_K - 1)) * n + tok
    dst_tab = jnp.where(valid, dst, nk + (blk_id[:, None] % 2) * MOE_BLOCK + r).reshape(-1)
    tok_tab = jnp.where(valid, tok, 0).reshape(-1)
    y4 = _moe_blocks(h, tok_tab, dst_tab, blk_e[1:n_blk + 1], nk + 2 * MOE_BLOCK, w1, b1, w2, b2, layer)
    return jnp.einsum("nk,knd->nd", gates, y4[:nk].reshape(TOP_K, n, d))


def _rope_tables(t):
    tok = jnp.arange(t)
    row = (tok // GRID_W).astype(F32)[:, None]
    col = (tok % GRID_W).astype(F32)[:, None]
    inv = 1.0 / (ROPE_THETA ** (jnp.arange(0, ROPE_AXIS, 2, dtype=F32) / ROPE_AXIS))
    ang = jnp.concatenate([row * inv, row * inv, col * inv, col * inv], axis=-1)
    cos = jnp.ones((t, V7X_LANES), F32).at[:, MLA_NOPE:MLA_QK].set(jnp.cos(ang))
    sin = jnp.zeros((t, V7X_LANES), F32).at[:, MLA_NOPE:MLA_QK].set(jnp.sin(ang))
    return cos, sin


def _pad_heads(w, per_head, offset=0):
    lead = w.shape[:-1]
    w = w.reshape(lead + (MLA_HEADS, per_head))
    out = jnp.zeros(lead + (MLA_HEADS, V7X_LANES), w.dtype).at[..., offset:offset + per_head].set(w)
    return out.reshape(lead + (MLA_PAD,))


def _layer_weights(l, g_mix, w_in, na_q_norm, na_k_norm, mla_q_a_norm, mla_w_q_b, mla_kv_a_norm, mla_w_kv_b,
                   mla_q_norm, mla_k_norm, gla_w_gate, gla_b_gate, gla_o_norm, w_out, g_ffn, w_router, b_router):
    d = w_in.shape[1]
    sizes = (NA_WIDTH, NA_WIDTH, NA_WIDTH, MLA_Q_RANK, MLA_KV_RANK, MLA_ROPE, GLA_KW, GLA_KW, GLA_WIDTH, GLA_WIDTH,
             GLA_GATE_RANK, GLA_GATE_RANK)
    (wq, wk, wv, wcq, wckv, wkr, wgq, wgk, wgv, wgo, wlf, wlb) = jnp.split(w_in[l], np.cumsum(sizes)[:-1].tolist(), axis=-1)
    small = jnp.zeros((d, V7X_LANES), F32)
    small = small.at[:, S_LF:S_LF + GLA_GATE_RANK].set(wlf).at[:, S_LB:S_LB + GLA_GATE_RANK].set(wlb)
    small = small.at[:, S_KR:S_KR + MLA_ROPE].set(wkr)
    w_packed = jnp.concatenate([wq, wk, wv, wcq, wckv, wgq, wgk, wgv, wgo, small], axis=-1).astype(BF16)
    kvb = mla_w_kv_b[l].reshape(MLA_KV_RANK, MLA_HEADS, MLA_NOPE + MLA_V)
    wg = jnp.zeros((V7X_LANES, 2 * GLA_KW), F32)
    wg = wg.at[S_LF:S_LF + GLA_GATE_RANK, :GLA_KW].set(gla_w_gate[l, 0])
    wg = wg.at[S_LB:S_LB + GLA_GATE_RANK, GLA_KW:].set(gla_w_gate[l, 1])
    wg_hi, wg_lo = _split(wg)
    seg = (np.arange(NA_WIDTH)[:, None] // NA_DIM == np.arange(NA_WIDTH)[None, :] // NA_DIM)
    wr = jnp.zeros((d, V7X_LANES), F32).at[:, :N_EXPERTS].set(w_router[l])
    wr_hi, wr_lo = _split(wr)
    wo = w_out[l].astype(BF16)
    return {
        "gmix": g_mix[l][None, :],
        "w_in": w_packed,
        "naqn": jnp.tile(na_q_norm[l], NA_HEADS)[None, :],
        "nakn": jnp.tile(na_k_norm[l], NA_HEADS)[None, :],
        "qan": mla_q_a_norm[l][None, :],
        "wqb": _pad_heads(mla_w_q_b[l], MLA_QK).astype(BF16),
        "kvan": mla_kv_a_norm[l][None, :],
        "wkk": _pad_heads(kvb[:, :, :MLA_NOPE].reshape(MLA_KV_RANK, -1), MLA_NOPE).astype(BF16),
        "wkv": kvb[:, :, MLA_NOPE:].reshape(MLA_KV_RANK, MLA_WIDTH).astype(BF16),
        "mqn": _pad_heads(jnp.tile(mla_q_norm[l], MLA_HEADS), MLA_QK)[None, :],
        "mkn": _pad_heads(jnp.tile(mla_k_norm[l], MLA_HEADS), MLA_QK)[None, :],
        "wg_hi": wg_hi, "wg_lo": wg_lo,
        "bg": jnp.concatenate([gla_b_gate[l, 0], gla_b_gate[l, 1]])[None, :],
        "seg64": jnp.asarray(seg, BF16),
        "onorm": jnp.tile(gla_o_norm[l], GLA_HEADS)[None, :],
        "gffn": g_ffn[l][None, :],
        "wo_a": wo[:NA_WIDTH], "wo_b": wo[NA_WIDTH:NA_WIDTH + MLA_WIDTH], "wo_c": wo[NA_WIDTH + MLA_WIDTH:],
        "wr_hi": wr_hi, "wr_lo": wr_lo,
        "br": jnp.zeros((1, V7X_LANES), F32).at[0, :N_EXPERTS].set(b_router[l]),
    }


def kernel(x, c, ctx, c_ctx, w_ada, b_ada, g_mix, w_in, na_q_norm, na_k_norm, na_rpb, mla_q_a_norm, mla_w_q_b,
           mla_kv_a_norm, mla_w_kv_b, mla_q_norm, mla_k_norm, gla_w_gate, gla_b_gate, gla_o_norm, w_out, g_ffn,
           w_router, b_router, w_moe1, b_moe1, w_moe2, b_moe2):
    bsz, t, d = x.shape
    tc = ctx.shape[1]
    depth = w_ada.shape[0]
    rows = t // GRID_W
    cos, sin = _rope_tables(t)

    pad = (-(bsz + 1)) % 8
    cvec = jnp.concatenate([c, c_ctx[None, :], jnp.zeros((pad, d), F32)], axis=0)
    mod = _ada(cvec, w_ada, b_ada)

    h_ctx = ctx
    for l in range(depth):
        ctx_out = l < depth - 1
        lw = _layer_weights(l, g_mix, w_in, na_q_norm, na_k_norm, mla_q_a_norm, mla_w_q_b, mla_kv_a_norm,
                            mla_w_kv_b, mla_q_norm, mla_k_norm, gla_w_gate, gla_b_gate, gla_o_norm, w_out, g_ffn,
                            w_router, b_router)
        m_lat = [m[:, None, :] for m in jnp.split(mod[l, :bsz], 6, axis=-1)]
        m_ctx = [jnp.broadcast_to(m[None, :, :], (bsz, 1, d)) for m in jnp.split(mod[l, bsz:bsz + 1], 6, axis=-1)]
        sh1, sc1, gt1, sh2, sc2, gt2 = m_lat
        csh1, csc1, cgt1, csh2, csc2, cgt2 = m_ctx

        (naq, nak, nav, mq, mk, mv, gq, gk, gv, gout, gf, gb) = _inproj(x, sc1, sh1, lw, cos, sin, True)
        (cnaq, cnak, cnav, cmq, cmk, cmv, cgq, cgk, cgv, cgout, cgf, cgb) = _inproj(
            h_ctx, csc1, csh1, lw, cos[:tc], sin[:tc], False)

        bias = _na_bias_table(na_rpb[l], rows)
        a_x = _na_attention(naq, nak, nav, cnak, cnav, bias)
        b_x = _attention(mq, [(mk, mv), (cmk, cmv)], MLA_HEADS, V7X_LANES, MLA_V, 256)
        c_x, c_c = _gla((gq, gk, gv, gf, gb, gout), (cgq, cgk, cgv, cgf, cgb, cgout), lw["onorm"], lw["seg64"], ctx_out)
        x_new, h2, logits = _outproj(a_x, b_x, c_x, x, gt1, sc2, sh2, lw)

        toks = [h2.reshape(bsz * t, d)]
        lgs = [logits.reshape(bsz * t, V7X_LANES)]
        if ctx_out:
            a_c = _attention_masked(cnaq, cnak, cnav, NA_DIM)
            b_c = _attention(cmq, [(cmk, cmv)], MLA_HEADS, V7X_LANES, MLA_V, 256)
            hc_new, hc2, clogits = _outproj(a_c, b_c, c_c, h_ctx, cgt1, csc2, csh2, lw)
            toks.append(hc2.reshape(bsz * tc, d))
            lgs.append(clogits.reshape(bsz * tc, V7X_LANES))
        moe = _moe(jnp.concatenate(toks, axis=0), jnp.concatenate(lgs, axis=0)[:, :N_EXPERTS],
                   w_moe1, b_moe1, w_moe2, b_moe2, l)
        x = x_new + gt2 * moe[:bsz * t].reshape(bsz, t, d)
        if ctx_out:
            h_ctx = hc_new + cgt2 * moe[bsz * t:].reshape(bsz, tc, d)
    return x
```

```python
import functools

import numpy as np
import jax
import jax.numpy as jnp
from jax import lax
from jax.experimental import pallas as pl
from jax.experimental.pallas import tpu as pltpu

F32 = jnp.float32
BF16 = jnp.bfloat16

V7X_LANES = 128
V7X_VMEM_BYTES = 64 * 1024 * 1024
DMA_QUEUES = 2

EPS = 1e-6
GRID_W = 64
NA_HEADS, NA_DIM, NA_KH, NA_KW = 4, 64, 8, 16
MLA_HEADS, MLA_NOPE, MLA_ROPE, MLA_V = 4, 64, 32, 128
MLA_QK = MLA_NOPE + MLA_ROPE
MLA_Q_RANK, MLA_KV_RANK = 256, 128
ROPE_AXIS = MLA_ROPE // 2
ROPE_THETA = 10000.0
GLA_HEADS, GLA_DK, GLA_DV = 4, 32, 64
GLA_GATE_RANK, GLA_GATE_NORM, GLA_CHUNK = 16, 16.0, 64
NA_WIDTH = NA_HEADS * NA_DIM
MLA_WIDTH = MLA_HEADS * MLA_V
MLA_PAD = MLA_HEADS * V7X_LANES
GLA_KW = GLA_HEADS * GLA_DK
GLA_WIDTH = GLA_HEADS * GLA_DV
N_EXPERTS, TOP_K = 32, 4
SWIGLU_LIMIT, SWIGLU_ALPHA = 7.0, 1.702
MOE_BLOCK = 512

C_NAQ, C_NAK, C_NAV, C_CQ, C_CKV = 0, 256, 512, 768, 1024
C_GQ, C_GK, C_GV, C_GOUT, C_SMALL = 1152, 1280, 1408, 1664, 1920
IN_PACKED = 2048
S_LF, S_LB, S_KR = 0, 16, 64


def _vmem_limit(mib):
    return min(mib * 1024 * 1024, V7X_VMEM_BYTES - 4 * 1024 * 1024)


def _params(sem, mib=48):
    return pltpu.CompilerParams(dimension_semantics=sem, vmem_limit_bytes=_vmem_limit(mib))


def _dot(a, b):
    return jnp.dot(a, b, preferred_element_type=F32)


def _dot_nt(a, b):
    return lax.dot_general(a, b, (((1,), (1,)), ((), ())), preferred_element_type=F32)


def _dot_tn(a, b):
    return lax.dot_general(a, b, (((0,), (0,)), ((), ())), preferred_element_type=F32)


def _split(x):
    hi = x.astype(BF16)
    lo = (x - hi.astype(F32)).astype(BF16)
    return hi, lo


def _rms(x):
    return x * lax.rsqrt(jnp.mean(x * x, axis=-1, keepdims=True) + EPS)


def _ada_kernel(c_ref, w_ref, b_ref, o_ref):
    c = c_ref[...]
    s = (c * jax.nn.sigmoid(c)).astype(BF16)
    o_ref[...] = _dot(s, w_ref[...].astype(BF16)) + b_ref[...]


def _ada(cvec, w_ada, b_ada):
    depth, d, n6 = w_ada.shape
    r = cvec.shape[0]
    tn = 1024
    return pl.pallas_call(
        _ada_kernel,
        grid=(depth, n6 // tn),
        in_specs=[
            pl.BlockSpec((r, d), lambda l, j: (0, 0)),
            pl.BlockSpec((None, d, tn), lambda l, j: (l, 0, j)),
            pl.BlockSpec((None, 1, tn), lambda l, j: (l, 0, j)),
        ],
        out_specs=pl.BlockSpec((None, r, tn), lambda l, j: (l, 0, j)),
        out_shape=jax.ShapeDtypeStruct((depth, r, n6), F32),
        compiler_params=_params(("parallel", "parallel")),
        name="ada_modulation",
    )(cvec, w_ada, b_ada.reshape(depth, 1, n6))


def _rope(x, cos, sin):
    lane = lax.broadcasted_iota(jnp.int32, (x.shape[0], V7X_LANES), 1)
    first = (lane & (ROPE_AXIS - 1)) < (ROPE_AXIS // 2)
    outs = []
    for h in range(MLA_HEADS):
        xs = x[:, V7X_LANES * h:V7X_LANES * (h + 1)]
        rot = jnp.where(first, -pltpu.roll(xs, V7X_LANES - ROPE_AXIS // 2, 1), pltpu.roll(xs, ROPE_AXIS // 2, 1))
        outs.append(xs * cos + rot * sin)
    return jnp.concatenate(outs, axis=1)


def _head_rms_padded(x, n_real):
    outs = []
    for h in range(MLA_HEADS):
        xs = x[:, V7X_LANES * h:V7X_LANES * (h + 1)]
        ms = jnp.sum(xs * xs, axis=-1, keepdims=True) * (1.0 / n_real)
        outs.append(xs * lax.rsqrt(ms + EPS))
    return jnp.concatenate(outs, axis=1)


def _log_sigmoid(x):
    return jnp.minimum(x, 0.0) - jnp.log1p(jnp.exp(-jnp.abs(x)))


def _inproj_kernel(x_ref, sc_ref, sh_ref, gmix_ref, w_ref, cos_ref, sin_ref,
                   naqn_ref, nakn_ref, qan_ref, wqb_ref, kvan_ref, wkk_ref, wkv_ref, mqn_ref, mkn_ref,
                   wgh_ref, wgl_ref, bg_ref, seg_ref,
                   naq_o, nak_o, nav_o, mq_o, mk_o, mv_o, gq_o, gk_o, gv_o, gout_o, gf_o, gb_o, *, use_rope):
    x = x_ref[...]
    h = _rms(x) * gmix_ref[...]
    h = h * (1.0 + sc_ref[...]) + sh_ref[...]
    p = _dot(h.astype(BF16), w_ref[...])

    seg = seg_ref[...]
    q = p[:, C_NAQ:C_NAQ + NA_WIDTH]
    k = p[:, C_NAK:C_NAK + NA_WIDTH]
    qss = _dot((q * q).astype(BF16), seg) * (1.0 / NA_DIM)
    kss = _dot((k * k).astype(BF16), seg) * (1.0 / NA_DIM)
    naq_o[...] = (q * lax.rsqrt(qss + EPS) * naqn_ref[...] * (NA_DIM ** -0.5)).astype(BF16)
    nak_o[...] = (k * lax.rsqrt(kss + EPS) * nakn_ref[...]).astype(BF16)
    nav_o[...] = p[:, C_NAV:C_NAV + NA_WIDTH].astype(BF16)

    small = p[:, C_SMALL:C_SMALL + V7X_LANES]
    cq = _rms(p[:, C_CQ:C_CQ + MLA_Q_RANK]) * qan_ref[...]
    mq = _head_rms_padded(_dot(cq.astype(BF16), wqb_ref[...]), MLA_QK) * mqn_ref[...]
    ckv = (_rms(p[:, C_CKV:C_CKV + MLA_KV_RANK]) * kvan_ref[...]).astype(BF16)
    lane = lax.broadcasted_iota(jnp.int32, small.shape, 1)
    kr = jnp.where((lane >= S_KR) & (lane < S_KR + MLA_ROPE), small, 0.0)
    mk = _dot(ckv, wkk_ref[...]) + jnp.concatenate([kr] * MLA_HEADS, axis=1)
    mk = _head_rms_padded(mk, MLA_QK) * mkn_ref[...]
    if use_rope:
        cos, sin = cos_ref[...], sin_ref[...]
        mq = _rope(mq, cos, sin)
        mk = _rope(mk, cos, sin)
    mq_o[...] = (mq * (MLA_QK ** -0.5)).astype(BF16)
    mk_o[...] = mk.astype(BF16)
    mv_o[...] = _dot(ckv, wkv_ref[...]).astype(BF16)

    gq_o[...] = p[:, C_GQ:C_GQ + GLA_KW] * (GLA_DK ** -0.5)
    gk_o[...] = p[:, C_GK:C_GK + GLA_KW]
    gv_o[...] = p[:, C_GV:C_GV + GLA_WIDTH]
    gout_o[...] = p[:, C_GOUT:C_GOUT + GLA_WIDTH]
    s_hi, s_lo = _split(small)
    wgh = wgh_ref[...]
    pre = _dot(s_hi, wgh) + _dot(s_lo, wgh) + _dot(s_hi, wgl_ref[...]) + bg_ref[...]
    ls = _log_sigmoid(pre) * (1.0 / GLA_GATE_NORM)
    gf_o[...] = ls[:, :GLA_KW]
    gb_o[...] = ls[:, GLA_KW:]


def _inproj(x, sc, sh, lw, cos, sin, use_rope):
    b, t, d = x.shape
    tm = min(512, t)
    full = lambda a: pl.BlockSpec(a.shape, lambda i, j: (0,) * a.ndim)
    row = lambda w: pl.BlockSpec((None, tm, w), lambda i, j: (i, j, 0))
    consts = [lw["gmix"], lw["w_in"]]
    tail = [lw["naqn"], lw["nakn"], lw["qan"], lw["wqb"], lw["kvan"], lw["wkk"], lw["wkv"], lw["mqn"], lw["mkn"],
            lw["wg_hi"], lw["wg_lo"], lw["bg"], lw["seg64"]]
    widths = [(NA_WIDTH, BF16)] * 3 + [(MLA_PAD, BF16), (MLA_PAD, BF16), (MLA_WIDTH, BF16),
                                       (GLA_KW, F32), (GLA_KW, F32), (GLA_WIDTH, F32), (GLA_WIDTH, F32),
                                       (GLA_KW, F32), (GLA_KW, F32)]
    return pl.pallas_call(
        functools.partial(_inproj_kernel, use_rope=use_rope),
        grid=(b, t // tm),
        in_specs=[row(d),
                  pl.BlockSpec((None, 1, d), lambda i, j: (i, 0, 0)),
                  pl.BlockSpec((None, 1, d), lambda i, j: (i, 0, 0))]
                 + [full(a) for a in consts]
                 + [pl.BlockSpec((tm, V7X_LANES), lambda i, j: (j, 0))] * 2
                 + [full(a) for a in tail],
        out_specs=[row(w) for w, _ in widths],
        out_shape=[jax.ShapeDtypeStruct((b, t, w), dt) for w, dt in widths],
        compiler_params=_params(("parallel", "parallel")),
        name="inproj_rope" if use_rope else "inproj_ctx",
    )(x, sc, sh, *consts, cos, sin, *tail)


def _stack_heads(q, width):
    lane = lax.broadcasted_iota(jnp.int32, q.shape, 1)
    zero = jnp.zeros_like(q)
    return jnp.concatenate([jnp.where(lane // width == h, q, zero) for h in range(NA_HEADS)], axis=0)


def _unstack_heads(o4, n, width):
    lane = lax.broadcasted_iota(jnp.int32, (n, o4.shape[1]), 1)
    out = jnp.zeros((n, o4.shape[1]), F32)
    for h in range(NA_HEADS):
        out = out + jnp.where(lane // width == h, o4[h * n:(h + 1) * n], 0.0)
    return out


def _na_kernel(q_ref, k_ref, v_ref, kc_ref, vc_ref, bias_ref, o_ref, *, rows):
    kc = kc_ref[...]
    vc = vc_ref[...]
    n_loc = NA_KH * GRID_W

    def body(r, carry):
        rs = jnp.clip(r - NA_KH // 2, 0, rows - NA_KH)
        off = r - rs
        q = q_ref[pl.ds(pl.multiple_of(r * GRID_W, GRID_W), GRID_W), :]
        ks = k_ref[pl.ds(pl.multiple_of(rs * GRID_W, GRID_W), n_loc), :]
        vs = v_ref[pl.ds(pl.multiple_of(rs * GRID_W, GRID_W), n_loc), :]
        q4 = _stack_heads(q, NA_DIM)
        s_loc = _dot_nt(q4, ks) + bias_ref[off]
        s_ctx = _dot_nt(q4, kc)
        m = jnp.maximum(jnp.max(s_loc, axis=-1, keepdims=True), jnp.max(s_ctx, axis=-1, keepdims=True))
        p_loc = jnp.exp(s_loc - m)
        p_ctx = jnp.exp(s_ctx - m)
        denom = jnp.sum(p_loc, axis=-1, keepdims=True) + jnp.sum(p_ctx, axis=-1, keepdims=True)
        o4 = (_dot(p_loc.astype(BF16), vs) + _dot(p_ctx.astype(BF16), vc)) * (1.0 / denom)
        o_ref[pl.ds(pl.multiple_of(r * GRID_W, GRID_W), GRID_W), :] = _unstack_heads(o4, GRID_W, NA_DIM).astype(BF16)
        return carry

    lax.fori_loop(0, rows, body, 0)


def _na_bias_table(rpb, rows):
    kh = NA_KH
    r_all = np.arange(rows)
    offs = r_all - np.clip(r_all - kh // 2, 0, rows - kh)
    n_off = int(offs.max()) + 1
    qc = np.arange(GRID_W)
    kcol = np.arange(GRID_W)
    cs = np.clip(qc - NA_KW // 2, 0, GRID_W - NA_KW)
    in_win = (kcol[None, :] >= cs[:, None]) & (kcol[None, :] < cs[:, None] + NA_KW)
    dc = np.clip(kcol[None, :] - qc[:, None] + NA_KW - 1, 0, 2 * NA_KW - 2)
    onehot = (dc[None] == np.arange(2 * NA_KW - 1)[:, None, None]).astype(np.float32)
    toep = jnp.einsum("hdc,cqk->hdqk", rpb.astype(F32), jnp.asarray(onehot), precision=lax.Precision.HIGHEST)
    toep = jnp.where(in_win[None, None], toep, -jnp.inf)
    per_off = [toep[:, NA_KH - 1 - off:2 * NA_KH - 1 - off] for off in range(n_off)]
    bias = jnp.stack(per_off, axis=0)
    return bias.transpose(0, 1, 3, 2, 4).reshape(n_off, NA_HEADS * GRID_W, kh * GRID_W)


def _na_attention(q, k, v, kc, vc, bias):
    b, t, w = q.shape
    tc = kc.shape[1]
    rows = t // GRID_W
    assert rows >= NA_KH and t % GRID_W == 0
    seq = lambda n: pl.BlockSpec((None, n, w), lambda i: (i, 0, 0))
    return pl.pallas_call(
        functools.partial(_na_kernel, rows=rows),
        grid=(b,),
        in_specs=[seq(t), seq(t), seq(t), seq(tc), seq(tc),
                  pl.BlockSpec(bias.shape, lambda i: (0, 0, 0))],
        out_specs=seq(t),
        out_shape=jax.ShapeDtypeStruct((b, t, w), BF16),
        compiler_params=_params(("parallel",)),
        name="na_attention",
    )(q, k, v, kc, vc, bias)


def _softmax_pv(scores, values):
    m = functools.reduce(jnp.maximum, [jnp.max(s, axis=-1, keepdims=True) for s in scores])
    ps = [jnp.exp(s - m) for s in scores]
    denom = functools.reduce(jnp.add, [jnp.sum(p, axis=-1, keepdims=True) for p in ps])
    o = functools.reduce(jnp.add, [_dot(p.astype(BF16), v) for p, v in zip(ps, values)])
    return o * (1.0 / denom)


def _attn_sliced_kernel(*refs, n_parts, heads, dh, dv):
    q_ref, o_ref = refs[0], refs[-1]
    for h in range(heads):
        q = q_ref[:, dh * h:dh * (h + 1)]
        scores = [_dot_nt(q, refs[1 + 2 * p][:, dh * h:dh * (h + 1)]) for p in range(n_parts)]
        values = [refs[2 + 2 * p][:, dv * h:dv * (h + 1)] for p in range(n_parts)]
        o_ref[:, dv * h:dv * (h + 1)] = _softmax_pv(scores, values).astype(o_ref.dtype)


def _attn_masked_kernel(q_ref, k_ref, v_ref, o_ref, *, width):
    n = q_ref.shape[0]
    q4 = _stack_heads(q_ref[...], width)
    o4 = _softmax_pv([_dot_nt(q4, k_ref[...])], [v_ref[...]])
    o_ref[...] = _unstack_heads(o4, n, width).astype(o_ref.dtype)


def _attention(q, parts, heads, dh, dv, tq):
    b, t, wq = q.shape
    tq = min(tq, t)
    flat = [a for kv in parts for a in kv]
    kv_spec = lambda a: pl.BlockSpec((None,) + a.shape[1:], lambda i, j: (i, 0, 0))
    return pl.pallas_call(
        functools.partial(_attn_sliced_kernel, n_parts=len(parts), heads=heads, dh=dh, dv=dv),
        grid=(b, t // tq),
        in_specs=[pl.BlockSpec((None, tq, wq), lambda i, j: (i, j, 0))] + [kv_spec(a) for a in flat],
        out_specs=pl.BlockSpec((None, tq, heads * dv), lambda i, j: (i, j, 0)),
        out_shape=jax.ShapeDtypeStruct((b, t, heads * dv), BF16),
        compiler_params=_params(("parallel", "parallel")),
        name="mla_attention_%d" % len(parts),
    )(q, *flat)


def _attention_masked(q, k, v, width):
    b, t, w = q.shape
    spec = lambda a: pl.BlockSpec((None,) + a.shape[1:], lambda i: (i, 0, 0))
    return pl.pallas_call(
        functools.partial(_attn_masked_kernel, width=width),
        grid=(b,),
        in_specs=[spec(q), spec(k), spec(v)],
        out_specs=spec(q),
        out_shape=jax.ShapeDtypeStruct((b, t, w), BF16),
        compiler_params=_params(("parallel",)),
        name="ctx_na_attention",
    )(q, k, v)


GLA_SAFE_DECAY = 60.0
GLA_EXACT_CHUNK = 16


def _gla_masks(c):
    ri = lax.broadcasted_iota(jnp.int32, (c, c), 0)
    ci = lax.broadcasted_iota(jnp.int32, (c, c), 1)
    ti = lax.broadcasted_iota(jnp.int32, (c, GLA_HEADS * c), 0)
    si = lax.broadcasted_iota(jnp.int32, (c, GLA_HEADS * c), 1) % c
    lk = lax.broadcasted_iota(jnp.int32, (c, GLA_KW), 1) // GLA_DK
    lv = lax.broadcasted_iota(jnp.int32, (c, GLA_WIDTH), 1) // GLA_DV
    head_v = lax.broadcasted_iota(jnp.int32, (GLA_WIDTH, GLA_KW), 0) // GLA_DV
    head_k = lax.broadcasted_iota(jnp.int32, (GLA_WIDTH, GLA_KW), 1) // GLA_DK
    return {
        "tri": ((ci <= ri).astype(BF16), (ci >= ri).astype(BF16)),
        "allowed": (si <= ti, si >= ti),
        "hm_k": [lk == h for h in range(GLA_HEADS)],
        "hm_v": [lv == h for h in range(GLA_HEADS)],
        "hm_s": head_v == head_k,
        "expand": (lax.broadcasted_iota(jnp.int32, (GLA_KW, GLA_WIDTH), 0) // GLA_DK
                   == lax.broadcasted_iota(jnp.int32, (GLA_KW, GLA_WIDTH), 1) // GLA_DV).astype(BF16),
    }


def _gla_intra_pairwise(q, k, v, bcum, expand, reverse):
    c = q.shape[0]
    row = lax.broadcasted_iota(jnp.int32, (c, GLA_WIDTH), 0)
    outs = []
    for t in range(c):
        w = jnp.exp(jnp.minimum(bcum[t:t + 1, :] - bcum, 0.0))
        a = _dot(((q[t:t + 1, :] * k) * w).astype(BF16), expand)
        keep = (row >= t) if reverse else (row <= t)
        outs.append(jnp.sum(jnp.where(keep, a * v, 0.0), axis=0, keepdims=True))
    return jnp.concatenate(outs, axis=0)


def _gla_chunk(q, k, v, g, st_ref, m, reverse, pairwise):
    c = q.shape[0]
    d = 1 if reverse else 0
    end_row = 0 if reverse else c - 1
    g_hi, g_lo = _split(g)
    tri = m["tri"][d]
    bcum = _dot(tri, g_hi) + _dot(tri, g_lo)
    bend = bcum[end_row:end_row + 1, :]
    qe = (q * jnp.exp(bcum)).astype(BF16)
    ke = (k * jnp.exp(bend - bcum)).astype(BF16)
    st = st_ref[...]
    o = _dot_nt(qe, st.astype(BF16))
    if pairwise:
        o = o + _gla_intra_pairwise(q, k, v, bcum, m["expand"], reverse)
    else:
        kt = k * jnp.exp(-bcum)
        kst = jnp.concatenate([jnp.where(hm, kt, 0.0) for hm in m["hm_k"]], axis=0).astype(BF16)
        a = jnp.where(m["allowed"][d], _dot_nt(qe, kst), 0.0)
        vbd = jnp.concatenate([jnp.where(hm, v, 0.0) for hm in m["hm_v"]], axis=0).astype(BF16)
        o = o + _dot(a.astype(BF16), vbd)
    upd = _dot_tn(v.astype(BF16), ke)
    st_ref[...] = st * jnp.exp(bend) + jnp.where(m["hm_s"], upd, 0.0)
    return o


def _gla_kernel(q_ref, k_ref, v_ref, gf_ref, gb_ref, go_ref,
                qc_ref, kc_ref, vc_ref, gfc_ref, gbc_ref, goc_ref, onorm_ref, seg_ref,
                *rest, ctx_out):
    if ctx_out:
        ox_ref, oc_ref, acc_ref, accc_ref, sf_ref, sb_ref = rest
    else:
        ox_ref, acc_ref, sf_ref, sb_ref = rest
        oc_ref = accc_ref = None
    t, tc = q_ref.shape[0], qc_ref.shape[0]

    def sweep(qr, kr, vr, gfr, gbr, dst, total, c, m, pairwise):
        count = total // c

        def body(j, carry):
            lo_f = pl.multiple_of(j * c, c)
            lo_b = pl.multiple_of((count - 1 - j) * c, c)
            of = _gla_chunk(qr[pl.ds(lo_f, c), :], kr[pl.ds(lo_f, c), :], vr[pl.ds(lo_f, c), :],
                            gfr[pl.ds(lo_f, c), :], sf_ref, m, False, pairwise)
            ob = _gla_chunk(qr[pl.ds(lo_b, c), :], kr[pl.ds(lo_b, c), :], vr[pl.ds(lo_b, c), :],
                            gbr[pl.ds(lo_b, c), :], sb_ref, m, True, pairwise)
            if dst is not None:
                dst[pl.ds(lo_f, c), :] += of
                dst[pl.ds(lo_b, c), :] += ob
            return carry
        lax.fori_loop(0, count, body, 0)

    def scans(c, pairwise):
        m = _gla_masks(c)
        sf_ref[...] = jnp.zeros_like(sf_ref)
        sb_ref[...] = jnp.zeros_like(sb_ref)
        acc_ref[...] = jnp.zeros_like(acc_ref)
        if ctx_out:
            accc_ref[...] = jnp.zeros_like(accc_ref)
        sweep(qc_ref, kc_ref, vc_ref, gfc_ref, gbc_ref, accc_ref, tc, c, m, pairwise)
        sweep(q_ref, k_ref, v_ref, gf_ref, gb_ref, acc_ref, t, c, m, pairwise)

    def min_chunk_decay(gr, total):
        def body(j, low):
            lo = pl.multiple_of(j * GLA_CHUNK, GLA_CHUNK)
            return jnp.minimum(low, jnp.sum(gr[pl.ds(lo, GLA_CHUNK), :], axis=0, keepdims=True))
        return lax.fori_loop(0, total // GLA_CHUNK, body, jnp.zeros((1, GLA_KW), F32))

    low = functools.reduce(jnp.minimum, [min_chunk_decay(gf_ref, t), min_chunk_decay(gb_ref, t),
                                         min_chunk_decay(gfc_ref, tc), min_chunk_decay(gbc_ref, tc)])
    safe = jnp.min(low) >= -GLA_SAFE_DECAY
    pl.when(safe)(functools.partial(scans, GLA_CHUNK, False))
    pl.when(jnp.logical_not(safe))(functools.partial(scans, GLA_EXACT_CHUNK, True))

    def finish(acc, gate, out, total):
        tr = min(256, total)

        def body(i, carry):
            lo = pl.multiple_of(i * tr, tr)
            o = acc[pl.ds(lo, tr), :]
            ms = _dot((o * o).astype(BF16), seg_ref[...]) * (1.0 / GLA_DV)
            gt = gate[pl.ds(lo, tr), :]
            y = o * lax.rsqrt(ms + EPS) * onorm_ref[...] * (gt * jax.nn.sigmoid(gt))
            out[pl.ds(lo, tr), :] = y.astype(out.dtype)
            return carry
        lax.fori_loop(0, total // tr, body, 0)

    finish(acc_ref, go_ref, ox_ref, t)
    if ctx_out:
        finish(accc_ref, goc_ref, oc_ref, tc)


def _gla(lat, ctx, onorm, seg64, ctx_out):
    b, t, _ = lat[0].shape
    tc = ctx[0].shape[1]
    spec = lambda a: pl.BlockSpec((None,) + a.shape[1:], lambda i: (i, 0, 0))
    full = lambda a: pl.BlockSpec(a.shape, lambda i: (0,) * a.ndim)
    out_shape = [jax.ShapeDtypeStruct((b, t, GLA_WIDTH), BF16)]
    scratch = [pltpu.VMEM((t, GLA_WIDTH), F32)]
    if ctx_out:
        out_shape.append(jax.ShapeDtypeStruct((b, tc, GLA_WIDTH), BF16))
        scratch.append(pltpu.VMEM((tc, GLA_WIDTH), F32))
    scratch += [pltpu.VMEM((GLA_WIDTH, GLA_KW), F32)] * 2
    outs = pl.pallas_call(
        functools.partial(_gla_kernel, ctx_out=ctx_out),
        grid=(b,),
        in_specs=[spec(a) for a in lat] + [spec(a) for a in ctx] + [full(onorm), full(seg64)],
        out_specs=[spec(s) for s in out_shape],
        out_shape=out_shape,
        scratch_shapes=scratch,
        compiler_params=_params(("parallel",)),
        name="gla_ctx_out" if ctx_out else "gla_last",
    )(*lat, *ctx, onorm, seg64)
    return outs if ctx_out else (outs[0], None)


def _outproj_kernel(a_ref, b_ref, c_ref, x_ref, gt_ref, sc_ref, sh_ref, gffn_ref,
                    wa_ref, wb_ref, wc_ref, wrh_ref, wrl_ref, br_ref, xo_ref, h_ref, lg_ref):
    mix = _dot(a_ref[...], wa_ref[...]) + _dot(b_ref[...], wb_ref[...]) + _dot(c_ref[...], wc_ref[...])
    xn = x_ref[...] + gt_ref[...] * mix
    xo_ref[...] = xn
    h = _rms(xn) * gffn_ref[...]
    h = h * (1.0 + sc_ref[...]) + sh_ref[...]
    h_ref[...] = h
    h_hi, h_lo = _split(h)
    wrh = wrh_ref[...]
    lg_ref[...] = _dot(h_hi, wrh) + _dot(h_lo, wrh) + _dot(h_hi, wrl_ref[...]) + br_ref[...]


def _outproj(a, bm, c, x, gt, sc, sh, lw):
    b, t, d = x.shape
    tm = min(512, t)
    row = lambda w: pl.BlockSpec((None, tm, w), lambda i, j: (i, j, 0))
    vec = pl.BlockSpec((None, 1, d), lambda i, j: (i, 0, 0))
    full = lambda arr: pl.BlockSpec(arr.shape, lambda i, j: (0,) * arr.ndim)
    consts = [lw["gffn"], lw["wo_a"], lw["wo_b"], lw["wo_c"], lw["wr_hi"], lw["wr_lo"], lw["br"]]
    return pl.pallas_call(
        _outproj_kernel,
        grid=(b, t // tm),
        in_specs=[row(NA_WIDTH), row(MLA_WIDTH), row(GLA_WIDTH), row(d), vec, vec, vec] + [full(w) for w in consts],
        out_specs=[row(d), row(d), row(V7X_LANES)],
        out_shape=[jax.ShapeDtypeStruct((b, t, d), F32), jax.ShapeDtypeStruct((b, t, d), F32),
                   jax.ShapeDtypeStruct((b, t, V7X_LANES), F32)],
        compiler_params=_params(("parallel", "parallel")),
        name="outproj_router",
    )(a, bm, c, x, gt, sc, sh, *consts)


def _moe_kernel(be_ref, tok_hbm, pair_hbm, h_hbm, w1_ref, b1_ref, w2_ref, b2_ref, y_hbm,
                g0, g1, s0, s1, x0, x1, y0, y1, w1b_ref, w2b_ref, gisem, sisem, gsem, ssem):
    i = pl.program_id(0)
    last = pl.num_programs(0) - 1
    blk = x0.shape[0]
    f = w2_ref.shape[0]
    gbuf, sbuf, xbuf, ybuf = (g0, g1), (s0, s1), (x0, x1), (y0, y1)

    def load_tok(t, s):
        return pltpu.make_async_copy(tok_hbm.at[pl.ds(pl.multiple_of(t * blk, blk), blk)], gbuf[s], gisem.at[s])

    def load_pair(t, s):
        return pltpu.make_async_copy(pair_hbm.at[pl.ds(pl.multiple_of(t * blk, blk), blk)], sbuf[s], sisem.at[s])

    def start_gathers(s):
        for r in range(blk):
            pltpu.make_async_copy(h_hbm.at[gbuf[s][r]], xbuf[s].at[r], gsem.at[s]).start(priority=r % DMA_QUEUES)

    def wait_gathers(s):
        for r in range(blk):
            pltpu.make_async_copy(h_hbm.at[0], xbuf[s].at[r], gsem.at[s]).wait()

    def start_scatters(s):
        for r in range(blk):
            pltpu.make_async_copy(ybuf[s].at[r], y_hbm.at[sbuf[s][r]], ssem.at[s]).start(priority=r % DMA_QUEUES)

    def wait_scatters(s):
        for r in range(blk):
            pltpu.make_async_copy(ybuf[s].at[r], y_hbm.at[0], ssem.at[s]).wait()

    @pl.when(i == 0)
    def _():
        y0[...] = jnp.zeros_like(y0)
        y1[...] = jnp.zeros_like(y1)
        first_spare = y_hbm.shape[0] - 2 * blk
        for r in range(blk):
            pltpu.make_async_copy(y0.at[r], y_hbm.at[first_spare + r], ssem.at[0]).start()
        load_tok(1, 0).start()
        load_tok(2, 1).start()
        load_pair(0, 1).start()
        load_tok(1, 0).wait()
        start_gathers(0)

    e = be_ref[i]
    prev = be_ref[jnp.maximum(i - 1, 0)]

    @pl.when((i == 0) | (e != prev))
    def _():
        rows = 128

        def cast1(j, carry):
            lo = pl.multiple_of(j * rows, rows)
            w1b_ref[pl.ds(lo, rows), :] = w1_ref[pl.ds(lo, rows), :].astype(BF16)
            return carry

        def cast2(j, carry):
            lo = pl.multiple_of(j * rows, rows)
            w2b_ref[pl.ds(lo, rows), :] = w2_ref[pl.ds(lo, rows), :].astype(BF16)
            return carry
        lax.fori_loop(0, w1_ref.shape[0] // rows, cast1, 0)
        lax.fori_loop(0, w2_ref.shape[0] // rows, cast2, 0)

    def step(p):
        q = 1 - p

        load_tok(i + 2, q).wait()
        load_pair(i, q).wait()
        load_tok(i + 3, p).start()
        load_pair(i + 1, p).start()
        wait_gathers(p)
        wait_scatters(p)
        start_gathers(q)
        start_scatters(q)

        hb = _dot(xbuf[p][...].astype(BF16), w1b_ref[...]) + b1_ref[...]
        glu = jnp.minimum(hb[:, :f], SWIGLU_LIMIT)
        lin = jnp.clip(hb[:, f:], -SWIGLU_LIMIT, SWIGLU_LIMIT)
        act = (lin + 1.0) * (glu * jax.nn.sigmoid(SWIGLU_ALPHA * glu))
        ybuf[p][...] = _dot(act.astype(BF16), w2b_ref[...]) + b2_ref[...]

        @pl.when(i == last)
        def _():
            load_tok(i + 3, p).wait()
            load_pair(i + 1, p).wait()
            wait_gathers(q)
            wait_scatters(q)
            start_scatters(p)
            wait_scatters(p)

    for p in range(2):
        pl.when(i % 2 == p)(functools.partial(step, p))


def _moe_blocks(h, tok_tab, pair_tab, blk_e, n_rows_out, w1, b1, w2, b2, layer):
    n_tok, d = h.shape
    n_blk = blk_e.shape[0]
    f2 = w1.shape[-1]
    f = w2.shape[-2]
    grid_spec = pltpu.PrefetchScalarGridSpec(
        num_scalar_prefetch=1,
        grid=(n_blk,),
        in_specs=[
            pl.BlockSpec(memory_space=pl.ANY),
            pl.BlockSpec(memory_space=pl.ANY),
            pl.BlockSpec(memory_space=pl.ANY),
            pl.BlockSpec((None, None, d, f2), lambda i, be: (layer, be[i], 0, 0)),
            pl.BlockSpec((None, None, 1, f2), lambda i, be: (layer, be[i], 0, 0)),
            pl.BlockSpec((None, None, f, d), lambda i, be: (layer, be[i], 0, 0)),
            pl.BlockSpec((None, None, 1, d), lambda i, be: (layer, be[i], 0, 0)),
        ],
        out_specs=pl.BlockSpec(memory_space=pl.ANY),
        scratch_shapes=[pltpu.SMEM((MOE_BLOCK,), jnp.int32)] * 4 + [pltpu.VMEM((MOE_BLOCK, d), F32)] * 4 + [
            pltpu.VMEM((d, f2), BF16),
            pltpu.VMEM((f, d), BF16),
        ] + [pltpu.SemaphoreType.DMA((2,))] * 4,
    )
    return pl.pallas_call(
        _moe_kernel,
        grid_spec=grid_spec,
        out_shape=jax.ShapeDtypeStruct((n_rows_out, d), F32),
        compiler_params=_params(("arbitrary",), mib=56),
        name="moe_experts",
    )(blk_e, tok_tab, pair_tab, h, w1, b1.reshape(b1.shape[0], b1.shape[1], 1, f2), w2,
      b2.reshape(b2.shape[0], b2.shape[1], 1, d))


def _moe(h, logits, w1, b1, w2, b2, layer):
    n, d = h.shape
    nk = n * TOP_K
    top_val, top_idx = lax.top_k(logits, TOP_K)
    gates = jax.nn.softmax(top_val, axis=-1)
    flat_e = top_idx.reshape(-1)
    order = jnp.argsort(flat_e).astype(jnp.int32)
    counts = jnp.bincount(flat_e, length=N_EXPERTS).astype(jnp.int32)
    padded = (counts + MOE_BLOCK - 1) // MOE_BLOCK * MOE_BLOCK
    start = jnp.cumsum(counts) - counts
    pend = jnp.cumsum(padded)
    pstart = pend - padded
    n_blk = -(-nk // MOE_BLOCK) + N_EXPERTS
    blk_id = jnp.arange(-1, n_blk + 2, dtype=jnp.int32)
    blk_lo = blk_id * MOE_BLOCK
    blk_e = jnp.minimum(jnp.sum((pend[None, :] <= blk_lo[:, None]).astype(jnp.int32), axis=1), N_EXPERTS - 1)
    s0 = start[blk_e] + blk_lo - pstart[blk_e]
    left = jnp.where(blk_id >= 0, start[blk_e] + counts[blk_e] - s0, 0)
    r = jnp.arange(MOE_BLOCK, dtype=jnp.int32)[None, :]
    valid = r < left[:, None]
    win = order[jnp.clip(s0[:, None] + r, 0, nk - 1)]
    tok = lax.shift_right_logical(win, 2)
    dst = (win & (TOP_K - 1)) * n + tok
    dst_tab = jnp.where(valid, dst, nk + (blk_id[:, None] % 2) * MOE_BLOCK + r).reshape(-1)
    tok_tab = jnp.where(valid, tok, 0).reshape(-1)
    y4 = _moe_blocks(h, tok_tab, dst_tab, blk_e[1:n_blk + 1], nk + 2 * MOE_BLOCK, w1, b1, w2, b2, layer)
    return y4, gates


def _combine_kernel(x_ref, gt_ref, g_ref, y0_ref, y1_ref, y2_ref, y3_ref, o_ref):
    g = g_ref[...]
    mix = g[:, 0:1] * y0_ref[...]
    for k, y_ref in enumerate((y1_ref, y2_ref, y3_ref), start=1):
        mix = mix + g[:, k:k + 1] * y_ref[...]
    o_ref[...] = x_ref[...] + gt_ref[...] * mix


def _combine(x, gt, gates, y4, n_tok, first_tok):
    b, t, d = x.shape
    tm = int(min(512, np.gcd.reduce([t, n_tok, first_tok or n_tok])))
    per_b = t // tm
    plane = n_tok // tm
    first = first_tok // tm
    row = pl.BlockSpec((None, tm, d), lambda i, j: (i, j, 0))
    y_spec = lambda k: pl.BlockSpec((tm, d), lambda i, j: (k * plane + first + i * per_b + j, 0))
    return pl.pallas_call(
        _combine_kernel,
        grid=(b, per_b),
        in_specs=[row, pl.BlockSpec((None, 1, d), lambda i, j: (i, 0, 0)),
                  pl.BlockSpec((None, tm, TOP_K), lambda i, j: (i, j, 0))] + [y_spec(k) for k in range(TOP_K)],
        out_specs=row,
        out_shape=jax.ShapeDtypeStruct((b, t, d), F32),
        compiler_params=_params(("parallel", "parallel")),
        name="moe_combine",
    )(x, gt, gates, y4, y4, y4, y4)


def _rope_tables(t):
    tok = jnp.arange(t)
    row = (tok // GRID_W).astype(F32)[:, None]
    col = (tok % GRID_W).astype(F32)[:, None]
    inv = 1.0 / (ROPE_THETA ** (jnp.arange(0, ROPE_AXIS, 2, dtype=F32) / ROPE_AXIS))
    ang = jnp.concatenate([row * inv, row * inv, col * inv, col * inv], axis=-1)
    cos = jnp.ones((t, V7X_LANES), F32).at[:, MLA_NOPE:MLA_QK].set(jnp.cos(ang))
    sin = jnp.zeros((t, V7X_LANES), F32).at[:, MLA_NOPE:MLA_QK].set(jnp.sin(ang))
    return cos, sin


def _pad_heads(w, per_head, offset=0):
    lead = w.shape[:-1]
    w = w.reshape(lead + (MLA_HEADS, per_head))
    out = jnp.zeros(lead + (MLA_HEADS, V7X_LANES), w.dtype).at[..., offset:offset + per_head].set(w)
    return out.reshape(lead + (MLA_PAD,))


def _layer_weights(l, g_mix, w_in, na_q_norm, na_k_norm, mla_q_a_norm, mla_w_q_b, mla_kv_a_norm, mla_w_kv_b,
                   mla_q_norm, mla_k_norm, gla_w_gate, gla_b_gate, gla_o_norm, w_out, g_ffn, w_router, b_router):
    d = w_in.shape[1]
    sizes = (NA_WIDTH, NA_WIDTH, NA_WIDTH, MLA_Q_RANK, MLA_KV_RANK, MLA_ROPE, GLA_KW, GLA_KW, GLA_WIDTH, GLA_WIDTH,
             GLA_GATE_RANK, GLA_GATE_RANK)
    (wq, wk, wv, wcq, wckv, wkr, wgq, wgk, wgv, wgo, wlf, wlb) = jnp.split(w_in[l], np.cumsum(sizes)[:-1].tolist(), axis=-1)
    small = jnp.zeros((d, V7X_LANES), F32)
    small = small.at[:, S_LF:S_LF + GLA_GATE_RANK].set(wlf).at[:, S_LB:S_LB + GLA_GATE_RANK].set(wlb)
    small = small.at[:, S_KR:S_KR + MLA_ROPE].set(wkr)
    w_packed = jnp.concatenate([wq, wk, wv, wcq, wckv, wgq, wgk, wgv, wgo, small], axis=-1).astype(BF16)
    kvb = mla_w_kv_b[l].reshape(MLA_KV_RANK, MLA_HEADS, MLA_NOPE + MLA_V)
    wg = jnp.zeros((V7X_LANES, 2 * GLA_KW), F32)
    wg = wg.at[S_LF:S_LF + GLA_GATE_RANK, :GLA_KW].set(gla_w_gate[l, 0])
    wg = wg.at[S_LB:S_LB + GLA_GATE_RANK, GLA_KW:].set(gla_w_gate[l, 1])
    wg_hi, wg_lo = _split(wg)
    seg = (np.arange(NA_WIDTH)[:, None] // NA_DIM == np.arange(NA_WIDTH)[None, :] // NA_DIM)
    wr = jnp.zeros((d, V7X_LANES), F32).at[:, :N_EXPERTS].set(w_router[l])
    wr_hi, wr_lo = _split(wr)
    wo = w_out[l].astype(BF16)
    return {
        "gmix": g_mix[l][None, :],
        "w_in": w_packed,
        "naqn": jnp.tile(na_q_norm[l], NA_HEADS)[None, :],
        "nakn": jnp.tile(na_k_norm[l], NA_HEADS)[None, :],
        "qan": mla_q_a_norm[l][None, :],
        "wqb": _pad_heads(mla_w_q_b[l], MLA_QK).astype(BF16),
        "kvan": mla_kv_a_norm[l][None, :],
        "wkk": _pad_heads(kvb[:, :, :MLA_NOPE].reshape(MLA_KV_RANK, -1), MLA_NOPE).astype(BF16),
        "wkv": kvb[:, :, MLA_NOPE:].reshape(MLA_KV_RANK, MLA_WIDTH).astype(BF16),
        "mqn": _pad_heads(jnp.tile(mla_q_norm[l], MLA_HEADS), MLA_QK)[None, :],
        "mkn": _pad_heads(jnp.tile(mla_k_norm[l], MLA_HEADS), MLA_QK)[None, :],
        "wg_hi": wg_hi, "wg_lo": wg_lo,
        "bg": jnp.concatenate([gla_b_gate[l, 0], gla_b_gate[l, 1]])[None, :],
        "seg64": jnp.asarray(seg, BF16),
        "onorm": jnp.tile(gla_o_norm[l], GLA_HEADS)[None, :],
        "gffn": g_ffn[l][None, :],
        "wo_a": wo[:NA_WIDTH], "wo_b": wo[NA_WIDTH:NA_WIDTH + MLA_WIDTH], "wo_c": wo[NA_WIDTH + MLA_WIDTH:],
        "wr_hi": wr_hi, "wr_lo": wr_lo,
        "br": jnp.zeros((1, V7X_LANES), F32).at[0, :N_EXPERTS].set(b_router[l]),
    }


def kernel(x, c, ctx, c_ctx, w_ada, b_ada, g_mix, w_in, na_q_norm, na_k_norm, na_rpb, mla_q_a_norm, mla_w_q_b,
           mla_kv_a_norm, mla_w_kv_b, mla_q_norm, mla_k_norm, gla_w_gate, gla_b_gate, gla_o_norm, w_out, g_ffn,
           w_router, b_router, w_moe1, b_moe1, w_moe2, b_moe2):
    bsz, t, d = x.shape
    tc = ctx.shape[1]
    depth = w_ada.shape[0]
    rows = t // GRID_W
    cos, sin = _rope_tables(t)

    pad = (-(bsz + 1)) % 8
    cvec = jnp.concatenate([c, c_ctx[None, :], jnp.zeros((pad, d), F32)], axis=0)
    mod = _ada(cvec, w_ada, b_ada)

    h_ctx = ctx
    for l in range(depth):
        ctx_out = l < depth - 1
        lw = _layer_weights(l, g_mix, w_in, na_q_norm, na_k_norm, mla_q_a_norm, mla_w_q_b, mla_kv_a_norm,
                            mla_w_kv_b, mla_q_norm, mla_k_norm, gla_w_gate, gla_b_gate, gla_o_norm, w_out, g_ffn,
                            w_router, b_router)
        m_lat = [m[:, None, :] for m in jnp.split(mod[l, :bsz], 6, axis=-1)]
        m_ctx = [jnp.broadcast_to(m[None, :, :], (bsz, 1, d)) for m in jnp.split(mod[l, bsz:bsz + 1], 6, axis=-1)]
        sh1, sc1, gt1, sh2, sc2, gt2 = m_lat
        csh1, csc1, cgt1, csh2, csc2, cgt2 = m_ctx

        (naq, nak, nav, mq, mk, mv, gq, gk, gv, gout, gf, gb) = _inproj(x, sc1, sh1, lw, cos, sin, True)
        (cnaq, cnak, cnav, cmq, cmk, cmv, cgq, cgk, cgv, cgout, cgf, cgb) = _inproj(
            h_ctx, csc1, csh1, lw, cos[:tc], sin[:tc], False)

        bias = _na_bias_table(na_rpb[l], rows)
        a_x = _na_attention(naq, nak, nav, cnak, cnav, bias)
        b_x = _attention(mq, [(mk, mv), (cmk, cmv)], MLA_HEADS, V7X_LANES, MLA_V, 256)
        c_x, c_c = _gla((gq, gk, gv, gf, gb, gout), (cgq, cgk, cgv, cgf, cgb, cgout), lw["onorm"], lw["seg64"], ctx_out)
        x_new, h2, logits = _outproj(a_x, b_x, c_x, x, gt1, sc2, sh2, lw)

        toks = [h2.reshape(bsz * t, d)]
        lgs = [logits.reshape(bsz * t, V7X_LANES)]
        if ctx_out:
            a_c = _attention_masked(cnaq, cnak, cnav, NA_DIM)
            b_c = _attention(cmq, [(cmk, cmv)], MLA_HEADS, V7X_LANES, MLA_V, 256)
            hc_new, hc2, clogits = _outproj(a_c, b_c, c_c, h_ctx, cgt1, csc2, csh2, lw)
            toks.append(hc2.reshape(bsz * tc, d))
            lgs.append(clogits.reshape(bsz * tc, V7X_LANES))
        n_tok = sum(a.shape[0] for a in toks)
        y4, gates = _moe(jnp.concatenate(toks, axis=0), jnp.concatenate(lgs, axis=0)[:, :N_EXPERTS],
                         w_moe1, b_moe1, w_moe2, b_moe2, l)
        x = _combine(x_new, gt2, gates[:bsz * t].reshape(bsz, t, TOP_K), y4, n_tok, 0)
        if ctx_out:
            h_ctx = _combine(hc_new, cgt2, gates[bsz * t:].reshape(bsz, tc, TOP_K), y4, n_tok, bsz * t)
    return x
```

```python
import functools

import numpy as np
import jax
import jax.numpy as jnp
from jax import lax
from jax.experimental import pallas as pl
from jax.experimental.pallas import tpu as pltpu

F32 = jnp.float32
BF16 = jnp.bfloat16

V7X_LANES = 128
V7X_VMEM_BYTES = 64 * 1024 * 1024
DMA_QUEUES = 2

EPS = 1e-6
GRID_W = 64
NA_HEADS, NA_DIM, NA_KH, NA_KW = 4, 64, 8, 16
MLA_HEADS, MLA_NOPE, MLA_ROPE, MLA_V = 4, 64, 32, 128
MLA_QK = MLA_NOPE + MLA_ROPE
MLA_Q_RANK, MLA_KV_RANK = 256, 128
ROPE_AXIS = MLA_ROPE // 2
ROPE_THETA = 10000.0
GLA_HEADS, GLA_DK, GLA_DV = 4, 32, 64
GLA_GATE_RANK, GLA_GATE_NORM, GLA_CHUNK = 16, 16.0, 64
NA_WIDTH = NA_HEADS * NA_DIM
MLA_WIDTH = MLA_HEADS * MLA_V
MLA_PAD = MLA_HEADS * V7X_LANES
GLA_KW = GLA_HEADS * GLA_DK
GLA_WIDTH = GLA_HEADS * GLA_DV
N_EXPERTS, TOP_K = 32, 4
SWIGLU_LIMIT, SWIGLU_ALPHA = 7.0, 1.702
MOE_BLOCK = 512

C_NAQ, C_NAK, C_NAV, C_CQ, C_CKV = 0, 256, 512, 768, 1024
C_GQ, C_GK, C_GV, C_GOUT, C_SMALL = 1152, 1280, 1408, 1664, 1920
IN_PACKED = 2048
S_LF, S_LB, S_KR = 0, 16, 64


def _vmem_limit(mib):
    return min(mib * 1024 * 1024, V7X_VMEM_BYTES - 4 * 1024 * 1024)


def _params(sem, mib=48):
    return pltpu.CompilerParams(dimension_semantics=sem, vmem_limit_bytes=_vmem_limit(mib))


def _dot(a, b):
    return jnp.dot(a, b, preferred_element_type=F32)


def _dot_nt(a, b):
    return lax.dot_general(a, b, (((1,), (1,)), ((), ())), preferred_element_type=F32)


def _dot_tn(a, b):
    return lax.dot_general(a, b, (((0,), (0,)), ((), ())), preferred_element_type=F32)


def _split(x):
    hi = x.astype(BF16)
    lo = (x - hi.astype(F32)).astype(BF16)
    return hi, lo


def _rms(x):
    return x * lax.rsqrt(jnp.mean(x * x, axis=-1, keepdims=True) + EPS)


V7X_SUBLANES = 8


def _store_row_tiled(ref, x):
    n = x.shape[0]
    for j in range(V7X_SUBLANES):
        ref[pl.ds(j, n, stride=V7X_SUBLANES), :] = x[:, V7X_LANES * j:V7X_LANES * (j + 1)]


def _load_row_tiled(ref):
    n = ref.shape[0] // V7X_SUBLANES
    return jnp.concatenate([ref[pl.ds(j, n, stride=V7X_SUBLANES), :] for j in range(V7X_SUBLANES)], axis=1)


def _ada_kernel(c_ref, w_ref, b_ref, o_ref):
    c = c_ref[...]
    s = (c * jax.nn.sigmoid(c)).astype(BF16)
    o_ref[...] = _dot(s, w_ref[...].astype(BF16)) + b_ref[...]


def _ada(cvec, w_ada, b_ada):
    depth, d, n6 = w_ada.shape
    r = cvec.shape[0]
    tn = 1024
    return pl.pallas_call(
        _ada_kernel,
        grid=(depth, n6 // tn),
        in_specs=[
            pl.BlockSpec((r, d), lambda l, j: (0, 0)),
            pl.BlockSpec((None, d, tn), lambda l, j: (l, 0, j)),
            pl.BlockSpec((None, 1, tn), lambda l, j: (l, 0, j)),
        ],
        out_specs=pl.BlockSpec((None, r, tn), lambda l, j: (l, 0, j)),
        out_shape=jax.ShapeDtypeStruct((depth, r, n6), F32),
        compiler_params=_params(("parallel", "parallel")),
        name="ada_modulation",
    )(cvec, w_ada, b_ada.reshape(depth, 1, n6))


def _rope(x, cos, sin):
    lane = lax.broadcasted_iota(jnp.int32, (x.shape[0], V7X_LANES), 1)
    first = (lane & (ROPE_AXIS - 1)) < (ROPE_AXIS // 2)
    outs = []
    for h in range(MLA_HEADS):
        xs = x[:, V7X_LANES * h:V7X_LANES * (h + 1)]
        rot = jnp.where(first, -pltpu.roll(xs, V7X_LANES - ROPE_AXIS // 2, 1), pltpu.roll(xs, ROPE_AXIS // 2, 1))
        outs.append(xs * cos + rot * sin)
    return jnp.concatenate(outs, axis=1)


def _head_rms_padded(x, n_real):
    outs = []
    for h in range(MLA_HEADS):
        xs = x[:, V7X_LANES * h:V7X_LANES * (h + 1)]
        ms = jnp.sum(xs * xs, axis=-1, keepdims=True) * (1.0 / n_real)
        outs.append(xs * lax.rsqrt(ms + EPS))
    return jnp.concatenate(outs, axis=1)


def _log_sigmoid(x):
    return jnp.minimum(x, 0.0) - jnp.log1p(jnp.exp(-jnp.abs(x)))


def _inproj_kernel(x_ref, sc_ref, sh_ref, gmix_ref, w_ref, cos_ref, sin_ref,
                   naqn_ref, nakn_ref, qan_ref, wqb_ref, kvan_ref, wkk_ref, wkv_ref, mqn_ref, mkn_ref,
                   wgh_ref, wgl_ref, bg_ref, seg_ref,
                   naq_o, nak_o, nav_o, mq_o, mk_o, mv_o, gq_o, gk_o, gv_o, gout_o, gf_o, gb_o, *, use_rope):
    x = x_ref[...]
    h = _rms(x) * gmix_ref[...]
    h = h * (1.0 + sc_ref[...]) + sh_ref[...]
    p = _dot(h.astype(BF16), w_ref[...])

    seg = seg_ref[...]
    q = p[:, C_NAQ:C_NAQ + NA_WIDTH]
    k = p[:, C_NAK:C_NAK + NA_WIDTH]
    qss = _dot((q * q).astype(BF16), seg) * (1.0 / NA_DIM)
    kss = _dot((k * k).astype(BF16), seg) * (1.0 / NA_DIM)
    naq_o[...] = (q * lax.rsqrt(qss + EPS) * naqn_ref[...] * (NA_DIM ** -0.5)).astype(BF16)
    nak_o[...] = (k * lax.rsqrt(kss + EPS) * nakn_ref[...]).astype(BF16)
    nav_o[...] = p[:, C_NAV:C_NAV + NA_WIDTH].astype(BF16)

    small = p[:, C_SMALL:C_SMALL + V7X_LANES]
    cq = _rms(p[:, C_CQ:C_CQ + MLA_Q_RANK]) * qan_ref[...]
    mq = _head_rms_padded(_dot(cq.astype(BF16), wqb_ref[...]), MLA_QK) * mqn_ref[...]
    ckv = (_rms(p[:, C_CKV:C_CKV + MLA_KV_RANK]) * kvan_ref[...]).astype(BF16)
    lane = lax.broadcasted_iota(jnp.int32, small.shape, 1)
    kr = jnp.where((lane >= S_KR) & (lane < S_KR + MLA_ROPE), small, 0.0)
    mk = _dot(ckv, wkk_ref[...]) + jnp.concatenate([kr] * MLA_HEADS, axis=1)
    mk = _head_rms_padded(mk, MLA_QK) * mkn_ref[...]
    if use_rope:
        cos, sin = cos_ref[...], sin_ref[...]
        mq = _rope(mq, cos, sin)
        mk = _rope(mk, cos, sin)
    mq_o[...] = (mq * (MLA_QK ** -0.5)).astype(BF16)
    mk_o[...] = mk.astype(BF16)
    mv_o[...] = _dot(ckv, wkv_ref[...]).astype(BF16)

    gq_o[...] = p[:, C_GQ:C_GQ + GLA_KW] * (GLA_DK ** -0.5)
    gk_o[...] = p[:, C_GK:C_GK + GLA_KW]
    gv_o[...] = p[:, C_GV:C_GV + GLA_WIDTH]
    gout_o[...] = p[:, C_GOUT:C_GOUT + GLA_WIDTH]
    s_hi, s_lo = _split(small)
    wgh = wgh_ref[...]
    pre = _dot(s_hi, wgh) + _dot(s_lo, wgh) + _dot(s_hi, wgl_ref[...]) + bg_ref[...]
    ls = _log_sigmoid(pre) * (1.0 / GLA_GATE_NORM)
    gf_o[...] = ls[:, :GLA_KW]
    gb_o[...] = ls[:, GLA_KW:]


def _inproj(x, sc, sh, lw, cos, sin, use_rope):
    b, t, d = x.shape
    tm = min(512, t)
    full = lambda a: pl.BlockSpec(a.shape, lambda i, j: (0,) * a.ndim)
    row = lambda w: pl.BlockSpec((None, tm, w), lambda i, j: (i, j, 0))
    consts = [lw["gmix"], lw["w_in"]]
    tail = [lw["naqn"], lw["nakn"], lw["qan"], lw["wqb"], lw["kvan"], lw["wkk"], lw["wkv"], lw["mqn"], lw["mkn"],
            lw["wg_hi"], lw["wg_lo"], lw["bg"], lw["seg64"]]
    widths = [(NA_WIDTH, BF16)] * 3 + [(MLA_PAD, BF16), (MLA_PAD, BF16), (MLA_WIDTH, BF16),
                                       (GLA_KW, F32), (GLA_KW, F32), (GLA_WIDTH, F32), (GLA_WIDTH, F32),
                                       (GLA_KW, F32), (GLA_KW, F32)]
    return pl.pallas_call(
        functools.partial(_inproj_kernel, use_rope=use_rope),
        grid=(b, t // tm),
        in_specs=[row(d),
                  pl.BlockSpec((None, 1, d), lambda i, j: (i, 0, 0)),
                  pl.BlockSpec((None, 1, d), lambda i, j: (i, 0, 0))]
                 + [full(a) for a in consts]
                 + [pl.BlockSpec((tm, V7X_LANES), lambda i, j: (j, 0))] * 2
                 + [full(a) for a in tail],
        out_specs=[row(w) for w, _ in widths],
        out_shape=[jax.ShapeDtypeStruct((b, t, w), dt) for w, dt in widths],
        compiler_params=_params(("parallel", "parallel")),
        name="inproj_rope" if use_rope else "inproj_ctx",
    )(x, sc, sh, *consts, cos, sin, *tail)


def _stack_heads(q, width):
    lane = lax.broadcasted_iota(jnp.int32, q.shape, 1)
    zero = jnp.zeros_like(q)
    return jnp.concatenate([jnp.where(lane // width == h, q, zero) for h in range(NA_HEADS)], axis=0)


def _unstack_heads(o4, n, width):
    lane = lax.broadcasted_iota(jnp.int32, (n, o4.shape[1]), 1)
    out = jnp.zeros((n, o4.shape[1]), F32)
    for h in range(NA_HEADS):
        out = out + jnp.where(lane // width == h, o4[h * n:(h + 1) * n], 0.0)
    return out


def _na_kernel(q_ref, k_ref, v_ref, kc_ref, vc_ref, bias_ref, o_ref, *, rows):
    kc = kc_ref[...]
    vc = vc_ref[...]
    n_loc = NA_KH * GRID_W

    def body(r, carry):
        rs = jnp.clip(r - NA_KH // 2, 0, rows - NA_KH)
        off = r - rs
        q = q_ref[pl.ds(pl.multiple_of(r * GRID_W, GRID_W), GRID_W), :]
        ks = k_ref[pl.ds(pl.multiple_of(rs * GRID_W, GRID_W), n_loc), :]
        vs = v_ref[pl.ds(pl.multiple_of(rs * GRID_W, GRID_W), n_loc), :]
        q4 = _stack_heads(q, NA_DIM)
        s_loc = _dot_nt(q4, ks) + bias_ref[off]
        s_ctx = _dot_nt(q4, kc)
        m = jnp.maximum(jnp.max(s_loc, axis=-1, keepdims=True), jnp.max(s_ctx, axis=-1, keepdims=True))
        p_loc = jnp.exp(s_loc - m)
        p_ctx = jnp.exp(s_ctx - m)
        denom = jnp.sum(p_loc, axis=-1, keepdims=True) + jnp.sum(p_ctx, axis=-1, keepdims=True)
        o4 = (_dot(p_loc.astype(BF16), vs) + _dot(p_ctx.astype(BF16), vc)) * (1.0 / denom)
        o_ref[pl.ds(pl.multiple_of(r * GRID_W, GRID_W), GRID_W), :] = _unstack_heads(o4, GRID_W, NA_DIM).astype(BF16)
        return carry

    lax.fori_loop(0, rows, body, 0)


def _na_bias_table(rpb, rows):
    kh = NA_KH
    r_all = np.arange(rows)
    offs = r_all - np.clip(r_all - kh // 2, 0, rows - kh)
    n_off = int(offs.max()) + 1
    qc = np.arange(GRID_W)
    kcol = np.arange(GRID_W)
    cs = np.clip(qc - NA_KW // 2, 0, GRID_W - NA_KW)
    in_win = (kcol[None, :] >= cs[:, None]) & (kcol[None, :] < cs[:, None] + NA_KW)
    dc = np.clip(kcol[None, :] - qc[:, None] + NA_KW - 1, 0, 2 * NA_KW - 2)
    onehot = (dc[None] == np.arange(2 * NA_KW - 1)[:, None, None]).astype(np.float32)
    toep = jnp.einsum("hdc,cqk->hdqk", rpb.astype(F32), jnp.asarray(onehot), precision=lax.Precision.HIGHEST)
    toep = jnp.where(in_win[None, None], toep, -jnp.inf)
    per_off = [toep[:, NA_KH - 1 - off:2 * NA_KH - 1 - off] for off in range(n_off)]
    bias = jnp.stack(per_off, axis=0)
    return bias.transpose(0, 1, 3, 2, 4).reshape(n_off, NA_HEADS * GRID_W, kh * GRID_W)


def _na_attention(q, k, v, kc, vc, bias):
    b, t, w = q.shape
    tc = kc.shape[1]
    rows = t // GRID_W
    assert rows >= NA_KH and t % GRID_W == 0
    seq = lambda n: pl.BlockSpec((None, n, w), lambda i: (i, 0, 0))
    return pl.pallas_call(
        functools.partial(_na_kernel, rows=rows),
        grid=(b,),
        in_specs=[seq(t), seq(t), seq(t), seq(tc), seq(tc),
                  pl.BlockSpec(bias.shape, lambda i: (0, 0, 0))],
        out_specs=seq(t),
        out_shape=jax.ShapeDtypeStruct((b, t, w), BF16),
        compiler_params=_params(("parallel",)),
        name="na_attention",
    )(q, k, v, kc, vc, bias)


def _softmax_pv(scores, values):
    m = functools.reduce(jnp.maximum, [jnp.max(s, axis=-1, keepdims=True) for s in scores])
    ps = [jnp.exp(s - m) for s in scores]
    denom = functools.reduce(jnp.add, [jnp.sum(p, axis=-1, keepdims=True) for p in ps])
    o = functools.reduce(jnp.add, [_dot(p.astype(BF16), v) for p, v in zip(ps, values)])
    return o * (1.0 / denom)


def _attn_sliced_kernel(*refs, n_parts, heads, dh, dv):
    q_ref, o_ref = refs[0], refs[-1]
    for h in range(heads):
        q = q_ref[:, dh * h:dh * (h + 1)]
        scores = [_dot_nt(q, refs[1 + 2 * p][:, dh * h:dh * (h + 1)]) for p in range(n_parts)]
        values = [refs[2 + 2 * p][:, dv * h:dv * (h + 1)] for p in range(n_parts)]
        o_ref[:, dv * h:dv * (h + 1)] = _softmax_pv(scores, values).astype(o_ref.dtype)


def _attn_masked_kernel(q_ref, k_ref, v_ref, o_ref, *, width):
    n = q_ref.shape[0]
    q4 = _stack_heads(q_ref[...], width)
    o4 = _softmax_pv([_dot_nt(q4, k_ref[...])], [v_ref[...]])
    o_ref[...] = _unstack_heads(o4, n, width).astype(o_ref.dtype)


def _attention(q, parts, heads, dh, dv, tq):
    b, t, wq = q.shape
    tq = min(tq, t)
    flat = [a for kv in parts for a in kv]
    kv_spec = lambda a: pl.BlockSpec((None,) + a.shape[1:], lambda i, j: (i, 0, 0))
    return pl.pallas_call(
        functools.partial(_attn_sliced_kernel, n_parts=len(parts), heads=heads, dh=dh, dv=dv),
        grid=(b, t // tq),
        in_specs=[pl.BlockSpec((None, tq, wq), lambda i, j: (i, j, 0))] + [kv_spec(a) for a in flat],
        out_specs=pl.BlockSpec((None, tq, heads * dv), lambda i, j: (i, j, 0)),
        out_shape=jax.ShapeDtypeStruct((b, t, heads * dv), BF16),
        compiler_params=_params(("parallel", "parallel")),
        name="mla_attention_%d" % len(parts),
    )(q, *flat)


def _attention_masked(q, k, v, width):
    b, t, w = q.shape
    spec = lambda a: pl.BlockSpec((None,) + a.shape[1:], lambda i: (i, 0, 0))
    return pl.pallas_call(
        functools.partial(_attn_masked_kernel, width=width),
        grid=(b,),
        in_specs=[spec(q), spec(k), spec(v)],
        out_specs=spec(q),
        out_shape=jax.ShapeDtypeStruct((b, t, w), BF16),
        compiler_params=_params(("parallel",)),
        name="ctx_na_attention",
    )(q, k, v)


GLA_SAFE_DECAY = 60.0
GLA_EXACT_CHUNK = 16


def _gla_masks(c):
    ri = lax.broadcasted_iota(jnp.int32, (c, c), 0)
    ci = lax.broadcasted_iota(jnp.int32, (c, c), 1)
    ti = lax.broadcasted_iota(jnp.int32, (c, GLA_HEADS * c), 0)
    si = lax.broadcasted_iota(jnp.int32, (c, GLA_HEADS * c), 1) % c
    lk = lax.broadcasted_iota(jnp.int32, (c, GLA_KW), 1) // GLA_DK
    lv = lax.broadcasted_iota(jnp.int32, (c, GLA_WIDTH), 1) // GLA_DV
    head_v = lax.broadcasted_iota(jnp.int32, (GLA_WIDTH, GLA_KW), 0) // GLA_DV
    head_k = lax.broadcasted_iota(jnp.int32, (GLA_WIDTH, GLA_KW), 1) // GLA_DK
    return {
        "tri": ((ci <= ri).astype(BF16), (ci >= ri).astype(BF16)),
        "allowed": (si <= ti, si >= ti),
        "hm_k": [lk == h for h in range(GLA_HEADS)],
        "hm_v": [lv == h for h in range(GLA_HEADS)],
        "hm_s": head_v == head_k,
        "expand": (lax.broadcasted_iota(jnp.int32, (GLA_KW, GLA_WIDTH), 0) // GLA_DK
                   == lax.broadcasted_iota(jnp.int32, (GLA_KW, GLA_WIDTH), 1) // GLA_DV).astype(BF16),
    }


def _gla_intra_pairwise(q, k, v, bcum, expand, reverse):
    c = q.shape[0]
    row = lax.broadcasted_iota(jnp.int32, (c, GLA_WIDTH), 0)
    outs = []
    for t in range(c):
        w = jnp.exp(jnp.minimum(bcum[t:t + 1, :] - bcum, 0.0))
        a = _dot(((q[t:t + 1, :] * k) * w).astype(BF16), expand)
        keep = (row >= t) if reverse else (row <= t)
        outs.append(jnp.sum(jnp.where(keep, a * v, 0.0), axis=0, keepdims=True))
    return jnp.concatenate(outs, axis=0)


def _gla_chunk(q, k, v, g, st_ref, m, reverse, pairwise):
    c = q.shape[0]
    d = 1 if reverse else 0
    end_row = 0 if reverse else c - 1
    g_hi, g_lo = _split(g)
    tri = m["tri"][d]
    bcum = _dot(tri, g_hi) + _dot(tri, g_lo)
    bend = bcum[end_row:end_row + 1, :]
    qe = (q * jnp.exp(bcum)).astype(BF16)
    ke = (k * jnp.exp(bend - bcum)).astype(BF16)
    st = st_ref[...]
    o = _dot_nt(qe, st.astype(BF16))
    if pairwise:
        o = o + _gla_intra_pairwise(q, k, v, bcum, m["expand"], reverse)
    else:
        kt = k * jnp.exp(-bcum)
        kst = jnp.concatenate([jnp.where(hm, kt, 0.0) for hm in m["hm_k"]], axis=0).astype(BF16)
        a = jnp.where(m["allowed"][d], _dot_nt(qe, kst), 0.0)
        vbd = jnp.concatenate([jnp.where(hm, v, 0.0) for hm in m["hm_v"]], axis=0).astype(BF16)
        o = o + _dot(a.astype(BF16), vbd)
    upd = _dot_tn(v.astype(BF16), ke)
    st_ref[...] = st * jnp.exp(bend) + jnp.where(m["hm_s"], upd, 0.0)
    return o


def _gla_kernel(q_ref, k_ref, v_ref, gf_ref, gb_ref, go_ref,
                qc_ref, kc_ref, vc_ref, gfc_ref, gbc_ref, goc_ref, onorm_ref, seg_ref,
                *rest, ctx_out):
    if ctx_out:
        ox_ref, oc_ref, acc_ref, accc_ref, sf_ref, sb_ref = rest
    else:
        ox_ref, acc_ref, sf_ref, sb_ref = rest
        oc_ref = accc_ref = None
    t, tc = q_ref.shape[0], qc_ref.shape[0]

    def sweep(qr, kr, vr, gfr, gbr, dst, total, c, m, pairwise):
        count = total // c

        def body(j, carry):
            lo_f = pl.multiple_of(j * c, c)
            lo_b = pl.multiple_of((count - 1 - j) * c, c)
            of = _gla_chunk(qr[pl.ds(lo_f, c), :], kr[pl.ds(lo_f, c), :], vr[pl.ds(lo_f, c), :],
                            gfr[pl.ds(lo_f, c), :], sf_ref, m, False, pairwise)
            ob = _gla_chunk(qr[pl.ds(lo_b, c), :], kr[pl.ds(lo_b, c), :], vr[pl.ds(lo_b, c), :],
                            gbr[pl.ds(lo_b, c), :], sb_ref, m, True, pairwise)
            if dst is not None:
                dst[pl.ds(lo_f, c), :] += of
                dst[pl.ds(lo_b, c), :] += ob
            return carry
        lax.fori_loop(0, count, body, 0)

    def scans(c, pairwise):
        m = _gla_masks(c)
        sf_ref[...] = jnp.zeros_like(sf_ref)
        sb_ref[...] = jnp.zeros_like(sb_ref)
        acc_ref[...] = jnp.zeros_like(acc_ref)
        if ctx_out:
            accc_ref[...] = jnp.zeros_like(accc_ref)
        sweep(qc_ref, kc_ref, vc_ref, gfc_ref, gbc_ref, accc_ref, tc, c, m, pairwise)
        sweep(q_ref, k_ref, v_ref, gf_ref, gb_ref, acc_ref, t, c, m, pairwise)

    def min_chunk_decay(gr, total):
        def body(j, low):
            lo = pl.multiple_of(j * GLA_CHUNK, GLA_CHUNK)
            return jnp.minimum(low, jnp.sum(gr[pl.ds(lo, GLA_CHUNK), :], axis=0, keepdims=True))
        return lax.fori_loop(0, total // GLA_CHUNK, body, jnp.zeros((1, GLA_KW), F32))

    low = functools.reduce(jnp.minimum, [min_chunk_decay(gf_ref, t), min_chunk_decay(gb_ref, t),
                                         min_chunk_decay(gfc_ref, tc), min_chunk_decay(gbc_ref, tc)])
    safe = jnp.min(low) >= -GLA_SAFE_DECAY
    pl.when(safe)(functools.partial(scans, GLA_CHUNK, False))
    pl.when(jnp.logical_not(safe))(functools.partial(scans, GLA_EXACT_CHUNK, True))

    def finish(acc, gate, out, total):
        tr = min(256, total)

        def body(i, carry):
            lo = pl.multiple_of(i * tr, tr)
            o = acc[pl.ds(lo, tr), :]
            ms = _dot((o * o).astype(BF16), seg_ref[...]) * (1.0 / GLA_DV)
            gt = gate[pl.ds(lo, tr), :]
            y = o * lax.rsqrt(ms + EPS) * onorm_ref[...] * (gt * jax.nn.sigmoid(gt))
            out[pl.ds(lo, tr), :] = y.astype(out.dtype)
            return carry
        lax.fori_loop(0, total // tr, body, 0)

    finish(acc_ref, go_ref, ox_ref, t)
    if ctx_out:
        finish(accc_ref, goc_ref, oc_ref, tc)


def _gla(lat, ctx, onorm, seg64, ctx_out):
    b, t, _ = lat[0].shape
    tc = ctx[0].shape[1]
    spec = lambda a: pl.BlockSpec((None,) + a.shape[1:], lambda i: (i, 0, 0))
    full = lambda a: pl.BlockSpec(a.shape, lambda i: (0,) * a.ndim)
    out_shape = [jax.ShapeDtypeStruct((b, t, GLA_WIDTH), BF16)]
    scratch = [pltpu.VMEM((t, GLA_WIDTH), F32)]
    if ctx_out:
        out_shape.append(jax.ShapeDtypeStruct((b, tc, GLA_WIDTH), BF16))
        scratch.append(pltpu.VMEM((tc, GLA_WIDTH), F32))
    scratch += [pltpu.VMEM((GLA_WIDTH, GLA_KW), F32)] * 2
    outs = pl.pallas_call(
        functools.partial(_gla_kernel, ctx_out=ctx_out),
        grid=(b,),
        in_specs=[spec(a) for a in lat] + [spec(a) for a in ctx] + [full(onorm), full(seg64)],
        out_specs=[spec(s) for s in out_shape],
        out_shape=out_shape,
        scratch_shapes=scratch,
        compiler_params=_params(("parallel",)),
        name="gla_ctx_out" if ctx_out else "gla_last",
    )(*lat, *ctx, onorm, seg64)
    return outs if ctx_out else (outs[0], None)


def _outproj_kernel(a_ref, b_ref, c_ref, x_ref, gt_ref, sc_ref, sh_ref, gffn_ref,
                    wa_ref, wb_ref, wc_ref, wrh_ref, wrl_ref, br_ref, xo_ref, h_ref, lg_ref):
    mix = _dot(a_ref[...], wa_ref[...]) + _dot(b_ref[...], wb_ref[...]) + _dot(c_ref[...], wc_ref[...])
    xn = x_ref[...] + gt_ref[...] * mix
    xo_ref[...] = xn
    h = _rms(xn) * gffn_ref[...]
    h = h * (1.0 + sc_ref[...]) + sh_ref[...]
    _store_row_tiled(h_ref, h)
    h_hi, h_lo = _split(h)
    wrh = wrh_ref[...]
    lg_ref[...] = _dot(h_hi, wrh) + _dot(h_lo, wrh) + _dot(h_hi, wrl_ref[...]) + br_ref[...]


def _outproj(a, bm, c, x, gt, sc, sh, lw):
    b, t, d = x.shape
    tm = min(512, t)
    row = lambda w: pl.BlockSpec((None, tm, w), lambda i, j: (i, j, 0))
    vec = pl.BlockSpec((None, 1, d), lambda i, j: (i, 0, 0))
    full = lambda arr: pl.BlockSpec(arr.shape, lambda i, j: (0,) * arr.ndim)
    consts = [lw["gffn"], lw["wo_a"], lw["wo_b"], lw["wo_c"], lw["wr_hi"], lw["wr_lo"], lw["br"]]
    return pl.pallas_call(
        _outproj_kernel,
        grid=(b, t // tm),
        in_specs=[row(NA_WIDTH), row(MLA_WIDTH), row(GLA_WIDTH), row(d), vec, vec, vec] + [full(w) for w in consts],
        out_specs=[row(d), pl.BlockSpec((None, tm * V7X_SUBLANES, V7X_LANES), lambda i, j: (i, j, 0)), row(V7X_LANES)],
        out_shape=[jax.ShapeDtypeStruct((b, t, d), F32),
                   jax.ShapeDtypeStruct((b, t * V7X_SUBLANES, V7X_LANES), F32),
                   jax.ShapeDtypeStruct((b, t, V7X_LANES), F32)],
        compiler_params=_params(("parallel", "parallel")),
        name="outproj_router",
    )(a, bm, c, x, gt, sc, sh, *consts)


def _moe_kernel(be_ref, tok_hbm, pair_hbm, h_hbm, w1_ref, b1_ref, w2_ref, b2_ref, y_hbm,
                g0, g1, s0, s1, x0, x1, y0, y1, w1b_ref, w2b_ref, gisem, sisem, gsem, ssem):
    i = pl.program_id(0)
    last = pl.num_programs(0) - 1
    sub = V7X_SUBLANES
    blk = x0.shape[0] // sub
    f = w2_ref.shape[0]
    gbuf, sbuf, xbuf, ybuf = (g0, g1), (s0, s1), (x0, x1), (y0, y1)

    def load_tok(t, s):
        return pltpu.make_async_copy(tok_hbm.at[pl.ds(pl.multiple_of(t * blk, blk), blk)], gbuf[s], gisem.at[s])

    def load_pair(t, s):
        return pltpu.make_async_copy(pair_hbm.at[pl.ds(pl.multiple_of(t * blk, blk), blk)], sbuf[s], sisem.at[s])

    def tile(ref, first_row):
        if not isinstance(first_row, int):
            first_row = pl.multiple_of(first_row, sub)
        return ref.at[pl.ds(first_row, sub)]

    def start_gathers(s):
        for r in range(blk):
            pltpu.make_async_copy(tile(h_hbm, gbuf[s][r]), tile(xbuf[s], r * sub), gsem.at[s]).start(
                priority=r % DMA_QUEUES)

    def wait_gathers(s):
        for r in range(blk):
            pltpu.make_async_copy(tile(h_hbm, 0), tile(xbuf[s], r * sub), gsem.at[s]).wait()

    def start_scatters(s):
        for r in range(blk):
            pltpu.make_async_copy(tile(ybuf[s], r * sub), tile(y_hbm, sbuf[s][r]), ssem.at[s]).start(
                priority=r % DMA_QUEUES)

    def wait_scatters(s):
        for r in range(blk):
            pltpu.make_async_copy(tile(ybuf[s], r * sub), tile(y_hbm, 0), ssem.at[s]).wait()

    @pl.when(i == 0)
    def _():
        y0[...] = jnp.zeros_like(y0)
        y1[...] = jnp.zeros_like(y1)
        first_spare = y_hbm.shape[0] - 2 * blk * sub
        for r in range(blk):
            pltpu.make_async_copy(tile(y0, r * sub), tile(y_hbm, first_spare + r * sub), ssem.at[0]).start()
        load_tok(1, 0).start()
        load_tok(2, 1).start()
        load_pair(0, 1).start()
        load_tok(1, 0).wait()
        start_gathers(0)

    e = be_ref[i]
    prev = be_ref[jnp.maximum(i - 1, 0)]

    @pl.when((i == 0) | (e != prev))
    def _():
        rows = 128

        def cast1(j, carry):
            lo = pl.multiple_of(j * rows, rows)
            w1b_ref[pl.ds(lo, rows), :] = w1_ref[pl.ds(lo, rows), :].astype(BF16)
            return carry

        def cast2(j, carry):
            lo = pl.multiple_of(j * rows, rows)
            w2b_ref[pl.ds(lo, rows), :] = w2_ref[pl.ds(lo, rows), :].astype(BF16)
            return carry
        lax.fori_loop(0, w1_ref.shape[0] // rows, cast1, 0)
        lax.fori_loop(0, w2_ref.shape[0] // rows, cast2, 0)

    def step(p):
        q = 1 - p

        load_tok(i + 2, q).wait()
        load_pair(i, q).wait()
        load_tok(i + 3, p).start()
        load_pair(i + 1, p).start()
        wait_gathers(p)
        wait_scatters(p)
        start_gathers(q)
        start_scatters(q)

        hb = _dot(_load_row_tiled(xbuf[p]).astype(BF16), w1b_ref[...]) + b1_ref[...]
        glu = jnp.minimum(hb[:, :f], SWIGLU_LIMIT)
        lin = jnp.clip(hb[:, f:], -SWIGLU_LIMIT, SWIGLU_LIMIT)
        act = (lin + 1.0) * (glu * jax.nn.sigmoid(SWIGLU_ALPHA * glu))
        _store_row_tiled(ybuf[p], _dot(act.astype(BF16), w2b_ref[...]) + b2_ref[...])

        @pl.when(i == last)
        def _():
            load_tok(i + 3, p).wait()
            load_pair(i + 1, p).wait()
            wait_gathers(q)
            wait_scatters(q)
            start_scatters(p)
            wait_scatters(p)

    for p in range(2):
        pl.when(i % 2 == p)(functools.partial(step, p))


def _moe_blocks(h, tok_tab, pair_tab, blk_e, n_rows_out, w1, b1, w2, b2, layer):
    d = V7X_SUBLANES * V7X_LANES
    n_blk = blk_e.shape[0]
    f2 = w1.shape[-1]
    f = w2.shape[-2]
    grid_spec = pltpu.PrefetchScalarGridSpec(
        num_scalar_prefetch=1,
        grid=(n_blk,),
        in_specs=[
            pl.BlockSpec(memory_space=pl.ANY),
            pl.BlockSpec(memory_space=pl.ANY),
            pl.BlockSpec(memory_space=pl.ANY),
            pl.BlockSpec((None, None, d, f2), lambda i, be: (layer, be[i], 0, 0)),
            pl.BlockSpec((None, None, 1, f2), lambda i, be: (layer, be[i], 0, 0)),
            pl.BlockSpec((None, None, f, d), lambda i, be: (layer, be[i], 0, 0)),
            pl.BlockSpec((None, None, 1, d), lambda i, be: (layer, be[i], 0, 0)),
        ],
        out_specs=pl.BlockSpec(memory_space=pl.ANY),
        scratch_shapes=[pltpu.SMEM((MOE_BLOCK,), jnp.int32)] * 4
        + [pltpu.VMEM((MOE_BLOCK * V7X_SUBLANES, V7X_LANES), F32)] * 4 + [
            pltpu.VMEM((d, f2), BF16),
            pltpu.VMEM((f, d), BF16),
        ] + [pltpu.SemaphoreType.DMA((2,))] * 4,
    )
    return pl.pallas_call(
        _moe_kernel,
        grid_spec=grid_spec,
        out_shape=jax.ShapeDtypeStruct((n_rows_out * V7X_SUBLANES, V7X_LANES), F32),
        compiler_params=_params(("arbitrary",), mib=56),
        name="moe_experts",
    )(blk_e, tok_tab, pair_tab, h, w1, b1.reshape(b1.shape[0], b1.shape[1], 1, f2), w2,
      b2.reshape(b2.shape[0], b2.shape[1], 1, d))


def _moe(h, logits, w1, b1, w2, b2, layer):
    n = logits.shape[0]
    nk = n * TOP_K
    top_val, top_idx = lax.top_k(logits, TOP_K)
    gates = jax.nn.softmax(top_val, axis=-1)
    flat_e = top_idx.reshape(-1)
    order = jnp.argsort(flat_e).astype(jnp.int32)
    counts = jnp.bincount(flat_e, length=N_EXPERTS).astype(jnp.int32)
    padded = (counts + MOE_BLOCK - 1) // MOE_BLOCK * MOE_BLOCK
    start = jnp.cumsum(counts) - counts
    pend = jnp.cumsum(padded)
    pstart = pend - padded
    n_blk = -(-nk // MOE_BLOCK) + N_EXPERTS
    blk_id = jnp.arange(-1, n_blk + 2, dtype=jnp.int32)
    blk_lo = blk_id * MOE_BLOCK
    blk_e = jnp.minimum(jnp.sum((pend[None, :] <= blk_lo[:, None]).astype(jnp.int32), axis=1), N_EXPERTS - 1)
    s0 = start[blk_e] + blk_lo - pstart[blk_e]
    left = jnp.where(blk_id >= 0, start[blk_e] + counts[blk_e] - s0, 0)
    r = jnp.arange(MOE_BLOCK, dtype=jnp.int32)[None, :]
    valid = r < left[:, None]
    win = order[jnp.clip(s0[:, None] + r, 0, nk - 1)]
    tok = lax.shift_right_logical(win, 2)
    dst = (win & (TOP_K - 1)) * n + tok
    dst_tab = jnp.where(valid, dst, nk + (blk_id[:, None] % 2) * MOE_BLOCK + r).reshape(-1)
    tok_tab = jnp.where(valid, tok, 0).reshape(-1)
    y4 = _moe_blocks(h, tok_tab * V7X_SUBLANES, dst_tab * V7X_SUBLANES, blk_e[1:n_blk + 1], nk + 2 * MOE_BLOCK,
                     w1, b1, w2, b2, layer)
    return y4, gates


def _combine_kernel(x_ref, gt_ref, g_ref, y0_ref, y1_ref, y2_ref, y3_ref, o_ref):
    g = g_ref[...]
    mix = g[:, 0:1] * _load_row_tiled(y0_ref)
    for k, y_ref in enumerate((y1_ref, y2_ref, y3_ref), start=1):
        mix = mix + g[:, k:k + 1] * _load_row_tiled(y_ref)
    o_ref[...] = x_ref[...] + gt_ref[...] * mix


def _combine(x, gt, gates, y4, n_tok, first_tok):
    b, t, d = x.shape
    tm = int(min(512, np.gcd.reduce([t, n_tok, first_tok or n_tok])))
    per_b = t // tm
    plane = n_tok // tm
    first = first_tok // tm
    row = pl.BlockSpec((None, tm, d), lambda i, j: (i, j, 0))
    y_spec = lambda k: pl.BlockSpec((tm * V7X_SUBLANES, V7X_LANES),
                                    lambda i, j: (k * plane + first + i * per_b + j, 0))
    return pl.pallas_call(
        _combine_kernel,
        grid=(b, per_b),
        in_specs=[row, pl.BlockSpec((None, 1, d), lambda i, j: (i, 0, 0)),
                  pl.BlockSpec((None, tm, TOP_K), lambda i, j: (i, j, 0))] + [y_spec(k) for k in range(TOP_K)],
        out_specs=row,
        out_shape=jax.ShapeDtypeStruct((b, t, d), F32),
        compiler_params=_params(("parallel", "parallel")),
        name="moe_combine",
    )(x, gt, gates, y4, y4, y4, y4)


def _rope_tables(t):
    tok = jnp.arange(t)
    row = (tok // GRID_W).astype(F32)[:, None]
    col = (tok % GRID_W).astype(F32)[:, None]
    inv = 1.0 / (ROPE_THETA ** (jnp.arange(0, ROPE_AXIS, 2, dtype=F32) / ROPE_AXIS))
    ang = jnp.concatenate([row * inv, row * inv, col * inv, col * inv], axis=-1)
    cos = jnp.ones((t, V7X_LANES), F32).at[:, MLA_NOPE:MLA_QK].set(jnp.cos(ang))
    sin = jnp.zeros((t, V7X_LANES), F32).at[:, MLA_NOPE:MLA_QK].set(jnp.sin(ang))
    return cos, sin


def _pad_heads(w, per_head, offset=0):
    lead = w.shape[:-1]
    w = w.reshape(lead + (MLA_HEADS, per_head))
    out = jnp.zeros(lead + (MLA_HEADS, V7X_LANES), w.dtype).at[..., offset:offset + per_head].set(w)
    return out.reshape(lead + (MLA_PAD,))


def _layer_weights(l, g_mix, w_in, na_q_norm, na_k_norm, mla_q_a_norm, mla_w_q_b, mla_kv_a_norm, mla_w_kv_b,
                   mla_q_norm, mla_k_norm, gla_w_gate, gla_b_gate, gla_o_norm, w_out, g_ffn, w_router, b_router):
    d = w_in.shape[1]
    sizes = (NA_WIDTH, NA_WIDTH, NA_WIDTH, MLA_Q_RANK, MLA_KV_RANK, MLA_ROPE, GLA_KW, GLA_KW, GLA_WIDTH, GLA_WIDTH,
             GLA_GATE_RANK, GLA_GATE_RANK)
    (wq, wk, wv, wcq, wckv, wkr, wgq, wgk, wgv, wgo, wlf, wlb) = jnp.split(w_in[l], np.cumsum(sizes)[:-1].tolist(), axis=-1)
    small = jnp.zeros((d, V7X_LANES), F32)
    small = small.at[:, S_LF:S_LF + GLA_GATE_RANK].set(wlf).at[:, S_LB:S_LB + GLA_GATE_RANK].set(wlb)
    small = small.at[:, S_KR:S_KR + MLA_ROPE].set(wkr)
    w_packed = jnp.concatenate([wq, wk, wv, wcq, wckv, wgq, wgk, wgv, wgo, small], axis=-1).astype(BF16)
    kvb = mla_w_kv_b[l].reshape(MLA_KV_RANK, MLA_HEADS, MLA_NOPE + MLA_V)
    wg = jnp.zeros((V7X_LANES, 2 * GLA_KW), F32)
    wg = wg.at[S_LF:S_LF + GLA_GATE_RANK, :GLA_KW].set(gla_w_gate[l, 0])
    wg = wg.at[S_LB:S_LB + GLA_GATE_RANK, GLA_KW:].set(gla_w_gate[l, 1])
    wg_hi, wg_lo = _split(wg)
    seg = (np.arange(NA_WIDTH)[:, None] // NA_DIM == np.arange(NA_WIDTH)[None, :] // NA_DIM)
    wr = jnp.zeros((d, V7X_LANES), F32).at[:, :N_EXPERTS].set(w_router[l])
    wr_hi, wr_lo = _split(wr)
    wo = w_out[l].astype(BF16)
    return {
        "gmix": g_mix[l][None, :],
        "w_in": w_packed,
        "naqn": jnp.tile(na_q_norm[l], NA_HEADS)[None, :],
        "nakn": jnp.tile(na_k_norm[l], NA_HEADS)[None, :],
        "qan": mla_q_a_norm[l][None, :],
        "wqb": _pad_heads(mla_w_q_b[l], MLA_QK).astype(BF16),
        "kvan": mla_kv_a_norm[l][None, :],
        "wkk": _pad_heads(kvb[:, :, :MLA_NOPE].reshape(MLA_KV_RANK, -1), MLA_NOPE).astype(BF16),
        "wkv": kvb[:, :, MLA_NOPE:].reshape(MLA_KV_RANK, MLA_WIDTH).astype(BF16),
        "mqn": _pad_heads(jnp.tile(mla_q_norm[l], MLA_HEADS), MLA_QK)[None, :],
        "mkn": _pad_heads(jnp.tile(mla_k_norm[l], MLA_HEADS), MLA_QK)[None, :],
        "wg_hi": wg_hi, "wg_lo": wg_lo,
        "bg": jnp.concatenate([gla_b_gate[l, 0], gla_b_gate[l, 1]])[None, :],
        "seg64": jnp.asarray(seg, BF16),
        "onorm": jnp.tile(gla_o_norm[l], GLA_HEADS)[None, :],
        "gffn": g_ffn[l][None, :],
        "wo_a": wo[:NA_WIDTH], "wo_b": wo[NA_WIDTH:NA_WIDTH + MLA_WIDTH], "wo_c": wo[NA_WIDTH + MLA_WIDTH:],
        "wr_hi": wr_hi, "wr_lo": wr_lo,
        "br": jnp.zeros((1, V7X_LANES), F32).at[0, :N_EXPERTS].set(b_router[l]),
    }


def kernel(x, c, ctx, c_ctx, w_ada, b_ada, g_mix, w_in, na_q_norm, na_k_norm, na_rpb, mla_q_a_norm, mla_w_q_b,
           mla_kv_a_norm, mla_w_kv_b, mla_q_norm, mla_k_norm, gla_w_gate, gla_b_gate, gla_o_norm, w_out, g_ffn,
           w_router, b_router, w_moe1, b_moe1, w_moe2, b_moe2):
    bsz, t, d = x.shape
    assert d == V7X_SUBLANES * V7X_LANES
    tc = ctx.shape[1]
    depth = w_ada.shape[0]
    rows = t // GRID_W
    cos, sin = _rope_tables(t)

    pad = (-(bsz + 1)) % 8
    cvec = jnp.concatenate([c, c_ctx[None, :], jnp.zeros((pad, d), F32)], axis=0)
    mod = _ada(cvec, w_ada, b_ada)

    h_ctx = ctx
    for l in range(depth):
        ctx_out = l < depth - 1
        lw = _layer_weights(l, g_mix, w_in, na_q_norm, na_k_norm, mla_q_a_norm, mla_w_q_b, mla_kv_a_norm,
                            mla_w_kv_b, mla_q_norm, mla_k_norm, gla_w_gate, gla_b_gate, gla_o_norm, w_out, g_ffn,
                            w_router, b_router)
        m_lat = [m[:, None, :] for m in jnp.split(mod[l, :bsz], 6, axis=-1)]
        m_ctx = [jnp.broadcast_to(m[None, :, :], (bsz, 1, d)) for m in jnp.split(mod[l, bsz:bsz + 1], 6, axis=-1)]
        sh1, sc1, gt1, sh2, sc2, gt2 = m_lat
        csh1, csc1, cgt1, csh2, csc2, cgt2 = m_ctx

        (naq, nak, nav, mq, mk, mv, gq, gk, gv, gout, gf, gb) = _inproj(x, sc1, sh1, lw, cos, sin, True)
        (cnaq, cnak, cnav, cmq, cmk, cmv, cgq, cgk, cgv, cgout, cgf, cgb) = _inproj(
            h_ctx, csc1, csh1, lw, cos[:tc], sin[:tc], False)

        bias = _na_bias_table(na_rpb[l], rows)
        a_x = _na_attention(naq, nak, nav, cnak, cnav, bias)
        b_x = _attention(mq, [(mk, mv), (cmk, cmv)], MLA_HEADS, V7X_LANES, MLA_V, 256)
        c_x, c_c = _gla((gq, gk, gv, gf, gb, gout), (cgq, cgk, cgv, cgf, cgb, cgout), lw["onorm"], lw["seg64"], ctx_out)
        x_new, h2, logits = _outproj(a_x, b_x, c_x, x, gt1, sc2, sh2, lw)

        toks = [h2.reshape(bsz * t * V7X_SUBLANES, V7X_LANES)]
        lgs = [logits.reshape(bsz * t, V7X_LANES)]
        if ctx_out:
            a_c = _attention_masked(cnaq, cnak, cnav, NA_DIM)
            b_c = _attention(cmq, [(cmk, cmv)], MLA_HEADS, V7X_LANES, MLA_V, 256)
            hc_new, hc2, clogits = _outproj(a_c, b_c, c_c, h_ctx, cgt1, csc2, csh2, lw)
            toks.append(hc2.reshape(bsz * tc * V7X_SUBLANES, V7X_LANES))
            lgs.append(clogits.reshape(bsz * tc, V7X_LANES))
        n_tok = sum(a.shape[0] for a in lgs)
        y4, gates = _moe(jnp.concatenate(toks, axis=0), jnp.concatenate(lgs, axis=0)[:, :N_EXPERTS],
                         w_moe1, b_moe1, w_moe2, b_moe2, l)
        x = _combine(x_new, gt2, gates[:bsz * t].reshape(bsz, t, TOP_K), y4, n_tok, 0)
        if ctx_out:
            h_ctx = _combine(hc_new, cgt2, gates[bsz * t:].reshape(bsz, tc, TOP_K), y4, n_tok, bsz * t)
    return x
```

```python
import functools

import numpy as np
import jax
import jax.numpy as jnp
from jax import lax
from jax.experimental import pallas as pl
from jax.experimental.pallas import tpu as pltpu

F32 = jnp.float32
BF16 = jnp.bfloat16

V7X_LANES = 128
V7X_VMEM_BYTES = 64 * 1024 * 1024
DMA_QUEUES = 2

EPS = 1e-6
GRID_W = 64
NA_HEADS, NA_DIM, NA_KH, NA_KW = 4, 64, 8, 16
MLA_HEADS, MLA_NOPE, MLA_ROPE, MLA_V = 4, 64, 32, 128
MLA_QK = MLA_NOPE + MLA_ROPE
MLA_Q_RANK, MLA_KV_RANK = 256, 128
ROPE_AXIS = MLA_ROPE // 2
ROPE_THETA = 10000.0
GLA_HEADS, GLA_DK, GLA_DV = 4, 32, 64
GLA_GATE_RANK, GLA_GATE_NORM, GLA_CHUNK = 16, 16.0, 64
NA_WIDTH = NA_HEADS * NA_DIM
MLA_WIDTH = MLA_HEADS * MLA_V
MLA_PAD = MLA_HEADS * V7X_LANES
GLA_KW = GLA_HEADS * GLA_DK
GLA_WIDTH = GLA_HEADS * GLA_DV
N_EXPERTS, TOP_K = 32, 4
SWIGLU_LIMIT, SWIGLU_ALPHA = 7.0, 1.702
MOE_BLOCK = 512

C_NAQ, C_NAK, C_NAV, C_CQ, C_CKV = 0, 256, 512, 768, 1024
C_GQ, C_GK, C_GV, C_GOUT, C_SMALL = 1152, 1280, 1408, 1664, 1920
IN_PACKED = 2048
S_LF, S_LB, S_KR = 0, 16, 64


def _vmem_limit(mib):
    return min(mib * 1024 * 1024, V7X_VMEM_BYTES - 4 * 1024 * 1024)


def _params(sem, mib=48):
    return pltpu.CompilerParams(dimension_semantics=sem, vmem_limit_bytes=_vmem_limit(mib))


def _dot(a, b):
    return jnp.dot(a, b, preferred_element_type=F32)


def _dot_nt(a, b):
    return lax.dot_general(a, b, (((1,), (1,)), ((), ())), preferred_element_type=F32)


def _dot_tn(a, b):
    return lax.dot_general(a, b, (((0,), (0,)), ((), ())), preferred_element_type=F32)


def _split(x):
    hi = x.astype(BF16)
    lo = (x - hi.astype(F32)).astype(BF16)
    return hi, lo


def _rms(x):
    return x * lax.rsqrt(jnp.mean(x * x, axis=-1, keepdims=True) + EPS)


V7X_SUBLANES = 8


def _store_row_tiled(ref, x):
    n = x.shape[0]
    for j in range(V7X_SUBLANES):
        ref[pl.ds(j, n, stride=V7X_SUBLANES), :] = x[:, V7X_LANES * j:V7X_LANES * (j + 1)]


def _load_row_tiled(ref):
    n = ref.shape[0] // V7X_SUBLANES
    return jnp.concatenate([ref[pl.ds(j, n, stride=V7X_SUBLANES), :] for j in range(V7X_SUBLANES)], axis=1)


def _ada_kernel(c_ref, w_ref, b_ref, o_ref):
    c = c_ref[...]
    s = (c * jax.nn.sigmoid(c)).astype(BF16)
    o_ref[...] = _dot(s, w_ref[...].astype(BF16)) + b_ref[...]


def _ada(cvec, w_ada, b_ada):
    depth, d, n6 = w_ada.shape
    r = cvec.shape[0]
    tn = 1024
    return pl.pallas_call(
        _ada_kernel,
        grid=(depth, n6 // tn),
        in_specs=[
            pl.BlockSpec((r, d), lambda l, j: (0, 0)),
            pl.BlockSpec((None, d, tn), lambda l, j: (l, 0, j)),
            pl.BlockSpec((None, 1, tn), lambda l, j: (l, 0, j)),
        ],
        out_specs=pl.BlockSpec((None, r, tn), lambda l, j: (l, 0, j)),
        out_shape=jax.ShapeDtypeStruct((depth, r, n6), F32),
        compiler_params=_params(("parallel", "parallel")),
        name="ada_modulation",
    )(cvec, w_ada, b_ada.reshape(depth, 1, n6))


def _rope(x, cos, sin):
    lane = lax.broadcasted_iota(jnp.int32, (x.shape[0], V7X_LANES), 1)
    first = (lane & (ROPE_AXIS - 1)) < (ROPE_AXIS // 2)
    outs = []
    for h in range(MLA_HEADS):
        xs = x[:, V7X_LANES * h:V7X_LANES * (h + 1)]
        rot = jnp.where(first, -pltpu.roll(xs, V7X_LANES - ROPE_AXIS // 2, 1), pltpu.roll(xs, ROPE_AXIS // 2, 1))
        outs.append(xs * cos + rot * sin)
    return jnp.concatenate(outs, axis=1)


def _head_rms_padded(x, n_real):
    outs = []
    for h in range(MLA_HEADS):
        xs = x[:, V7X_LANES * h:V7X_LANES * (h + 1)]
        ms = jnp.sum(xs * xs, axis=-1, keepdims=True) * (1.0 / n_real)
        outs.append(xs * lax.rsqrt(ms + EPS))
    return jnp.concatenate(outs, axis=1)


def _log_sigmoid(x):
    return jnp.minimum(x, 0.0) - jnp.log1p(jnp.exp(-jnp.abs(x)))


def _inproj_kernel(x_ref, sc_ref, sh_ref, gmix_ref, w_ref, cos_ref, sin_ref,
                   naqn_ref, nakn_ref, qan_ref, wqb_ref, kvan_ref, wkk_ref, wkv_ref, mqn_ref, mkn_ref,
                   wgh_ref, wgl_ref, bg_ref, seg_ref,
                   naq_o, nak_o, nav_o, mq_o, mk_o, mv_o, gq_o, gk_o, gv_o, gout_o, gf_o, gb_o, *, use_rope):
    x = x_ref[...]
    h = _rms(x) * gmix_ref[...]
    h = h * (1.0 + sc_ref[...]) + sh_ref[...]
    p = _dot(h.astype(BF16), w_ref[...])

    seg = seg_ref[...]
    q = p[:, C_NAQ:C_NAQ + NA_WIDTH]
    k = p[:, C_NAK:C_NAK + NA_WIDTH]
    qss = _dot((q * q).astype(BF16), seg) * (1.0 / NA_DIM)
    kss = _dot((k * k).astype(BF16), seg) * (1.0 / NA_DIM)
    naq_o[...] = (q * lax.rsqrt(qss + EPS) * naqn_ref[...] * (NA_DIM ** -0.5)).astype(BF16)
    nak_o[...] = (k * lax.rsqrt(kss + EPS) * nakn_ref[...]).astype(BF16)
    nav_o[...] = p[:, C_NAV:C_NAV + NA_WIDTH].astype(BF16)

    small = p[:, C_SMALL:C_SMALL + V7X_LANES]
    cq = _rms(p[:, C_CQ:C_CQ + MLA_Q_RANK]) * qan_ref[...]
    mq = _head_rms_padded(_dot(cq.astype(BF16), wqb_ref[...]), MLA_QK) * mqn_ref[...]
    ckv = (_rms(p[:, C_CKV:C_CKV + MLA_KV_RANK]) * kvan_ref[...]).astype(BF16)
    lane = lax.broadcasted_iota(jnp.int32, small.shape, 1)
    kr = jnp.where((lane >= S_KR) & (lane < S_KR + MLA_ROPE), small, 0.0)
    mk = _dot(ckv, wkk_ref[...]) + jnp.concatenate([kr] * MLA_HEADS, axis=1)
    mk = _head_rms_padded(mk, MLA_QK) * mkn_ref[...]
    if use_rope:
        cos, sin = cos_ref[...], sin_ref[...]
        mq = _rope(mq, cos, sin)
        mk = _rope(mk, cos, sin)
    mq_o[...] = (mq * (MLA_QK ** -0.5)).astype(BF16)
    mk_o[...] = mk.astype(BF16)
    mv_o[...] = _dot(ckv, wkv_ref[...]).astype(BF16)

    gq_o[...] = p[:, C_GQ:C_GQ + GLA_KW] * (GLA_DK ** -0.5)
    gk_o[...] = p[:, C_GK:C_GK + GLA_KW]
    gv_o[...] = p[:, C_GV:C_GV + GLA_WIDTH]
    gout_o[...] = p[:, C_GOUT:C_GOUT + GLA_WIDTH]
    s_hi, s_lo = _split(small)
    wgh = wgh_ref[...]
    pre = _dot(s_hi, wgh) + _dot(s_lo, wgh) + _dot(s_hi, wgl_ref[...]) + bg_ref[...]
    ls = _log_sigmoid(pre) * (1.0 / GLA_GATE_NORM)
    gf_o[...] = ls[:, :GLA_KW]
    gb_o[...] = ls[:, GLA_KW:]


def _inproj(x, sc, sh, lw, cos, sin, use_rope):
    b, t, d = x.shape
    tm = min(512, t)
    full = lambda a: pl.BlockSpec(a.shape, lambda i, j: (0,) * a.ndim)
    row = lambda w: pl.BlockSpec((None, tm, w), lambda i, j: (i, j, 0))
    consts = [lw["gmix"], lw["w_in"]]
    tail = [lw["naqn"], lw["nakn"], lw["qan"], lw["wqb"], lw["kvan"], lw["wkk"], lw["wkv"], lw["mqn"], lw["mkn"],
            lw["wg_hi"], lw["wg_lo"], lw["bg"], lw["seg64"]]
    widths = [(NA_WIDTH, BF16)] * 3 + [(MLA_PAD, BF16), (MLA_PAD, BF16), (MLA_WIDTH, BF16),
                                       (GLA_KW, F32), (GLA_KW, F32), (GLA_WIDTH, F32), (GLA_WIDTH, F32),
                                       (GLA_KW, F32), (GLA_KW, F32)]
    return pl.pallas_call(
        functools.partial(_inproj_kernel, use_rope=use_rope),
        grid=(b, t // tm),
        in_specs=[row(d),
                  pl.BlockSpec((None, 1, d), lambda i, j: (i, 0, 0)),
                  pl.BlockSpec((None, 1, d), lambda i, j: (i, 0, 0))]
                 + [full(a) for a in consts]
                 + [pl.BlockSpec((tm, V7X_LANES), lambda i, j: (j, 0))] * 2
                 + [full(a) for a in tail],
        out_specs=[row(w) for w, _ in widths],
        out_shape=[jax.ShapeDtypeStruct((b, t, w), dt) for w, dt in widths],
        compiler_params=_params(("parallel", "parallel")),
        name="inproj_rope" if use_rope else "inproj_ctx",
    )(x, sc, sh, *consts, cos, sin, *tail)


def _stack_heads(q, width):
    lane = lax.broadcasted_iota(jnp.int32, q.shape, 1)
    zero = jnp.zeros_like(q)
    return jnp.concatenate([jnp.where(lane // width == h, q, zero) for h in range(NA_HEADS)], axis=0)


def _unstack_heads(o4, n, width):
    lane = lax.broadcasted_iota(jnp.int32, (n, o4.shape[1]), 1)
    out = jnp.zeros((n, o4.shape[1]), F32)
    for h in range(NA_HEADS):
        out = out + jnp.where(lane // width == h, o4[h * n:(h + 1) * n], 0.0)
    return out


def _na_kernel(q_ref, k_ref, v_ref, kc_ref, vc_ref, bias_ref, o_ref, *, rows):
    kc = kc_ref[...]
    vc = vc_ref[...]
    n_loc = NA_KH * GRID_W

    def body(r, carry):
        rs = jnp.clip(r - NA_KH // 2, 0, rows - NA_KH)
        off = r - rs
        q = q_ref[pl.ds(pl.multiple_of(r * GRID_W, GRID_W), GRID_W), :]
        ks = k_ref[pl.ds(pl.multiple_of(rs * GRID_W, GRID_W), n_loc), :]
        vs = v_ref[pl.ds(pl.multiple_of(rs * GRID_W, GRID_W), n_loc), :]
        q4 = _stack_heads(q, NA_DIM)
        s_loc = _dot_nt(q4, ks) + bias_ref[off]
        s_ctx = _dot_nt(q4, kc)
        m = jnp.maximum(jnp.max(s_loc, axis=-1, keepdims=True), jnp.max(s_ctx, axis=-1, keepdims=True))
        p_loc = jnp.exp(s_loc - m)
        p_ctx = jnp.exp(s_ctx - m)
        denom = jnp.sum(p_loc, axis=-1, keepdims=True) + jnp.sum(p_ctx, axis=-1, keepdims=True)
        o4 = (_dot(p_loc.astype(BF16), vs) + _dot(p_ctx.astype(BF16), vc)) * (1.0 / denom)
        o_ref[pl.ds(pl.multiple_of(r * GRID_W, GRID_W), GRID_W), :] = _unstack_heads(o4, GRID_W, NA_DIM).astype(BF16)
        return carry

    lax.fori_loop(0, rows, body, 0, unroll=4)


def _na_bias_table(rpb, rows):
    kh = NA_KH
    r_all = np.arange(rows)
    offs = r_all - np.clip(r_all - kh // 2, 0, rows - kh)
    n_off = int(offs.max()) + 1
    qc = np.arange(GRID_W)
    kcol = np.arange(GRID_W)
    cs = np.clip(qc - NA_KW // 2, 0, GRID_W - NA_KW)
    in_win = (kcol[None, :] >= cs[:, None]) & (kcol[None, :] < cs[:, None] + NA_KW)
    dc = np.clip(kcol[None, :] - qc[:, None] + NA_KW - 1, 0, 2 * NA_KW - 2)
    onehot = (dc[None] == np.arange(2 * NA_KW - 1)[:, None, None]).astype(np.float32)
    toep = jnp.einsum("hdc,cqk->hdqk", rpb.astype(F32), jnp.asarray(onehot), precision=lax.Precision.HIGHEST)
    toep = jnp.where(in_win[None, None], toep, -jnp.inf)
    per_off = [toep[:, NA_KH - 1 - off:2 * NA_KH - 1 - off] for off in range(n_off)]
    bias = jnp.stack(per_off, axis=0)
    return bias.transpose(0, 1, 3, 2, 4).reshape(n_off, NA_HEADS * GRID_W, kh * GRID_W)


def _na_attention(q, k, v, kc, vc, bias):
    b, t, w = q.shape
    tc = kc.shape[1]
    rows = t // GRID_W
    assert rows >= NA_KH and t % GRID_W == 0
    seq = lambda n: pl.BlockSpec((None, n, w), lambda i: (i, 0, 0))
    return pl.pallas_call(
        functools.partial(_na_kernel, rows=rows),
        grid=(b,),
        in_specs=[seq(t), seq(t), seq(t), seq(tc), seq(tc),
                  pl.BlockSpec(bias.shape, lambda i: (0, 0, 0))],
        out_specs=seq(t),
        out_shape=jax.ShapeDtypeStruct((b, t, w), BF16),
        compiler_params=_params(("parallel",)),
        name="na_attention",
    )(q, k, v, kc, vc, bias)


def _softmax_pv(scores, values):
    m = functools.reduce(jnp.maximum, [jnp.max(s, axis=-1, keepdims=True) for s in scores])
    ps = [jnp.exp(s - m) for s in scores]
    denom = functools.reduce(jnp.add, [jnp.sum(p, axis=-1, keepdims=True) for p in ps])
    o = functools.reduce(jnp.add, [_dot(p.astype(BF16), v) for p, v in zip(ps, values)])
    return o * (1.0 / denom)


def _attn_sliced_kernel(*refs, n_parts, heads, dh, dv):
    q_ref, o_ref = refs[0], refs[-1]
    for h in range(heads):
        q = q_ref[:, dh * h:dh * (h + 1)]
        scores = [_dot_nt(q, refs[1 + 2 * p][:, dh * h:dh * (h + 1)]) for p in range(n_parts)]
        values = [refs[2 + 2 * p][:, dv * h:dv * (h + 1)] for p in range(n_parts)]
        o_ref[:, dv * h:dv * (h + 1)] = _softmax_pv(scores, values).astype(o_ref.dtype)


def _attn_masked_kernel(q_ref, k_ref, v_ref, o_ref, *, width):
    n = q_ref.shape[0]
    q4 = _stack_heads(q_ref[...], width)
    o4 = _softmax_pv([_dot_nt(q4, k_ref[...])], [v_ref[...]])
    o_ref[...] = _unstack_heads(o4, n, width).astype(o_ref.dtype)


def _attention(q, parts, heads, dh, dv, tq):
    b, t, wq = q.shape
    tq = min(tq, t)
    flat = [a for kv in parts for a in kv]
    kv_spec = lambda a: pl.BlockSpec((None,) + a.shape[1:], lambda i, j: (i, 0, 0))
    return pl.pallas_call(
        functools.partial(_attn_sliced_kernel, n_parts=len(parts), heads=heads, dh=dh, dv=dv),
        grid=(b, t // tq),
        in_specs=[pl.BlockSpec((None, tq, wq), lambda i, j: (i, j, 0))] + [kv_spec(a) for a in flat],
        out_specs=pl.BlockSpec((None, tq, heads * dv), lambda i, j: (i, j, 0)),
        out_shape=jax.ShapeDtypeStruct((b, t, heads * dv), BF16),
        compiler_params=_params(("parallel", "parallel")),
        name="mla_attention_%d" % len(parts),
    )(q, *flat)


def _attention_masked(q, k, v, width):
    b, t, w = q.shape
    spec = lambda a: pl.BlockSpec((None,) + a.shape[1:], lambda i: (i, 0, 0))
    return pl.pallas_call(
        functools.partial(_attn_masked_kernel, width=width),
        grid=(b,),
        in_specs=[spec(q), spec(k), spec(v)],
        out_specs=spec(q),
        out_shape=jax.ShapeDtypeStruct((b, t, w), BF16),
        compiler_params=_params(("parallel",)),
        name="ctx_na_attention",
    )(q, k, v)


GLA_SAFE_DECAY = 60.0
GLA_EXACT_CHUNK = 16


def _gla_masks(c):
    ri = lax.broadcasted_iota(jnp.int32, (c, c), 0)
    ci = lax.broadcasted_iota(jnp.int32, (c, c), 1)
    ti = lax.broadcasted_iota(jnp.int32, (c, GLA_HEADS * c), 0)
    si = lax.broadcasted_iota(jnp.int32, (c, GLA_HEADS * c), 1) % c
    lk = lax.broadcasted_iota(jnp.int32, (c, GLA_KW), 1) // GLA_DK
    lv = lax.broadcasted_iota(jnp.int32, (c, GLA_WIDTH), 1) // GLA_DV
    head_v = lax.broadcasted_iota(jnp.int32, (GLA_WIDTH, GLA_KW), 0) // GLA_DV
    head_k = lax.broadcasted_iota(jnp.int32, (GLA_WIDTH, GLA_KW), 1) // GLA_DK
    return {
        "tri": ((ci <= ri).astype(BF16), (ci >= ri).astype(BF16)),
        "allowed": (si <= ti, si >= ti),
        "hm_k": [lk == h for h in range(GLA_HEADS)],
        "hm_v": [lv == h for h in range(GLA_HEADS)],
        "hm_s": head_v == head_k,
        "expand": (lax.broadcasted_iota(jnp.int32, (GLA_KW, GLA_WIDTH), 0) // GLA_DK
                   == lax.broadcasted_iota(jnp.int32, (GLA_KW, GLA_WIDTH), 1) // GLA_DV).astype(BF16),
    }


def _gla_intra_pairwise(q, k, v, bcum, expand, reverse):
    c = q.shape[0]
    row = lax.broadcasted_iota(jnp.int32, (c, GLA_WIDTH), 0)
    outs = []
    for t in range(c):
        w = jnp.exp(jnp.minimum(bcum[t:t + 1, :] - bcum, 0.0))
        a = _dot(((q[t:t + 1, :] * k) * w).astype(BF16), expand)
        keep = (row >= t) if reverse else (row <= t)
        outs.append(jnp.sum(jnp.where(keep, a * v, 0.0), axis=0, keepdims=True))
    return jnp.concatenate(outs, axis=0)


def _gla_chunk(q, k, v, g, st_ref, m, reverse, pairwise):
    c = q.shape[0]
    d = 1 if reverse else 0
    end_row = 0 if reverse else c - 1
    g_hi, g_lo = _split(g)
    tri = m["tri"][d]
    bcum = _dot(tri, g_hi) + _dot(tri, g_lo)
    bend = bcum[end_row:end_row + 1, :]
    qe = (q * jnp.exp(bcum)).astype(BF16)
    ke = (k * jnp.exp(bend - bcum)).astype(BF16)
    st = st_ref[...]
    o = _dot_nt(qe, st.astype(BF16))
    if pairwise:
        o = o + _gla_intra_pairwise(q, k, v, bcum, m["expand"], reverse)
    else:
        kt = k * jnp.exp(-bcum)
        kst = jnp.concatenate([jnp.where(hm, kt, 0.0) for hm in m["hm_k"]], axis=0).astype(BF16)
        a = jnp.where(m["allowed"][d], _dot_nt(qe, kst), 0.0)
        vbd = jnp.concatenate([jnp.where(hm, v, 0.0) for hm in m["hm_v"]], axis=0).astype(BF16)
        o = o + _dot(a.astype(BF16), vbd)
    upd = _dot_tn(v.astype(BF16), ke)
    st_ref[...] = st * jnp.exp(bend) + jnp.where(m["hm_s"], upd, 0.0)
    return o


def _gla_kernel(q_ref, k_ref, v_ref, gf_ref, gb_ref, go_ref,
                qc_ref, kc_ref, vc_ref, gfc_ref, gbc_ref, goc_ref, onorm_ref, seg_ref,
                *rest, ctx_out):
    if ctx_out:
        ox_ref, oc_ref, acc_ref, accc_ref, sf_ref, sb_ref = rest
    else:
        ox_ref, acc_ref, sf_ref, sb_ref = rest
        oc_ref = accc_ref = None
    t, tc = q_ref.shape[0], qc_ref.shape[0]

    def sweep(qr, kr, vr, gfr, gbr, dst, total, c, m, pairwise):
        count = total // c

        def body(j, carry):
            lo_f = pl.multiple_of(j * c, c)
            lo_b = pl.multiple_of((count - 1 - j) * c, c)
            of = _gla_chunk(qr[pl.ds(lo_f, c), :], kr[pl.ds(lo_f, c), :], vr[pl.ds(lo_f, c), :],
                            gfr[pl.ds(lo_f, c), :], sf_ref, m, False, pairwise)
            ob = _gla_chunk(qr[pl.ds(lo_b, c), :], kr[pl.ds(lo_b, c), :], vr[pl.ds(lo_b, c), :],
                            gbr[pl.ds(lo_b, c), :], sb_ref, m, True, pairwise)
            if dst is not None:
                dst[pl.ds(lo_f, c), :] += of
                dst[pl.ds(lo_b, c), :] += ob
            return carry
        lax.fori_loop(0, count, body, 0, unroll=1 if pairwise else 4)

    def scans(c, pairwise):
        m = _gla_masks(c)
        sf_ref[...] = jnp.zeros_like(sf_ref)
        sb_ref[...] = jnp.zeros_like(sb_ref)
        acc_ref[...] = jnp.zeros_like(acc_ref)
        if ctx_out:
            accc_ref[...] = jnp.zeros_like(accc_ref)
        sweep(qc_ref, kc_ref, vc_ref, gfc_ref, gbc_ref, accc_ref, tc, c, m, pairwise)
        sweep(q_ref, k_ref, v_ref, gf_ref, gb_ref, acc_ref, t, c, m, pairwise)

    def min_chunk_decay(gr, total):
        def body(j, low):
            lo = pl.multiple_of(j * GLA_CHUNK, GLA_CHUNK)
            return jnp.minimum(low, jnp.sum(gr[pl.ds(lo, GLA_CHUNK), :], axis=0, keepdims=True))
        return lax.fori_loop(0, total // GLA_CHUNK, body, jnp.zeros((1, GLA_KW), F32))

    low = functools.reduce(jnp.minimum, [min_chunk_decay(gf_ref, t), min_chunk_decay(gb_ref, t),
                                         min_chunk_decay(gfc_ref, tc), min_chunk_decay(gbc_ref, tc)])
    safe = jnp.min(low) >= -GLA_SAFE_DECAY
    pl.when(safe)(functools.partial(scans, GLA_CHUNK, False))
    pl.when(jnp.logical_not(safe))(functools.partial(scans, GLA_EXACT_CHUNK, True))

    def finish(acc, gate, out, total):
        tr = min(256, total)

        def body(i, carry):
            lo = pl.multiple_of(i * tr, tr)
            o = acc[pl.ds(lo, tr), :]
            ms = _dot((o * o).astype(BF16), seg_ref[...]) * (1.0 / GLA_DV)
            gt = gate[pl.ds(lo, tr), :]
            y = o * lax.rsqrt(ms + EPS) * onorm_ref[...] * (gt * jax.nn.sigmoid(gt))
            out[pl.ds(lo, tr), :] = y.astype(out.dtype)
            return carry
        lax.fori_loop(0, total // tr, body, 0)

    finish(acc_ref, go_ref, ox_ref, t)
    if ctx_out:
        finish(accc_ref, goc_ref, oc_ref, tc)


def _gla(lat, ctx, onorm, seg64, ctx_out):
    b, t, _ = lat[0].shape
    tc = ctx[0].shape[1]
    spec = lambda a: pl.BlockSpec((None,) + a.shape[1:], lambda i: (i, 0, 0))
    full = lambda a: pl.BlockSpec(a.shape, lambda i: (0,) * a.ndim)
    out_shape = [jax.ShapeDtypeStruct((b, t, GLA_WIDTH), BF16)]
    scratch = [pltpu.VMEM((t, GLA_WIDTH), F32)]
    if ctx_out:
        out_shape.append(jax.ShapeDtypeStruct((b, tc, GLA_WIDTH), BF16))
        scratch.append(pltpu.VMEM((tc, GLA_WIDTH), F32))
    scratch += [pltpu.VMEM((GLA_WIDTH, GLA_KW), F32)] * 2
    outs = pl.pallas_call(
        functools.partial(_gla_kernel, ctx_out=ctx_out),
        grid=(b,),
        in_specs=[spec(a) for a in lat] + [spec(a) for a in ctx] + [full(onorm), full(seg64)],
        out_specs=[spec(s) for s in out_shape],
        out_shape=out_shape,
        scratch_shapes=scratch,
        compiler_params=_params(("parallel",)),
        name="gla_ctx_out" if ctx_out else "gla_last",
    )(*lat, *ctx, onorm, seg64)
    return outs if ctx_out else (outs[0], None)


def _outproj_kernel(a_ref, b_ref, c_ref, x_ref, gt_ref, sc_ref, sh_ref, gffn_ref,
                    wa_ref, wb_ref, wc_ref, wrh_ref, wrl_ref, br_ref, xo_ref, h_ref, lg_ref):
    mix = _dot(a_ref[...], wa_ref[...]) + _dot(b_ref[...], wb_ref[...]) + _dot(c_ref[...], wc_ref[...])
    xn = x_ref[...] + gt_ref[...] * mix
    xo_ref[...] = xn
    h = _rms(xn) * gffn_ref[...]
    h = h * (1.0 + sc_ref[...]) + sh_ref[...]
    _store_row_tiled(h_ref, h)
    h_hi, h_lo = _split(h)
    wrh = wrh_ref[...]
    lg_ref[...] = _dot(h_hi, wrh) + _dot(h_lo, wrh) + _dot(h_hi, wrl_ref[...]) + br_ref[...]


def _outproj(a, bm, c, x, gt, sc, sh, lw):
    b, t, d = x.shape
    tm = min(512, t)
    row = lambda w: pl.BlockSpec((None, tm, w), lambda i, j: (i, j, 0))
    vec = pl.BlockSpec((None, 1, d), lambda i, j: (i, 0, 0))
    full = lambda arr: pl.BlockSpec(arr.shape, lambda i, j: (0,) * arr.ndim)
    consts = [lw["gffn"], lw["wo_a"], lw["wo_b"], lw["wo_c"], lw["wr_hi"], lw["wr_lo"], lw["br"]]
    return pl.pallas_call(
        _outproj_kernel,
        grid=(b, t // tm),
        in_specs=[row(NA_WIDTH), row(MLA_WIDTH), row(GLA_WIDTH), row(d), vec, vec, vec] + [full(w) for w in consts],
        out_specs=[row(d), pl.BlockSpec((None, tm * V7X_SUBLANES, V7X_LANES), lambda i, j: (i, j, 0)), row(V7X_LANES)],
        out_shape=[jax.ShapeDtypeStruct((b, t, d), F32),
                   jax.ShapeDtypeStruct((b, t * V7X_SUBLANES, V7X_LANES), F32),
                   jax.ShapeDtypeStruct((b, t, V7X_LANES), F32)],
        compiler_params=_params(("parallel", "parallel")),
        name="outproj_router",
    )(a, bm, c, x, gt, sc, sh, *consts)


def _moe_kernel(be_ref, tok_hbm, pair_hbm, h_hbm, w1_ref, b1_ref, w2_ref, b2_ref, y_hbm,
                g0, g1, s0, s1, x0, x1, y0, y1, w1b_ref, w2b_ref, gisem, sisem, gsem, ssem):
    i = pl.program_id(0)
    last = pl.num_programs(0) - 1
    sub = V7X_SUBLANES
    blk = x0.shape[0] // sub
    f = w2_ref.shape[0]
    gbuf, sbuf, xbuf, ybuf = (g0, g1), (s0, s1), (x0, x1), (y0, y1)

    def load_tok(t, s):
        return pltpu.make_async_copy(tok_hbm.at[pl.ds(pl.multiple_of(t * blk, blk), blk)], gbuf[s], gisem.at[s])

    def load_pair(t, s):
        return pltpu.make_async_copy(pair_hbm.at[pl.ds(pl.multiple_of(t * blk, blk), blk)], sbuf[s], sisem.at[s])

    def tile(ref, first_row):
        if not isinstance(first_row, int):
            first_row = pl.multiple_of(first_row, sub)
        return ref.at[pl.ds(first_row, sub)]

    def start_gathers(s):
        for r in range(blk):
            pltpu.make_async_copy(tile(h_hbm, gbuf[s][r]), tile(xbuf[s], r * sub), gsem.at[s]).start(
                priority=r % DMA_QUEUES)

    def wait_gathers(s):
        for r in range(blk):
            pltpu.make_async_copy(tile(h_hbm, 0), tile(xbuf[s], r * sub), gsem.at[s]).wait()

    def start_scatters(s):
        for r in range(blk):
            pltpu.make_async_copy(tile(ybuf[s], r * sub), tile(y_hbm, sbuf[s][r]), ssem.at[s]).start(
                priority=r % DMA_QUEUES)

    def wait_scatters(s):
        for r in range(blk):
            pltpu.make_async_copy(tile(ybuf[s], r * sub), tile(y_hbm, 0), ssem.at[s]).wait()

    @pl.when(i == 0)
    def _():
        y0[...] = jnp.zeros_like(y0)
        y1[...] = jnp.zeros_like(y1)
        first_spare = y_hbm.shape[0] - 2 * blk * sub
        for r in range(blk):
            pltpu.make_async_copy(tile(y0, r * sub), tile(y_hbm, first_spare + r * sub), ssem.at[0]).start()
        load_tok(1, 0).start()
        load_tok(2, 1).start()
        load_pair(0, 1).start()
        load_tok(1, 0).wait()
        start_gathers(0)

    e = be_ref[i]
    prev = be_ref[jnp.maximum(i - 1, 0)]

    @pl.when((i == 0) | (e != prev))
    def _():
        rows = 128

        def cast1(j, carry):
            lo = pl.multiple_of(j * rows, rows)
            w1b_ref[pl.ds(lo, rows), :] = w1_ref[pl.ds(lo, rows), :].astype(BF16)
            return carry

        def cast2(j, carry):
            lo = pl.multiple_of(j * rows, rows)
            w2b_ref[pl.ds(lo, rows), :] = w2_ref[pl.ds(lo, rows), :].astype(BF16)
            return carry
        lax.fori_loop(0, w1_ref.shape[0] // rows, cast1, 0)
        lax.fori_loop(0, w2_ref.shape[0] // rows, cast2, 0)

    def step(p):
        q = 1 - p

        load_tok(i + 2, q).wait()
        load_pair(i, q).wait()
        load_tok(i + 3, p).start()
        load_pair(i + 1, p).start()
        wait_gathers(p)
        wait_scatters(p)
        start_gathers(q)
        start_scatters(q)

        hb = _dot(_load_row_tiled(xbuf[p]).astype(BF16), w1b_ref[...]) + b1_ref[...]
        glu = jnp.minimum(hb[:, :f], SWIGLU_LIMIT)
        lin = jnp.clip(hb[:, f:], -SWIGLU_LIMIT, SWIGLU_LIMIT)
        act = (lin + 1.0) * (glu * jax.nn.sigmoid(SWIGLU_ALPHA * glu))
        _store_row_tiled(ybuf[p], _dot(act.astype(BF16), w2b_ref[...]) + b2_ref[...])

        @pl.when(i == last)
        def _():
            load_tok(i + 3, p).wait()
            load_pair(i + 1, p).wait()
            wait_gathers(q)
            wait_scatters(q)
            start_scatters(p)
            wait_scatters(p)

    for p in range(2):
        pl.when(i % 2 == p)(functools.partial(step, p))


def _moe_blocks(h, tok_tab, pair_tab, blk_e, n_rows_out, w1, b1, w2, b2, layer):
    d = V7X_SUBLANES * V7X_LANES
    n_blk = blk_e.shape[0]
    f2 = w1.shape[-1]
    f = w2.shape[-2]
    grid_spec = pltpu.PrefetchScalarGridSpec(
        num_scalar_prefetch=1,
        grid=(n_blk,),
        in_specs=[
            pl.BlockSpec(memory_space=pl.ANY),
            pl.BlockSpec(memory_space=pl.ANY),
            pl.BlockSpec(memory_space=pl.ANY),
            pl.BlockSpec((None, None, d, f2), lambda i, be: (layer, be[i], 0, 0)),
            pl.BlockSpec((None, None, 1, f2), lambda i, be: (layer, be[i], 0, 0)),
            pl.BlockSpec((None, None, f, d), lambda i, be: (layer, be[i], 0, 0)),
            pl.BlockSpec((None, None, 1, d), lambda i, be: (layer, be[i], 0, 0)),
        ],
        out_specs=pl.BlockSpec(memory_space=pl.ANY),
        scratch_shapes=[pltpu.SMEM((MOE_BLOCK,), jnp.int32)] * 4
        + [pltpu.VMEM((MOE_BLOCK * V7X_SUBLANES, V7X_LANES), F32)] * 4 + [
            pltpu.VMEM((d, f2), BF16),
            pltpu.VMEM((f, d), BF16),
        ] + [pltpu.SemaphoreType.DMA((2,))] * 4,
    )
    return pl.pallas_call(
        _moe_kernel,
        grid_spec=grid_spec,
        out_shape=jax.ShapeDtypeStruct((n_rows_out * V7X_SUBLANES, V7X_LANES), F32),
        compiler_params=_params(("arbitrary",), mib=56),
        name="moe_experts",
    )(blk_e, tok_tab, pair_tab, h, w1, b1.reshape(b1.shape[0], b1.shape[1], 1, f2), w2,
      b2.reshape(b2.shape[0], b2.shape[1], 1, d))


def _moe(h, logits, w1, b1, w2, b2, layer):
    n = logits.shape[0]
    nk = n * TOP_K
    top_val, top_idx = lax.top_k(logits, TOP_K)
    gates = jax.nn.softmax(top_val, axis=-1)
    flat_e = top_idx.reshape(-1)
    order = jnp.argsort(flat_e).astype(jnp.int32)
    counts = jnp.bincount(flat_e, length=N_EXPERTS).astype(jnp.int32)
    padded = (counts + MOE_BLOCK - 1) // MOE_BLOCK * MOE_BLOCK
    start = jnp.cumsum(counts) - counts
    pend = jnp.cumsum(padded)
    pstart = pend - padded
    n_blk = -(-nk // MOE_BLOCK) + N_EXPERTS
    blk_id = jnp.arange(-1, n_blk + 2, dtype=jnp.int32)
    blk_lo = blk_id * MOE_BLOCK
    blk_e = jnp.minimum(jnp.sum((pend[None, :] <= blk_lo[:, None]).astype(jnp.int32), axis=1), N_EXPERTS - 1)
    s0 = start[blk_e] + blk_lo - pstart[blk_e]
    left = jnp.where(blk_id >= 0, start[blk_e] + counts[blk_e] - s0, 0)
    r = jnp.arange(MOE_BLOCK, dtype=jnp.int32)[None, :]
    valid = r < left[:, None]
    win = order[jnp.clip(s0[:, None] + r, 0, nk - 1)]
    tok = lax.shift_right_logical(win, 2)
    dst = (win & (TOP_K - 1)) * n + tok
    dst_tab = jnp.where(valid, dst, nk + (blk_id[:, None] % 2) * MOE_BLOCK + r).reshape(-1)
    tok_tab = jnp.where(valid, tok, 0).reshape(-1)
    y4 = _moe_blocks(h, tok_tab * V7X_SUBLANES, dst_tab * V7X_SUBLANES, blk_e[1:n_blk + 1], nk + 2 * MOE_BLOCK,
                     w1, b1, w2, b2, layer)
    return y4, gates


def _combine_kernel(x_ref, gt_ref, g_ref, y0_ref, y1_ref, y2_ref, y3_ref, o_ref):
    g = g_ref[...]
    mix = g[:, 0:1] * _load_row_tiled(y0_ref)
    for k, y_ref in enumerate((y1_ref, y2_ref, y3_ref), start=1):
        mix = mix + g[:, k:k + 1] * _load_row_tiled(y_ref)
    o_ref[...] = x_ref[...] + gt_ref[...] * mix


def _combine(x, gt, gates, y4, n_tok, first_tok):
    b, t, d = x.shape
    tm = int(min(512, np.gcd.reduce([t, n_tok, first_tok or n_tok])))
    per_b = t // tm
    plane = n_tok // tm
    first = first_tok // tm
    row = pl.BlockSpec((None, tm, d), lambda i, j: (i, j, 0))
    y_spec = lambda k: pl.BlockSpec((tm * V7X_SUBLANES, V7X_LANES),
                                    lambda i, j: (k * plane + first + i * per_b + j, 0))
    return pl.pallas_call(
        _combine_kernel,
        grid=(b, per_b),
        in_specs=[row, pl.BlockSpec((None, 1, d), lambda i, j: (i, 0, 0)),
                  pl.BlockSpec((None, tm, TOP_K), lambda i, j: (i, j, 0))] + [y_spec(k) for k in range(TOP_K)],
        out_specs=row,
        out_shape=jax.ShapeDtypeStruct((b, t, d), F32),
        compiler_params=_params(("parallel", "parallel")),
        name="moe_combine",
    )(x, gt, gates, y4, y4, y4, y4)


def _rope_tables(t):
    tok = jnp.arange(t)
    row = (tok // GRID_W).astype(F32)[:, None]
    col = (tok % GRID_W).astype(F32)[:, None]
    inv = 1.0 / (ROPE_THETA ** (jnp.arange(0, ROPE_AXIS, 2, dtype=F32) / ROPE_AXIS))
    ang = jnp.concatenate([row * inv, row * inv, col * inv, col * inv], axis=-1)
    cos = jnp.ones((t, V7X_LANES), F32).at[:, MLA_NOPE:MLA_QK].set(jnp.cos(ang))
    sin = jnp.zeros((t, V7X_LANES), F32).at[:, MLA_NOPE:MLA_QK].set(jnp.sin(ang))
    return cos, sin


def _pad_heads(w, per_head, offset=0):
    lead = w.shape[:-1]
    w = w.reshape(lead + (MLA_HEADS, per_head))
    out = jnp.zeros(lead + (MLA_HEADS, V7X_LANES), w.dtype).at[..., offset:offset + per_head].set(w)
    return out.reshape(lead + (MLA_PAD,))


def _layer_weights(l, g_mix, w_in, na_q_norm, na_k_norm, mla_q_a_norm, mla_w_q_b, mla_kv_a_norm, mla_w_kv_b,
                   mla_q_norm, mla_k_norm, gla_w_gate, gla_b_gate, gla_o_norm, w_out, g_ffn, w_router, b_router):
    d = w_in.shape[1]
    sizes = (NA_WIDTH, NA_WIDTH, NA_WIDTH, MLA_Q_RANK, MLA_KV_RANK, MLA_ROPE, GLA_KW, GLA_KW, GLA_WIDTH, GLA_WIDTH,
             GLA_GATE_RANK, GLA_GATE_RANK)
    (wq, wk, wv, wcq, wckv, wkr, wgq, wgk, wgv, wgo, wlf, wlb) = jnp.split(w_in[l], np.cumsum(sizes)[:-1].tolist(), axis=-1)
    small = jnp.zeros((d, V7X_LANES), F32)
    small = small.at[:, S_LF:S_LF + GLA_GATE_RANK].set(wlf).at[:, S_LB:S_LB + GLA_GATE_RANK].set(wlb)
    small = small.at[:, S_KR:S_KR + MLA_ROPE].set(wkr)
    w_packed = jnp.concatenate([wq, wk, wv, wcq, wckv, wgq, wgk, wgv, wgo, small], axis=-1).astype(BF16)
    kvb = mla_w_kv_b[l].reshape(MLA_KV_RANK, MLA_HEADS, MLA_NOPE + MLA_V)
    wg = jnp.zeros((V7X_LANES, 2 * GLA_KW), F32)
    wg = wg.at[S_LF:S_LF + GLA_GATE_RANK, :GLA_KW].set(gla_w_gate[l, 0])
    wg = wg.at[S_LB:S_LB + GLA_GATE_RANK, GLA_KW:].set(gla_w_gate[l, 1])
    wg_hi, wg_lo = _split(wg)
    seg = (np.arange(NA_WIDTH)[:, None] // NA_DIM == np.arange(NA_WIDTH)[None, :] // NA_DIM)
    wr = jnp.zeros((d, V7X_LANES), F32).at[:, :N_EXPERTS].set(w_router[l])
    wr_hi, wr_lo = _split(wr)
    wo = w_out[l].astype(BF16)
    return {
        "gmix": g_mix[l][None, :],
        "w_in": w_packed,
        "naqn": jnp.tile(na_q_norm[l], NA_HEADS)[None, :],
        "nakn": jnp.tile(na_k_norm[l], NA_HEADS)[None, :],
        "qan": mla_q_a_norm[l][None, :],
        "wqb": _pad_heads(mla_w_q_b[l], MLA_QK).astype(BF16),
        "kvan": mla_kv_a_norm[l][None, :],
        "wkk": _pad_heads(kvb[:, :, :MLA_NOPE].reshape(MLA_KV_RANK, -1), MLA_NOPE).astype(BF16),
        "wkv": kvb[:, :, MLA_NOPE:].reshape(MLA_KV_RANK, MLA_WIDTH).astype(BF16),
        "mqn": _pad_heads(jnp.tile(mla_q_norm[l], MLA_HEADS), MLA_QK)[None, :],
        "mkn": _pad_heads(jnp.tile(mla_k_norm[l], MLA_HEADS), MLA_QK)[None, :],
        "wg_hi": wg_hi, "wg_lo": wg_lo,
        "bg": jnp.concatenate([gla_b_gate[l, 0], gla_b_gate[l, 1]])[None, :],
        "seg64": jnp.asarray(seg, BF16),
        "onorm": jnp.tile(gla_o_norm[l], GLA_HEADS)[None, :],
        "gffn": g_ffn[l][None, :],
        "wo_a": wo[:NA_WIDTH], "wo_b": wo[NA_WIDTH:NA_WIDTH + MLA_WIDTH], "wo_c": wo[NA_WIDTH + MLA_WIDTH:],
        "wr_hi": wr_hi, "wr_lo": wr_lo,
        "br": jnp.zeros((1, V7X_LANES), F32).at[0, :N_EXPERTS].set(b_router[l]),
    }


def kernel(x, c, ctx, c_ctx, w_ada, b_ada, g_mix, w_in, na_q_norm, na_k_norm, na_rpb, mla_q_a_norm, mla_w_q_b,
           mla_kv_a_norm, mla_w_kv_b, mla_q_norm, mla_k_norm, gla_w_gate, gla_b_gate, gla_o_norm, w_out, g_ffn,
           w_router, b_router, w_moe1, b_moe1, w_moe2, b_moe2):
    bsz, t, d = x.shape
    assert d == V7X_SUBLANES * V7X_LANES
    tc = ctx.shape[1]
    depth = w_ada.shape[0]
    rows = t // GRID_W
    cos, sin = _rope_tables(t)

    pad = (-(bsz + 1)) % 8
    cvec = jnp.concatenate([c, c_ctx[None, :], jnp.zeros((pad, d), F32)], axis=0)
    mod = _ada(cvec, w_ada, b_ada)

    h_ctx = ctx
    for l in range(depth):
        ctx_out = l < depth - 1
        lw = _layer_weights(l, g_mix, w_in, na_q_norm, na_k_norm, mla_q_a_norm, mla_w_q_b, mla_kv_a_norm,
                            mla_w_kv_b, mla_q_norm, mla_k_norm, gla_w_gate, gla_b_gate, gla_o_norm, w_out, g_ffn,
                            w_router, b_router)
        m_lat = [m[:, None, :] for m in jnp.split(mod[l, :bsz], 6, axis=-1)]
        m_ctx = [jnp.broadcast_to(m[None, :, :], (bsz, 1, d)) for m in jnp.split(mod[l, bsz:bsz + 1], 6, axis=-1)]
        sh1, sc1, gt1, sh2, sc2, gt2 = m_lat
        csh1, csc1, cgt1, csh2, csc2, cgt2 = m_ctx

        (naq, nak, nav, mq, mk, mv, gq, gk, gv, gout, gf, gb) = _inproj(x, sc1, sh1, lw, cos, sin, True)
        (cnaq, cnak, cnav, cmq, cmk, cmv, cgq, cgk, cgv, cgout, cgf, cgb) = _inproj(
            h_ctx, csc1, csh1, lw, cos[:tc], sin[:tc], False)

        bias = _na_bias_table(na_rpb[l], rows)
        a_x = _na_attention(naq, nak, nav, cnak, cnav, bias)
        b_x = _attention(mq, [(mk, mv), (cmk, cmv)], MLA_HEADS, V7X_LANES, MLA_V, 256)
        c_x, c_c = _gla((gq, gk, gv, gf, gb, gout), (cgq, cgk, cgv, cgf, cgb, cgout), lw["onorm"], lw["seg64"], ctx_out)
        x_new, h2, logits = _outproj(a_x, b_x, c_x, x, gt1, sc2, sh2, lw)

        toks = [h2.reshape(bsz * t * V7X_SUBLANES, V7X_LANES)]
        lgs = [logits.reshape(bsz * t, V7X_LANES)]
        if ctx_out:
            a_c = _attention_masked(cnaq, cnak, cnav, NA_DIM)
            b_c = _attention(cmq, [(cmk, cmv)], MLA_HEADS, V7X_LANES, MLA_V, 256)
            hc_new, hc2, clogits = _outproj(a_c, b_c, c_c, h_ctx, cgt1, csc2, csh2, lw)
            toks.append(hc2.reshape(bsz * tc * V7X_SUBLANES, V7X_LANES))
            lgs.append(clogits.reshape(bsz * tc, V7X_LANES))
        n_tok = sum(a.shape[0] for a in lgs)
        y4, gates = _moe(jnp.concatenate(toks, axis=0), jnp.concatenate(lgs, axis=0)[:, :N_EXPERTS],
                         w_moe1, b_moe1, w_moe2, b_moe2, l)
        x = _combine(x_new, gt2, gates[:bsz * t].reshape(bsz, t, TOP_K), y4, n_tok, 0)
        if ctx_out:
            h_ctx = _combine(hc_new, cgt2, gates[bsz * t:].reshape(bsz, tc, TOP_K), y4, n_tok, bsz * t)
    return x
```

```python
import functools

import numpy as np
import jax
import jax.numpy as jnp
from jax import lax
from jax.experimental import pallas as pl
from jax.experimental.pallas import tpu as pltpu

F32 = jnp.float32
BF16 = jnp.bfloat16

V7X_LANES = 128
V7X_VMEM_BYTES = 64 * 1024 * 1024
DMA_QUEUES = 2

EPS = 1e-6
GRID_W = 64
NA_HEADS, NA_DIM, NA_KH, NA_KW = 4, 64, 8, 16
MLA_HEADS, MLA_NOPE, MLA_ROPE, MLA_V = 4, 64, 32, 128
MLA_QK = MLA_NOPE + MLA_ROPE
MLA_Q_RANK, MLA_KV_RANK = 256, 128
ROPE_AXIS = MLA_ROPE // 2
ROPE_THETA = 10000.0
GLA_HEADS, GLA_DK, GLA_DV = 4, 32, 64
GLA_GATE_RANK, GLA_GATE_NORM, GLA_CHUNK = 16, 16.0, 64
NA_WIDTH = NA_HEADS * NA_DIM
MLA_WIDTH = MLA_HEADS * MLA_V
MLA_PAD = MLA_HEADS * V7X_LANES
GLA_KW = GLA_HEADS * GLA_DK
GLA_WIDTH = GLA_HEADS * GLA_DV
N_EXPERTS, TOP_K = 32, 4
SWIGLU_LIMIT, SWIGLU_ALPHA = 7.0, 1.702
MOE_BLOCK = 512
MOE_RING = 3

C_NAQ, C_NAK, C_NAV, C_CQ, C_CKV = 0, 256, 512, 768, 1024
C_GQ, C_GK, C_GV, C_GOUT, C_SMALL = 1152, 1280, 1408, 1664, 1920
IN_PACKED = 2048
S_LF, S_LB, S_KR = 0, 16, 64


def _vmem_limit(mib):
    return min(mib * 1024 * 1024, V7X_VMEM_BYTES - 4 * 1024 * 1024)


def _params(sem, mib=48):
    return pltpu.CompilerParams(dimension_semantics=sem, vmem_limit_bytes=_vmem_limit(mib))


def _dot(a, b):
    return jnp.dot(a, b, preferred_element_type=F32)


def _dot_nt(a, b):
    return lax.dot_general(a, b, (((1,), (1,)), ((), ())), preferred_element_type=F32)


def _dot_tn(a, b):
    return lax.dot_general(a, b, (((0,), (0,)), ((), ())), preferred_element_type=F32)


def _split(x):
    hi = x.astype(BF16)
    lo = (x - hi.astype(F32)).astype(BF16)
    return hi, lo


def _rms(x):
    return x * lax.rsqrt(jnp.mean(x * x, axis=-1, keepdims=True) + EPS)


V7X_SUBLANES = 8


def _store_row_tiled(ref, x):
    n = x.shape[0]
    for j in range(V7X_SUBLANES):
        ref[pl.ds(j, n, stride=V7X_SUBLANES), :] = x[:, V7X_LANES * j:V7X_LANES * (j + 1)]


def _load_row_tiled(ref):
    n = ref.shape[0] // V7X_SUBLANES
    return jnp.concatenate([ref[pl.ds(j, n, stride=V7X_SUBLANES), :] for j in range(V7X_SUBLANES)], axis=1)


def _ada_kernel(c_ref, w_ref, b_ref, o_ref):
    c = c_ref[...]
    s = (c * jax.nn.sigmoid(c)).astype(BF16)
    o_ref[...] = _dot(s, w_ref[...].astype(BF16)) + b_ref[...]


def _ada(cvec, w_ada, b_ada):
    depth, d, n6 = w_ada.shape
    r = cvec.shape[0]
    tn = 1024
    return pl.pallas_call(
        _ada_kernel,
        grid=(depth, n6 // tn),
        in_specs=[
            pl.BlockSpec((r, d), lambda l, j: (0, 0)),
            pl.BlockSpec((None, d, tn), lambda l, j: (l, 0, j)),
            pl.BlockSpec((None, 1, tn), lambda l, j: (l, 0, j)),
        ],
        out_specs=pl.BlockSpec((None, r, tn), lambda l, j: (l, 0, j)),
        out_shape=jax.ShapeDtypeStruct((depth, r, n6), F32),
        compiler_params=_params(("parallel", "parallel")),
        name="ada_modulation",
    )(cvec, w_ada, b_ada.reshape(depth, 1, n6))


def _rope(x, cos, sin):
    lane = lax.broadcasted_iota(jnp.int32, (x.shape[0], V7X_LANES), 1)
    first = (lane & (ROPE_AXIS - 1)) < (ROPE_AXIS // 2)
    outs = []
    for h in range(MLA_HEADS):
        xs = x[:, V7X_LANES * h:V7X_LANES * (h + 1)]
        rot = jnp.where(first, -pltpu.roll(xs, V7X_LANES - ROPE_AXIS // 2, 1), pltpu.roll(xs, ROPE_AXIS // 2, 1))
        outs.append(xs * cos + rot * sin)
    return jnp.concatenate(outs, axis=1)


def _head_rms_padded(x, n_real):
    outs = []
    for h in range(MLA_HEADS):
        xs = x[:, V7X_LANES * h:V7X_LANES * (h + 1)]
        ms = jnp.sum(xs * xs, axis=-1, keepdims=True) * (1.0 / n_real)
        outs.append(xs * lax.rsqrt(ms + EPS))
    return jnp.concatenate(outs, axis=1)


def _log_sigmoid(x):
    return jnp.minimum(x, 0.0) - jnp.log1p(jnp.exp(-jnp.abs(x)))


def _inproj_kernel(x_ref, sc_ref, sh_ref, gmix_ref, w_ref, cos_ref, sin_ref,
                   naqn_ref, nakn_ref, qan_ref, wqb_ref, kvan_ref, wkk_ref, wkv_ref, mqn_ref, mkn_ref,
                   wgh_ref, wgl_ref, bg_ref, seg_ref,
                   naq_o, nak_o, nav_o, mq_o, mk_o, mv_o, gq_o, gk_o, gv_o, gout_o, gf_o, gb_o, *, use_rope):
    x = x_ref[...]
    h = _rms(x) * gmix_ref[...]
    h = h * (1.0 + sc_ref[...]) + sh_ref[...]
    p = _dot(h.astype(BF16), w_ref[...])

    seg = seg_ref[...]
    q = p[:, C_NAQ:C_NAQ + NA_WIDTH]
    k = p[:, C_NAK:C_NAK + NA_WIDTH]
    qss = _dot((q * q).astype(BF16), seg) * (1.0 / NA_DIM)
    kss = _dot((k * k).astype(BF16), seg) * (1.0 / NA_DIM)
    naq_o[...] = (q * lax.rsqrt(qss + EPS) * naqn_ref[...] * (NA_DIM ** -0.5)).astype(BF16)
    nak_o[...] = (k * lax.rsqrt(kss + EPS) * nakn_ref[...]).astype(BF16)
    nav_o[...] = p[:, C_NAV:C_NAV + NA_WIDTH].astype(BF16)

    small = p[:, C_SMALL:C_SMALL + V7X_LANES]
    cq = _rms(p[:, C_CQ:C_CQ + MLA_Q_RANK]) * qan_ref[...]
    mq = _head_rms_padded(_dot(cq.astype(BF16), wqb_ref[...]), MLA_QK) * mqn_ref[...]
    ckv = (_rms(p[:, C_CKV:C_CKV + MLA_KV_RANK]) * kvan_ref[...]).astype(BF16)
    lane = lax.broadcasted_iota(jnp.int32, small.shape, 1)
    kr = jnp.where((lane >= S_KR) & (lane < S_KR + MLA_ROPE), small, 0.0)
    mk = _dot(ckv, wkk_ref[...]) + jnp.concatenate([kr] * MLA_HEADS, axis=1)
    mk = _head_rms_padded(mk, MLA_QK) * mkn_ref[...]
    if use_rope:
        cos, sin = cos_ref[...], sin_ref[...]
        mq = _rope(mq, cos, sin)
        mk = _rope(mk, cos, sin)
    mq_o[...] = (mq * (MLA_QK ** -0.5)).astype(BF16)
    mk_o[...] = mk.astype(BF16)
    mv_o[...] = _dot(ckv, wkv_ref[...]).astype(BF16)

    gq_o[...] = p[:, C_GQ:C_GQ + GLA_KW] * (GLA_DK ** -0.5)
    gk_o[...] = p[:, C_GK:C_GK + GLA_KW]
    gv_o[...] = p[:, C_GV:C_GV + GLA_WIDTH]
    gout_o[...] = p[:, C_GOUT:C_GOUT + GLA_WIDTH]
    s_hi, s_lo = _split(small)
    wgh = wgh_ref[...]
    pre = _dot(s_hi, wgh) + _dot(s_lo, wgh) + _dot(s_hi, wgl_ref[...]) + bg_ref[...]
    ls = _log_sigmoid(pre) * (1.0 / GLA_GATE_NORM)
    gf_o[...] = ls[:, :GLA_KW]
    gb_o[...] = ls[:, GLA_KW:]


def _inproj(x, sc, sh, lw, cos, sin, use_rope):
    b, t, d = x.shape
    tm = min(512, t)
    full = lambda a: pl.BlockSpec(a.shape, lambda i, j: (0,) * a.ndim)
    row = lambda w: pl.BlockSpec((None, tm, w), lambda i, j: (i, j, 0))
    consts = [lw["gmix"], lw["w_in"]]
    tail = [lw["naqn"], lw["nakn"], lw["qan"], lw["wqb"], lw["kvan"], lw["wkk"], lw["wkv"], lw["mqn"], lw["mkn"],
            lw["wg_hi"], lw["wg_lo"], lw["bg"], lw["seg64"]]
    widths = [(NA_WIDTH, BF16)] * 3 + [(MLA_PAD, BF16), (MLA_PAD, BF16), (MLA_WIDTH, BF16),
                                       (GLA_KW, F32), (GLA_KW, F32), (GLA_WIDTH, F32), (GLA_WIDTH, F32),
                                       (GLA_KW, F32), (GLA_KW, F32)]
    return pl.pallas_call(
        functools.partial(_inproj_kernel, use_rope=use_rope),
        grid=(b, t // tm),
        in_specs=[row(d),
                  pl.BlockSpec((None, 1, d), lambda i, j: (i, 0, 0)),
                  pl.BlockSpec((None, 1, d), lambda i, j: (i, 0, 0))]
                 + [full(a) for a in consts]
                 + [pl.BlockSpec((tm, V7X_LANES), lambda i, j: (j, 0))] * 2
                 + [full(a) for a in tail],
        out_specs=[row(w) for w, _ in widths],
        out_shape=[jax.ShapeDtypeStruct((b, t, w), dt) for w, dt in widths],
        compiler_params=_params(("parallel", "parallel")),
        name="inproj_rope" if use_rope else "inproj_ctx",
    )(x, sc, sh, *consts, cos, sin, *tail)


def _stack_heads(q, width):
    lane = lax.broadcasted_iota(jnp.int32, q.shape, 1)
    zero = jnp.zeros_like(q)
    return jnp.concatenate([jnp.where(lane // width == h, q, zero) for h in range(NA_HEADS)], axis=0)


def _unstack_heads(o4, n, width):
    lane = lax.broadcasted_iota(jnp.int32, (n, o4.shape[1]), 1)
    out = jnp.zeros((n, o4.shape[1]), F32)
    for h in range(NA_HEADS):
        out = out + jnp.where(lane // width == h, o4[h * n:(h + 1) * n], 0.0)
    return out


def _na_kernel(q_ref, k_ref, v_ref, kc_ref, vc_ref, bias_ref, o_ref, *, rows):
    kc = kc_ref[...]
    vc = vc_ref[...]
    n_loc = NA_KH * GRID_W

    def body(r, carry):
        rs = jnp.clip(r - NA_KH // 2, 0, rows - NA_KH)
        off = r - rs
        q = q_ref[pl.ds(pl.multiple_of(r * GRID_W, GRID_W), GRID_W), :]
        ks = k_ref[pl.ds(pl.multiple_of(rs * GRID_W, GRID_W), n_loc), :]
        vs = v_ref[pl.ds(pl.multiple_of(rs * GRID_W, GRID_W), n_loc), :]
        q4 = _stack_heads(q, NA_DIM)
        s_loc = _dot_nt(q4, ks) + bias_ref[off]
        s_ctx = _dot_nt(q4, kc)
        m = jnp.maximum(jnp.max(s_loc, axis=-1, keepdims=True), jnp.max(s_ctx, axis=-1, keepdims=True))
        p_loc = jnp.exp(s_loc - m)
        p_ctx = jnp.exp(s_ctx - m)
        denom = jnp.sum(p_loc, axis=-1, keepdims=True) + jnp.sum(p_ctx, axis=-1, keepdims=True)
        o4 = (_dot(p_loc.astype(BF16), vs) + _dot(p_ctx.astype(BF16), vc)) * (1.0 / denom)
        o_ref[pl.ds(pl.multiple_of(r * GRID_W, GRID_W), GRID_W), :] = _unstack_heads(o4, GRID_W, NA_DIM).astype(BF16)
        return carry

    lax.fori_loop(0, rows, body, 0, unroll=4)


def _na_bias_table(rpb, rows):
    kh = NA_KH
    r_all = np.arange(rows)
    offs = r_all - np.clip(r_all - kh // 2, 0, rows - kh)
    n_off = int(offs.max()) + 1
    qc = np.arange(GRID_W)
    kcol = np.arange(GRID_W)
    cs = np.clip(qc - NA_KW // 2, 0, GRID_W - NA_KW)
    in_win = (kcol[None, :] >= cs[:, None]) & (kcol[None, :] < cs[:, None] + NA_KW)
    dc = np.clip(kcol[None, :] - qc[:, None] + NA_KW - 1, 0, 2 * NA_KW - 2)
    onehot = (dc[None] == np.arange(2 * NA_KW - 1)[:, None, None]).astype(np.float32)
    toep = jnp.einsum("hdc,cqk->hdqk", rpb.astype(F32), jnp.asarray(onehot), precision=lax.Precision.HIGHEST)
    toep = jnp.where(in_win[None, None], toep, -jnp.inf)
    per_off = [toep[:, NA_KH - 1 - off:2 * NA_KH - 1 - off] for off in range(n_off)]
    bias = jnp.stack(per_off, axis=0)
    return bias.transpose(0, 1, 3, 2, 4).reshape(n_off, NA_HEADS * GRID_W, kh * GRID_W)


def _na_attention(q, k, v, kc, vc, bias):
    b, t, w = q.shape
    tc = kc.shape[1]
    rows = t // GRID_W
    assert rows >= NA_KH and t % GRID_W == 0
    seq = lambda n: pl.BlockSpec((None, n, w), lambda i: (i, 0, 0))
    return pl.pallas_call(
        functools.partial(_na_kernel, rows=rows),
        grid=(b,),
        in_specs=[seq(t), seq(t), seq(t), seq(tc), seq(tc),
                  pl.BlockSpec(bias.shape, lambda i: (0, 0, 0))],
        out_specs=seq(t),
        out_shape=jax.ShapeDtypeStruct((b, t, w), BF16),
        compiler_params=_params(("parallel",)),
        name="na_attention",
    )(q, k, v, kc, vc, bias)


def _softmax_pv(scores, values):
    m = functools.reduce(jnp.maximum, [jnp.max(s, axis=-1, keepdims=True) for s in scores])
    ps = [jnp.exp(s - m) for s in scores]
    denom = functools.reduce(jnp.add, [jnp.sum(p, axis=-1, keepdims=True) for p in ps])
    o = functools.reduce(jnp.add, [_dot(p.astype(BF16), v) for p, v in zip(ps, values)])
    return o * (1.0 / denom)


def _attn_sliced_kernel(*refs, n_parts, heads, dh, dv):
    q_ref, o_ref = refs[0], refs[-1]
    for h in range(heads):
        q = q_ref[:, dh * h:dh * (h + 1)]
        scores = [_dot_nt(q, refs[1 + 2 * p][:, dh * h:dh * (h + 1)]) for p in range(n_parts)]
        values = [refs[2 + 2 * p][:, dv * h:dv * (h + 1)] for p in range(n_parts)]
        o_ref[:, dv * h:dv * (h + 1)] = _softmax_pv(scores, values).astype(o_ref.dtype)


def _attn_masked_kernel(q_ref, k_ref, v_ref, o_ref, *, width):
    n = q_ref.shape[0]
    q4 = _stack_heads(q_ref[...], width)
    o4 = _softmax_pv([_dot_nt(q4, k_ref[...])], [v_ref[...]])
    o_ref[...] = _unstack_heads(o4, n, width).astype(o_ref.dtype)


def _attention(q, parts, heads, dh, dv, tq):
    b, t, wq = q.shape
    tq = min(tq, t)
    flat = [a for kv in parts for a in kv]
    kv_spec = lambda a: pl.BlockSpec((None,) + a.shape[1:], lambda i, j: (i, 0, 0))
    return pl.pallas_call(
        functools.partial(_attn_sliced_kernel, n_parts=len(parts), heads=heads, dh=dh, dv=dv),
        grid=(b, t // tq),
        in_specs=[pl.BlockSpec((None, tq, wq), lambda i, j: (i, j, 0))] + [kv_spec(a) for a in flat],
        out_specs=pl.BlockSpec((None, tq, heads * dv), lambda i, j: (i, j, 0)),
        out_shape=jax.ShapeDtypeStruct((b, t, heads * dv), BF16),
        compiler_params=_params(("parallel", "parallel")),
        name="mla_attention_%d" % len(parts),
    )(q, *flat)


def _attention_masked(q, k, v, width):
    b, t, w = q.shape
    spec = lambda a: pl.BlockSpec((None,) + a.shape[1:], lambda i: (i, 0, 0))
    return pl.pallas_call(
        functools.partial(_attn_masked_kernel, width=width),
        grid=(b,),
        in_specs=[spec(q), spec(k), spec(v)],
        out_specs=spec(q),
        out_shape=jax.ShapeDtypeStruct((b, t, w), BF16),
        compiler_params=_params(("parallel",)),
        name="ctx_na_attention",
    )(q, k, v)


GLA_SAFE_DECAY = 60.0
GLA_EXACT_CHUNK = 16


def _gla_masks(c):
    ri = lax.broadcasted_iota(jnp.int32, (c, c), 0)
    ci = lax.broadcasted_iota(jnp.int32, (c, c), 1)
    ti = lax.broadcasted_iota(jnp.int32, (c, GLA_HEADS * c), 0)
    si = lax.broadcasted_iota(jnp.int32, (c, GLA_HEADS * c), 1) % c
    lk = lax.broadcasted_iota(jnp.int32, (c, GLA_KW), 1) // GLA_DK
    lv = lax.broadcasted_iota(jnp.int32, (c, GLA_WIDTH), 1) // GLA_DV
    head_v = lax.broadcasted_iota(jnp.int32, (GLA_WIDTH, GLA_KW), 0) // GLA_DV
    head_k = lax.broadcasted_iota(jnp.int32, (GLA_WIDTH, GLA_KW), 1) // GLA_DK
    return {
        "tri": ((ci <= ri).astype(BF16), (ci >= ri).astype(BF16)),
        "allowed": (si <= ti, si >= ti),
        "hm_k": [lk == h for h in range(GLA_HEADS)],
        "hm_v": [lv == h for h in range(GLA_HEADS)],
        "hm_s": head_v == head_k,
        "expand": (lax.broadcasted_iota(jnp.int32, (GLA_KW, GLA_WIDTH), 0) // GLA_DK
                   == lax.broadcasted_iota(jnp.int32, (GLA_KW, GLA_WIDTH), 1) // GLA_DV).astype(BF16),
    }


def _gla_intra_pairwise(q, k, v, bcum, expand, reverse):
    c = q.shape[0]
    row = lax.broadcasted_iota(jnp.int32, (c, GLA_WIDTH), 0)
    outs = []
    for t in range(c):
        w = jnp.exp(jnp.minimum(bcum[t:t + 1, :] - bcum, 0.0))
        a = _dot(((q[t:t + 1, :] * k) * w).astype(BF16), expand)
        keep = (row >= t) if reverse else (row <= t)
        outs.append(jnp.sum(jnp.where(keep, a * v, 0.0), axis=0, keepdims=True))
    return jnp.concatenate(outs, axis=0)


def _gla_chunk(q, k, v, g, st_ref, m, reverse, pairwise):
    c = q.shape[0]
    d = 1 if reverse else 0
    end_row = 0 if reverse else c - 1
    g_hi, g_lo = _split(g)
    tri = m["tri"][d]
    bcum = _dot(tri, g_hi) + _dot(tri, g_lo)
    bend = bcum[end_row:end_row + 1, :]
    qe = (q * jnp.exp(bcum)).astype(BF16)
    ke = (k * jnp.exp(bend - bcum)).astype(BF16)
    st = st_ref[...]
    o = _dot_nt(qe, st.astype(BF16))
    if pairwise:
        o = o + _gla_intra_pairwise(q, k, v, bcum, m["expand"], reverse)
    else:
        kt = k * jnp.exp(-bcum)
        kst = jnp.concatenate([jnp.where(hm, kt, 0.0) for hm in m["hm_k"]], axis=0).astype(BF16)
        a = jnp.where(m["allowed"][d], _dot_nt(qe, kst), 0.0)
        vbd = jnp.concatenate([jnp.where(hm, v, 0.0) for hm in m["hm_v"]], axis=0).astype(BF16)
        o = o + _dot(a.astype(BF16), vbd)
    upd = _dot_tn(v.astype(BF16), ke)
    st_ref[...] = st * jnp.exp(bend) + jnp.where(m["hm_s"], upd, 0.0)
    return o


def _gla_kernel(q_ref, k_ref, v_ref, gf_ref, gb_ref, go_ref,
                qc_ref, kc_ref, vc_ref, gfc_ref, gbc_ref, goc_ref, onorm_ref, seg_ref,
                *rest, ctx_out):
    if ctx_out:
        ox_ref, oc_ref, acc_ref, accc_ref, sf_ref, sb_ref = rest
    else:
        ox_ref, acc_ref, sf_ref, sb_ref = rest
        oc_ref = accc_ref = None
    t, tc = q_ref.shape[0], qc_ref.shape[0]

    def sweep(qr, kr, vr, gfr, gbr, dst, total, c, m, pairwise):
        count = total // c

        def body(j, carry):
            lo_f = pl.multiple_of(j * c, c)
            lo_b = pl.multiple_of((count - 1 - j) * c, c)
            of = _gla_chunk(qr[pl.ds(lo_f, c), :], kr[pl.ds(lo_f, c), :], vr[pl.ds(lo_f, c), :],
                            gfr[pl.ds(lo_f, c), :], sf_ref, m, False, pairwise)
            ob = _gla_chunk(qr[pl.ds(lo_b, c), :], kr[pl.ds(lo_b, c), :], vr[pl.ds(lo_b, c), :],
                            gbr[pl.ds(lo_b, c), :], sb_ref, m, True, pairwise)
            if dst is not None:
                dst[pl.ds(lo_f, c), :] += of
                dst[pl.ds(lo_b, c), :] += ob
            return carry
        lax.fori_loop(0, count, body, 0, unroll=1 if pairwise else 4)

    def scans(c, pairwise):
        m = _gla_masks(c)
        sf_ref[...] = jnp.zeros_like(sf_ref)
        sb_ref[...] = jnp.zeros_like(sb_ref)
        acc_ref[...] = jnp.zeros_like(acc_ref)
        if ctx_out:
            accc_ref[...] = jnp.zeros_like(accc_ref)
        sweep(qc_ref, kc_ref, vc_ref, gfc_ref, gbc_ref, accc_ref, tc, c, m, pairwise)
        sweep(q_ref, k_ref, v_ref, gf_ref, gb_ref, acc_ref, t, c, m, pairwise)

    def min_chunk_decay(gr, total):
        def body(j, low):
            lo = pl.multiple_of(j * GLA_CHUNK, GLA_CHUNK)
            return jnp.minimum(low, jnp.sum(gr[pl.ds(lo, GLA_CHUNK), :], axis=0, keepdims=True))
        return lax.fori_loop(0, total // GLA_CHUNK, body, jnp.zeros((1, GLA_KW), F32))

    low = functools.reduce(jnp.minimum, [min_chunk_decay(gf_ref, t), min_chunk_decay(gb_ref, t),
                                         min_chunk_decay(gfc_ref, tc), min_chunk_decay(gbc_ref, tc)])
    safe = jnp.min(low) >= -GLA_SAFE_DECAY
    pl.when(safe)(functools.partial(scans, GLA_CHUNK, False))
    pl.when(jnp.logical_not(safe))(functools.partial(scans, GLA_EXACT_CHUNK, True))

    def finish(acc, gate, out, total):
        tr = min(256, total)

        def body(i, carry):
            lo = pl.multiple_of(i * tr, tr)
            o = acc[pl.ds(lo, tr), :]
            ms = _dot((o * o).astype(BF16), seg_ref[...]) * (1.0 / GLA_DV)
            gt = gate[pl.ds(lo, tr), :]
            y = o * lax.rsqrt(ms + EPS) * onorm_ref[...] * (gt * jax.nn.sigmoid(gt))
            out[pl.ds(lo, tr), :] = y.astype(out.dtype)
            return carry
        lax.fori_loop(0, total // tr, body, 0)

    finish(acc_ref, go_ref, ox_ref, t)
    if ctx_out:
        finish(accc_ref, goc_ref, oc_ref, tc)


def _gla(lat, ctx, onorm, seg64, ctx_out):
    b, t, _ = lat[0].shape
    tc = ctx[0].shape[1]
    spec = lambda a: pl.BlockSpec((None,) + a.shape[1:], lambda i: (i, 0, 0))
    full = lambda a: pl.BlockSpec(a.shape, lambda i: (0,) * a.ndim)
    out_shape = [jax.ShapeDtypeStruct((b, t, GLA_WIDTH), BF16)]
    scratch = [pltpu.VMEM((t, GLA_WIDTH), F32)]
    if ctx_out:
        out_shape.append(jax.ShapeDtypeStruct((b, tc, GLA_WIDTH), BF16))
        scratch.append(pltpu.VMEM((tc, GLA_WIDTH), F32))
    scratch += [pltpu.VMEM((GLA_WIDTH, GLA_KW), F32)] * 2
    outs = pl.pallas_call(
        functools.partial(_gla_kernel, ctx_out=ctx_out),
        grid=(b,),
        in_specs=[spec(a) for a in lat] + [spec(a) for a in ctx] + [full(onorm), full(seg64)],
        out_specs=[spec(s) for s in out_shape],
        out_shape=out_shape,
        scratch_shapes=scratch,
        compiler_params=_params(("parallel",)),
        name="gla_ctx_out" if ctx_out else "gla_last",
    )(*lat, *ctx, onorm, seg64)
    return outs if ctx_out else (outs[0], None)


def _outproj_kernel(a_ref, b_ref, c_ref, x_ref, gt_ref, sc_ref, sh_ref, gffn_ref,
                    wa_ref, wb_ref, wc_ref, wrh_ref, wrl_ref, br_ref, xo_ref, h_ref, lg_ref):
    mix = _dot(a_ref[...], wa_ref[...]) + _dot(b_ref[...], wb_ref[...]) + _dot(c_ref[...], wc_ref[...])
    xn = x_ref[...] + gt_ref[...] * mix
    xo_ref[...] = xn
    h = _rms(xn) * gffn_ref[...]
    h = h * (1.0 + sc_ref[...]) + sh_ref[...]
    _store_row_tiled(h_ref, h)
    h_hi, h_lo = _split(h)
    wrh = wrh_ref[...]
    lg_ref[...] = _dot(h_hi, wrh) + _dot(h_lo, wrh) + _dot(h_hi, wrl_ref[...]) + br_ref[...]


def _outproj(a, bm, c, x, gt, sc, sh, lw):
    b, t, d = x.shape
    tm = min(512, t)
    row = lambda w: pl.BlockSpec((None, tm, w), lambda i, j: (i, j, 0))
    vec = pl.BlockSpec((None, 1, d), lambda i, j: (i, 0, 0))
    full = lambda arr: pl.BlockSpec(arr.shape, lambda i, j: (0,) * arr.ndim)
    consts = [lw["gffn"], lw["wo_a"], lw["wo_b"], lw["wo_c"], lw["wr_hi"], lw["wr_lo"], lw["br"]]
    return pl.pallas_call(
        _outproj_kernel,
        grid=(b, t // tm),
        in_specs=[row(NA_WIDTH), row(MLA_WIDTH), row(GLA_WIDTH), row(d), vec, vec, vec] + [full(w) for w in consts],
        out_specs=[row(d), pl.BlockSpec((None, tm * V7X_SUBLANES, V7X_LANES), lambda i, j: (i, j, 0)), row(V7X_LANES)],
        out_shape=[jax.ShapeDtypeStruct((b, t, d), F32),
                   jax.ShapeDtypeStruct((b, t * V7X_SUBLANES, V7X_LANES), F32),
                   jax.ShapeDtypeStruct((b, t, V7X_LANES), F32)],
        compiler_params=_params(("parallel", "parallel")),
        name="outproj_router",
    )(a, bm, c, x, gt, sc, sh, *consts)


def _moe_kernel(be_ref, tok_hbm, pair_hbm, h_hbm, w1_ref, b1_ref, w2_ref, b2_ref, y_hbm,
                g0, g1, g2, s0, s1, s2, x0, x1, x2, y0, y1, y2, w1b_ref, w2b_ref, gisem, sisem, gsem, ssem):
    i = pl.program_id(0)
    last = pl.num_programs(0) - 1
    sub = V7X_SUBLANES
    blk = x0.shape[0] // sub
    f = w2_ref.shape[0]
    gbuf, sbuf, xbuf, ybuf = (g0, g1, g2), (s0, s1, s2), (x0, x1, x2), (y0, y1, y2)

    def load_tok(t, s):
        return pltpu.make_async_copy(tok_hbm.at[pl.ds(pl.multiple_of(t * blk, blk), blk)], gbuf[s], gisem.at[s])

    def load_pair(t, s):
        return pltpu.make_async_copy(pair_hbm.at[pl.ds(pl.multiple_of(t * blk, blk), blk)], sbuf[s], sisem.at[s])

    def tile(ref, first_row):
        if not isinstance(first_row, int):
            first_row = pl.multiple_of(first_row, sub)
        return ref.at[pl.ds(first_row, sub)]

    def start_gathers(s):
        for r in range(blk):
            pltpu.make_async_copy(tile(h_hbm, gbuf[s][r]), tile(xbuf[s], r * sub), gsem.at[s]).start(
                priority=r % DMA_QUEUES)

    def wait_gathers(s):
        for r in range(blk):
            pltpu.make_async_copy(tile(h_hbm, 0), tile(xbuf[s], r * sub), gsem.at[s]).wait()

    def start_scatters(s):
        for r in range(blk):
            pltpu.make_async_copy(tile(ybuf[s], r * sub), tile(y_hbm, sbuf[s][r]), ssem.at[s]).start(
                priority=r % DMA_QUEUES)

    def wait_scatters(s):
        for r in range(blk):
            pltpu.make_async_copy(tile(ybuf[s], r * sub), tile(y_hbm, 0), ssem.at[s]).wait()

    @pl.when(i == 0)
    def _():
        for yb in ybuf:
            yb[...] = jnp.zeros_like(yb)
        first_spare = y_hbm.shape[0] - MOE_RING * blk * sub
        for s in range(MOE_RING - 1):
            for r in range(blk):
                pltpu.make_async_copy(tile(ybuf[s], r * sub), tile(y_hbm, first_spare + (s * blk + r) * sub),
                                      ssem.at[s]).start()
        load_tok(1, 0).start()
        load_tok(2, 1).start()
        load_tok(3, 2).start()
        load_pair(0, 2).start()
        load_tok(1, 0).wait()
        start_gathers(0)
        load_tok(2, 1).wait()
        start_gathers(1)

    e = be_ref[i]
    prev = be_ref[jnp.maximum(i - 1, 0)]

    @pl.when((i == 0) | (e != prev))
    def _():
        rows = 128

        def cast1(j, carry):
            lo = pl.multiple_of(j * rows, rows)
            w1b_ref[pl.ds(lo, rows), :] = w1_ref[pl.ds(lo, rows), :].astype(BF16)
            return carry

        def cast2(j, carry):
            lo = pl.multiple_of(j * rows, rows)
            w2b_ref[pl.ds(lo, rows), :] = w2_ref[pl.ds(lo, rows), :].astype(BF16)
            return carry
        lax.fori_loop(0, w1_ref.shape[0] // rows, cast1, 0)
        lax.fori_loop(0, w2_ref.shape[0] // rows, cast2, 0)

    def step(a):
        b, z = (a + 1) % MOE_RING, (a + 2) % MOE_RING

        load_tok(i + 3, z).wait()
        load_pair(i, z).wait()
        load_tok(i + 4, a).start()
        load_pair(i + 1, a).start()
        wait_gathers(a)
        wait_scatters(a)
        start_gathers(z)
        start_scatters(z)

        hb = _dot(_load_row_tiled(xbuf[a]).astype(BF16), w1b_ref[...]) + b1_ref[...]
        glu = jnp.minimum(hb[:, :f], SWIGLU_LIMIT)
        lin = jnp.clip(hb[:, f:], -SWIGLU_LIMIT, SWIGLU_LIMIT)
        act = (lin + 1.0) * (glu * jax.nn.sigmoid(SWIGLU_ALPHA * glu))
        _store_row_tiled(ybuf[a], _dot(act.astype(BF16), w2b_ref[...]) + b2_ref[...])

        @pl.when(i == last)
        def _():
            load_tok(i + 4, a).wait()
            load_pair(i + 1, a).wait()
            wait_gathers(b)
            wait_gathers(z)
            wait_scatters(b)
            wait_scatters(z)
            start_scatters(a)
            wait_scatters(a)

    for a in range(MOE_RING):
        pl.when(i % MOE_RING == a)(functools.partial(step, a))


def _moe_blocks(h, tok_tab, pair_tab, blk_e, n_rows_out, w1, b1, w2, b2, layer):
    d = V7X_SUBLANES * V7X_LANES
    n_blk = blk_e.shape[0]
    f2 = w1.shape[-1]
    f = w2.shape[-2]
    grid_spec = pltpu.PrefetchScalarGridSpec(
        num_scalar_prefetch=1,
        grid=(n_blk,),
        in_specs=[
            pl.BlockSpec(memory_space=pl.ANY),
            pl.BlockSpec(memory_space=pl.ANY),
            pl.BlockSpec(memory_space=pl.ANY),
            pl.BlockSpec((None, None, d, f2), lambda i, be: (layer, be[i], 0, 0)),
            pl.BlockSpec((None, None, 1, f2), lambda i, be: (layer, be[i], 0, 0)),
            pl.BlockSpec((None, None, f, d), lambda i, be: (layer, be[i], 0, 0)),
            pl.BlockSpec((None, None, 1, d), lambda i, be: (layer, be[i], 0, 0)),
        ],
        out_specs=pl.BlockSpec(memory_space=pl.ANY),
        scratch_shapes=[pltpu.SMEM((MOE_BLOCK,), jnp.int32)] * (2 * MOE_RING)
        + [pltpu.VMEM((MOE_BLOCK * V7X_SUBLANES, V7X_LANES), F32)] * (2 * MOE_RING) + [
            pltpu.VMEM((d, f2), BF16),
            pltpu.VMEM((f, d), BF16),
        ] + [pltpu.SemaphoreType.DMA((MOE_RING,))] * 4,
    )
    return pl.pallas_call(
        _moe_kernel,
        grid_spec=grid_spec,
        out_shape=jax.ShapeDtypeStruct((n_rows_out * V7X_SUBLANES, V7X_LANES), F32),
        compiler_params=_params(("arbitrary",), mib=56),
        name="moe_experts",
    )(blk_e, tok_tab, pair_tab, h, w1, b1.reshape(b1.shape[0], b1.shape[1], 1, f2), w2,
      b2.reshape(b2.shape[0], b2.shape[1], 1, d))


def _moe(h, logits, w1, b1, w2, b2, layer):
    n = logits.shape[0]
    nk = n * TOP_K
    top_val, top_idx = lax.top_k(logits, TOP_K)
    gates = jax.nn.softmax(top_val, axis=-1)
    flat_e = top_idx.reshape(-1)
    order = jnp.argsort(flat_e).astype(jnp.int32)
    counts = jnp.bincount(flat_e, length=N_EXPERTS).astype(jnp.int32)
    padded = (counts + MOE_BLOCK - 1) // MOE_BLOCK * MOE_BLOCK
    start = jnp.cumsum(counts) - counts
    pend = jnp.cumsum(padded)
    pstart = pend - padded
    n_blk = -(-nk // MOE_BLOCK) + N_EXPERTS
    blk_id = jnp.arange(-1, n_blk + 3, dtype=jnp.int32)
    blk_lo = blk_id * MOE_BLOCK
    blk_e = jnp.minimum(jnp.sum((pend[None, :] <= blk_lo[:, None]).astype(jnp.int32), axis=1), N_EXPERTS - 1)
    s0 = start[blk_e] + blk_lo - pstart[blk_e]
    left = jnp.where(blk_id >= 0, start[blk_e] + counts[blk_e] - s0, 0)
    r = jnp.arange(MOE_BLOCK, dtype=jnp.int32)[None, :]
    valid = r < left[:, None]
    win = order[jnp.clip(s0[:, None] + r, 0, nk - 1)]
    tok = lax.shift_right_logical(win, 2)
    dst = (win & (TOP_K - 1)) * n + tok
    dst_tab = jnp.where(valid, dst, nk + (blk_id[:, None] % MOE_RING) * MOE_BLOCK + r).reshape(-1)
    tok_tab = jnp.where(valid, tok, 0).reshape(-1)
    y4 = _moe_blocks(h, tok_tab * V7X_SUBLANES, dst_tab * V7X_SUBLANES, blk_e[1:n_blk + 1],
                     nk + MOE_RING * MOE_BLOCK, w1, b1, w2, b2, layer)
    return y4, gates


def _combine_kernel(x_ref, gt_ref, g_ref, y0_ref, y1_ref, y2_ref, y3_ref, o_ref):
    g = g_ref[...]
    mix = g[:, 0:1] * _load_row_tiled(y0_ref)
    for k, y_ref in enumerate((y1_ref, y2_ref, y3_ref), start=1):
        mix = mix + g[:, k:k + 1] * _load_row_tiled(y_ref)
    o_ref[...] = x_ref[...] + gt_ref[...] * mix


def _combine(x, gt, gates, y4, n_tok, first_tok):
    b, t, d = x.shape
    tm = int(min(512, np.gcd.reduce([t, n_tok, first_tok or n_tok])))
    per_b = t // tm
    plane = n_tok // tm
    first = first_tok // tm
    row = pl.BlockSpec((None, tm, d), lambda i, j: (i, j, 0))
    y_spec = lambda k: pl.BlockSpec((tm * V7X_SUBLANES, V7X_LANES),
                                    lambda i, j: (k * plane + first + i * per_b + j, 0))
    return pl.pallas_call(
        _combine_kernel,
        grid=(b, per_b),
        in_specs=[row, pl.BlockSpec((None, 1, d), lambda i, j: (i, 0, 0)),
                  pl.BlockSpec((None, tm, TOP_K), lambda i, j: (i, j, 0))] + [y_spec(k) for k in range(TOP_K)],
        out_specs=row,
        out_shape=jax.ShapeDtypeStruct((b, t, d), F32),
        compiler_params=_params(("parallel", "parallel")),
        name="moe_combine",
    )(x, gt, gates, y4, y4, y4, y4)


def _rope_tables(t):
    tok = jnp.arange(t)
    row = (tok // GRID_W).astype(F32)[:, None]
    col = (tok % GRID_W).astype(F32)[:, None]
    inv = 1.0 / (ROPE_THETA ** (jnp.arange(0, ROPE_AXIS, 2, dtype=F32) / ROPE_AXIS))
    ang = jnp.concatenate([row * inv, row * inv, col * inv, col * inv], axis=-1)
    cos = jnp.ones((t, V7X_LANES), F32).at[:, MLA_NOPE:MLA_QK].set(jnp.cos(ang))
    sin = jnp.zeros((t, V7X_LANES), F32).at[:, MLA_NOPE:MLA_QK].set(jnp.sin(ang))
    return cos, sin


def _pad_heads(w, per_head, offset=0):
    lead = w.shape[:-1]
    w = w.reshape(lead + (MLA_HEADS, per_head))
    out = jnp.zeros(lead + (MLA_HEADS, V7X_LANES), w.dtype).at[..., offset:offset + per_head].set(w)
    return out.reshape(lead + (MLA_PAD,))


def _layer_weights(l, g_mix, w_in, na_q_norm, na_k_norm, mla_q_a_norm, mla_w_q_b, mla_kv_a_norm, mla_w_kv_b,
                   mla_q_norm, mla_k_norm, gla_w_gate, gla_b_gate, gla_o_norm, w_out, g_ffn, w_router, b_router):
    d = w_in.shape[1]
    sizes = (NA_WIDTH, NA_WIDTH, NA_WIDTH, MLA_Q_RANK, MLA_KV_RANK, MLA_ROPE, GLA_KW, GLA_KW, GLA_WIDTH, GLA_WIDTH,
             GLA_GATE_RANK, GLA_GATE_RANK)
    (wq, wk, wv, wcq, wckv, wkr, wgq, wgk, wgv, wgo, wlf, wlb) = jnp.split(w_in[l], np.cumsum(sizes)[:-1].tolist(), axis=-1)
    small = jnp.zeros((d, V7X_LANES), F32)
    small = small.at[:, S_LF:S_LF + GLA_GATE_RANK].set(wlf).at[:, S_LB:S_LB + GLA_GATE_RANK].set(wlb)
    small = small.at[:, S_KR:S_KR + MLA_ROPE].set(wkr)
    w_packed = jnp.concatenate([wq, wk, wv, wcq, wckv, wgq, wgk, wgv, wgo, small], axis=-1).astype(BF16)
    kvb = mla_w_kv_b[l].reshape(MLA_KV_RANK, MLA_HEADS, MLA_NOPE + MLA_V)
    wg = jnp.zeros((V7X_LANES, 2 * GLA_KW), F32)
    wg = wg.at[S_LF:S_LF + GLA_GATE_RANK, :GLA_KW].set(gla_w_gate[l, 0])
    wg = wg.at[S_LB:S_LB + GLA_GATE_RANK, GLA_KW:].set(gla_w_gate[l, 1])
    wg_hi, wg_lo = _split(wg)
    seg = (np.arange(NA_WIDTH)[:, None] // NA_DIM == np.arange(NA_WIDTH)[None, :] // NA_DIM)
    wr = jnp.zeros((d, V7X_LANES), F32).at[:, :N_EXPERTS].set(w_router[l])
    wr_hi, wr_lo = _split(wr)
    wo = w_out[l].astype(BF16)
    return {
        "gmix": g_mix[l][None, :],
        "w_in": w_packed,
        "naqn": jnp.tile(na_q_norm[l], NA_HEADS)[None, :],
        "nakn": jnp.tile(na_k_norm[l], NA_HEADS)[None, :],
        "qan": mla_q_a_norm[l][None, :],
        "wqb": _pad_heads(mla_w_q_b[l], MLA_QK).astype(BF16),
        "kvan": mla_kv_a_norm[l][None, :],
        "wkk": _pad_heads(kvb[:, :, :MLA_NOPE].reshape(MLA_KV_RANK, -1), MLA_NOPE).astype(BF16),
        "wkv": kvb[:, :, MLA_NOPE:].reshape(MLA_KV_RANK, MLA_WIDTH).astype(BF16),
        "mqn": _pad_heads(jnp.tile(mla_q_norm[l], MLA_HEADS), MLA_QK)[None, :],
        "mkn": _pad_heads(jnp.tile(mla_k_norm[l], MLA_HEADS), MLA_QK)[None, :],
        "wg_hi": wg_hi, "wg_lo": wg_lo,
        "bg": jnp.concatenate([gla_b_gate[l, 0], gla_b_gate[l, 1]])[None, :],
        "seg64": jnp.asarray(seg, BF16),
        "onorm": jnp.tile(gla_o_norm[l], GLA_HEADS)[None, :],
        "gffn": g_ffn[l][None, :],
        "wo_a": wo[:NA_WIDTH], "wo_b": wo[NA_WIDTH:NA_WIDTH + MLA_WIDTH], "wo_c": wo[NA_WIDTH + MLA_WIDTH:],
        "wr_hi": wr_hi, "wr_lo": wr_lo,
        "br": jnp.zeros((1, V7X_LANES), F32).at[0, :N_EXPERTS].set(b_router[l]),
    }


def kernel(x, c, ctx, c_ctx, w_ada, b_ada, g_mix, w_in, na_q_norm, na_k_norm, na_rpb, mla_q_a_norm, mla_w_q_b,
           mla_kv_a_norm, mla_w_kv_b, mla_q_norm, mla_k_norm, gla_w_gate, gla_b_gate, gla_o_norm, w_out, g_ffn,
           w_router, b_router, w_moe1, b_moe1, w_moe2, b_moe2):
    bsz, t, d = x.shape
    assert d == V7X_SUBLANES * V7X_LANES
    tc = ctx.shape[1]
    depth = w_ada.shape[0]
    rows = t // GRID_W
    cos, sin = _rope_tables(t)

    pad = (-(bsz + 1)) % 8
    cvec = jnp.concatenate([c, c_ctx[None, :], jnp.zeros((pad, d), F32)], axis=0)
    mod = _ada(cvec, w_ada, b_ada)

    h_ctx = ctx
    for l in range(depth):
        ctx_out = l < depth - 1
        lw = _layer_weights(l, g_mix, w_in, na_q_norm, na_k_norm, mla_q_a_norm, mla_w_q_b, mla_kv_a_norm,
                            mla_w_kv_b, mla_q_norm, mla_k_norm, gla_w_gate, gla_b_gate, gla_o_norm, w_out, g_ffn,
                            w_router, b_router)
        m_lat = [m[:, None, :] for m in jnp.split(mod[l, :bsz], 6, axis=-1)]
        m_ctx = [jnp.broadcast_to(m[None, :, :], (bsz, 1, d)) for m in jnp.split(mod[l, bsz:bsz + 1], 6, axis=-1)]
        sh1, sc1, gt1, sh2, sc2, gt2 = m_lat
        csh1, csc1, cgt1, csh2, csc2, cgt2 = m_ctx

        (naq, nak, nav, mq, mk, mv, gq, gk, gv, gout, gf, gb) = _inproj(x, sc1, sh1, lw, cos, sin, True)
        (cnaq, cnak, cnav, cmq, cmk, cmv, cgq, cgk, cgv, cgout, cgf, cgb) = _inproj(
            h_ctx, csc1, csh1, lw, cos[:tc], sin[:tc], False)

        bias = _na_bias_table(na_rpb[l], rows)
        a_x = _na_attention(naq, nak, nav, cnak, cnav, bias)
        b_x = _attention(mq, [(mk, mv), (cmk, cmv)], MLA_HEADS, V7X_LANES, MLA_V, 256)
        c_x, c_c = _gla((gq, gk, gv, gf, gb, gout), (cgq, cgk, cgv, cgf, cgb, cgout), lw["onorm"], lw["seg64"], ctx_out)
        x_new, h2, logits = _outproj(a_x, b_x, c_x, x, gt1, sc2, sh2, lw)

        toks = [h2.reshape(bsz * t * V7X_SUBLANES, V7X_LANES)]
        lgs = [logits.reshape(bsz * t, V7X_LANES)]
        if ctx_out:
            a_c = _attention_masked(cnaq, cnak, cnav, NA_DIM)
            b_c = _attention(cmq, [(cmk, cmv)], MLA_HEADS, V7X_LANES, MLA_V, 256)
            hc_new, hc2, clogits = _outproj(a_c, b_c, c_c, h_ctx, cgt1, csc2, csh2, lw)
            toks.append(hc2.reshape(bsz * tc * V7X_SUBLANES, V7X_LANES))
            lgs.append(clogits.reshape(bsz * tc, V7X_LANES))
        n_tok = sum(a.shape[0] for a in lgs)
        y4, gates = _moe(jnp.concatenate(toks, axis=0), jnp.concatenate(lgs, axis=0)[:, :N_EXPERTS],
                         w_moe1, b_moe1, w_moe2, b_moe2, l)
        x = _combine(x_new, gt2, gates[:bsz * t].reshape(bsz, t, TOP_K), y4, n_tok, 0)
        if ctx_out:
            h_ctx = _combine(hc_new, cgt2, gates[bsz * t:].reshape(bsz, tc, TOP_K), y4, n_tok, bsz * t)
    return x
```

```python
import functools

import numpy as np
import jax
import jax.numpy as jnp
from jax import lax
from jax.experimental import pallas as pl
from jax.experimental.pallas import tpu as pltpu

F32 = jnp.float32
BF16 = jnp.bfloat16

V7X_LANES = 128
V7X_VMEM_BYTES = 64 * 1024 * 1024
DMA_QUEUES = 2

EPS = 1e-6
GRID_W = 64
NA_HEADS, NA_DIM, NA_KH, NA_KW = 4, 64, 8, 16
MLA_HEADS, MLA_NOPE, MLA_ROPE, MLA_V = 4, 64, 32, 128
MLA_QK = MLA_NOPE + MLA_ROPE
MLA_Q_RANK, MLA_KV_RANK = 256, 128
ROPE_AXIS = MLA_ROPE // 2
ROPE_THETA = 10000.0
GLA_HEADS, GLA_DK, GLA_DV = 4, 32, 64
GLA_GATE_RANK, GLA_GATE_NORM, GLA_CHUNK = 16, 16.0, 64
NA_WIDTH = NA_HEADS * NA_DIM
MLA_WIDTH = MLA_HEADS * MLA_V
MLA_PAD = MLA_HEADS * V7X_LANES
GLA_KW = GLA_HEADS * GLA_DK
GLA_WIDTH = GLA_HEADS * GLA_DV
N_EXPERTS, TOP_K = 32, 4
SWIGLU_LIMIT, SWIGLU_ALPHA = 7.0, 1.702
MOE_BLOCK = 512
MOE_RING = 3

C_NAQ, C_NAK, C_NAV, C_CQ, C_CKV = 0, 256, 512, 768, 1024
C_GQ, C_GK, C_GV, C_GOUT, C_SMALL = 1152, 1280, 1408, 1664, 1920
IN_PACKED = 2048
S_LF, S_LB, S_KR = 0, 16, 64


def _vmem_limit(mib):
    return min(mib * 1024 * 1024, V7X_VMEM_BYTES - 4 * 1024 * 1024)


def _params(sem, mib=48):
    return pltpu.CompilerParams(dimension_semantics=sem, vmem_limit_bytes=_vmem_limit(mib))


def _dot(a, b):
    return jnp.dot(a, b, preferred_element_type=F32)


def _dot_nt(a, b):
    return lax.dot_general(a, b, (((1,), (1,)), ((), ())), preferred_element_type=F32)


def _dot_tn(a, b):
    return lax.dot_general(a, b, (((0,), (0,)), ((), ())), preferred_element_type=F32)


def _split(x):
    hi = x.astype(BF16)
    lo = (x - hi.astype(F32)).astype(BF16)
    return hi, lo


def _rms(x):
    return x * lax.rsqrt(jnp.mean(x * x, axis=-1, keepdims=True) + EPS)


V7X_SUBLANES = 8


def _store_row_tiled(ref, x):
    n = x.shape[0]
    for j in range(V7X_SUBLANES):
        ref[pl.ds(j, n, stride=V7X_SUBLANES), :] = x[:, V7X_LANES * j:V7X_LANES * (j + 1)]


def _load_row_tiled(ref):
    n = ref.shape[0] // V7X_SUBLANES
    return jnp.concatenate([ref[pl.ds(j, n, stride=V7X_SUBLANES), :] for j in range(V7X_SUBLANES)], axis=1)


def _ada_kernel(c_ref, w_ref, b_ref, o_ref):
    c = c_ref[...]
    s = (c * jax.nn.sigmoid(c)).astype(BF16)
    o_ref[...] = _dot(s, w_ref[...].astype(BF16)) + b_ref[...]


def _ada(cvec, w_ada, b_ada):
    depth, d, n6 = w_ada.shape
    r = cvec.shape[0]
    tn = 1024
    return pl.pallas_call(
        _ada_kernel,
        grid=(depth, n6 // tn),
        in_specs=[
            pl.BlockSpec((r, d), lambda l, j: (0, 0)),
            pl.BlockSpec((None, d, tn), lambda l, j: (l, 0, j)),
            pl.BlockSpec((None, 1, tn), lambda l, j: (l, 0, j)),
        ],
        out_specs=pl.BlockSpec((None, r, tn), lambda l, j: (l, 0, j)),
        out_shape=jax.ShapeDtypeStruct((depth, r, n6), F32),
        compiler_params=_params(("parallel", "parallel")),
        name="ada_modulation",
    )(cvec, w_ada, b_ada.reshape(depth, 1, n6))


def _rope(x, cos, sin):
    lane = lax.broadcasted_iota(jnp.int32, (x.shape[0], V7X_LANES), 1)
    first = (lane & (ROPE_AXIS - 1)) < (ROPE_AXIS // 2)
    outs = []
    for h in range(MLA_HEADS):
        xs = x[:, V7X_LANES * h:V7X_LANES * (h + 1)]
        rot = jnp.where(first, -pltpu.roll(xs, V7X_LANES - ROPE_AXIS // 2, 1), pltpu.roll(xs, ROPE_AXIS // 2, 1))
        outs.append(xs * cos + rot * sin)
    return jnp.concatenate(outs, axis=1)


def _head_rms_padded(x, n_real):
    outs = []
    for h in range(MLA_HEADS):
        xs = x[:, V7X_LANES * h:V7X_LANES * (h + 1)]
        ms = jnp.sum(xs * xs, axis=-1, keepdims=True) * (1.0 / n_real)
        outs.append(xs * lax.rsqrt(ms + EPS))
    return jnp.concatenate(outs, axis=1)


def _log_sigmoid(x):
    return jnp.minimum(x, 0.0) - jnp.log1p(jnp.exp(-jnp.abs(x)))


def _inproj_kernel(x_ref, sc_ref, sh_ref, gmix_ref, w_ref, cos_ref, sin_ref,
                   naqn_ref, nakn_ref, qan_ref, wqb_ref, kvan_ref, wkk_ref, wkv_ref, mqn_ref, mkn_ref,
                   wgh_ref, wgl_ref, bg_ref, seg_ref,
                   naq_o, nak_o, nav_o, mq_o, mk_o, mv_o, gq_o, gk_o, gv_o, gout_o, gf_o, gb_o, *, use_rope):
    x = x_ref[...]
    h = _rms(x) * gmix_ref[...]
    h = h * (1.0 + sc_ref[...]) + sh_ref[...]
    p = _dot(h.astype(BF16), w_ref[...])

    seg = seg_ref[...]
    q = p[:, C_NAQ:C_NAQ + NA_WIDTH]
    k = p[:, C_NAK:C_NAK + NA_WIDTH]
    qss = _dot((q * q).astype(BF16), seg) * (1.0 / NA_DIM)
    kss = _dot((k * k).astype(BF16), seg) * (1.0 / NA_DIM)
    naq_o[...] = (q * lax.rsqrt(qss + EPS) * naqn_ref[...] * (NA_DIM ** -0.5)).astype(BF16)
    nak_o[...] = (k * lax.rsqrt(kss + EPS) * nakn_ref[...]).astype(BF16)
    nav_o[...] = p[:, C_NAV:C_NAV + NA_WIDTH].astype(BF16)

    small = p[:, C_SMALL:C_SMALL + V7X_LANES]
    cq = _rms(p[:, C_CQ:C_CQ + MLA_Q_RANK]) * qan_ref[...]
    mq = _head_rms_padded(_dot(cq.astype(BF16), wqb_ref[...]), MLA_QK) * mqn_ref[...]
    ckv = (_rms(p[:, C_CKV:C_CKV + MLA_KV_RANK]) * kvan_ref[...]).astype(BF16)
    lane = lax.broadcasted_iota(jnp.int32, small.shape, 1)
    kr = jnp.where((lane >= S_KR) & (lane < S_KR + MLA_ROPE), small, 0.0)
    mk = _dot(ckv, wkk_ref[...]) + jnp.concatenate([kr] * MLA_HEADS, axis=1)
    mk = _head_rms_padded(mk, MLA_QK) * mkn_ref[...]
    if use_rope:
        cos, sin = cos_ref[...], sin_ref[...]
        mq = _rope(mq, cos, sin)
        mk = _rope(mk, cos, sin)
    mq_o[...] = (mq * (MLA_QK ** -0.5)).astype(BF16)
    mk_o[...] = mk.astype(BF16)
    mv_o[...] = _dot(ckv, wkv_ref[...]).astype(BF16)

    gq_o[...] = p[:, C_GQ:C_GQ + GLA_KW] * (GLA_DK ** -0.5)
    gk_o[...] = p[:, C_GK:C_GK + GLA_KW]
    gv_o[...] = p[:, C_GV:C_GV + GLA_WIDTH]
    gout_o[...] = p[:, C_GOUT:C_GOUT + GLA_WIDTH]
    s_hi, s_lo = _split(small)
    wgh = wgh_ref[...]
    pre = _dot(s_hi, wgh) + _dot(s_lo, wgh) + _dot(s_hi, wgl_ref[...]) + bg_ref[...]
    ls = _log_sigmoid(pre) * (1.0 / GLA_GATE_NORM)
    gf_o[...] = ls[:, :GLA_KW]
    gb_o[...] = ls[:, GLA_KW:]


def _inproj(x, sc, sh, lw, cos, sin, use_rope):
    b, t, d = x.shape
    tm = min(512, t)
    full = lambda a: pl.BlockSpec(a.shape, lambda i, j: (0,) * a.ndim)
    row = lambda w: pl.BlockSpec((None, tm, w), lambda i, j: (i, j, 0))
    consts = [lw["gmix"], lw["w_in"]]
    tail = [lw["naqn"], lw["nakn"], lw["qan"], lw["wqb"], lw["kvan"], lw["wkk"], lw["wkv"], lw["mqn"], lw["mkn"],
            lw["wg_hi"], lw["wg_lo"], lw["bg"], lw["seg64"]]
    widths = [(NA_WIDTH, BF16)] * 3 + [(MLA_PAD, BF16), (MLA_PAD, BF16), (MLA_WIDTH, BF16),
                                       (GLA_KW, F32), (GLA_KW, F32), (GLA_WIDTH, F32), (GLA_WIDTH, F32),
                                       (GLA_KW, F32), (GLA_KW, F32)]
    return pl.pallas_call(
        functools.partial(_inproj_kernel, use_rope=use_rope),
        grid=(b, t // tm),
        in_specs=[row(d),
                  pl.BlockSpec((None, 1, d), lambda i, j: (i, 0, 0)),
                  pl.BlockSpec((None, 1, d), lambda i, j: (i, 0, 0))]
                 + [full(a) for a in consts]
                 + [pl.BlockSpec((tm, V7X_LANES), lambda i, j: (j, 0))] * 2
                 + [full(a) for a in tail],
        out_specs=[row(w) for w, _ in widths],
        out_shape=[jax.ShapeDtypeStruct((b, t, w), dt) for w, dt in widths],
        compiler_params=_params(("parallel", "parallel")),
        name="inproj_rope" if use_rope else "inproj_ctx",
    )(x, sc, sh, *consts, cos, sin, *tail)


def _stack_heads(q, width):
    lane = lax.broadcasted_iota(jnp.int32, q.shape, 1)
    zero = jnp.zeros_like(q)
    return jnp.concatenate([jnp.where(lane // width == h, q, zero) for h in range(NA_HEADS)], axis=0)


def _unstack_heads(o4, n, width):
    lane = lax.broadcasted_iota(jnp.int32, (n, o4.shape[1]), 1)
    out = jnp.zeros((n, o4.shape[1]), F32)
    for h in range(NA_HEADS):
        out = out + jnp.where(lane // width == h, o4[h * n:(h + 1) * n], 0.0)
    return out


def _na_kernel(q_ref, k_ref, v_ref, kc_ref, vc_ref, bias_ref, o_ref, *, rows):
    kc = kc_ref[...]
    vc = vc_ref[...]
    n_loc = NA_KH * GRID_W

    def body(r, carry):
        rs = jnp.clip(r - NA_KH // 2, 0, rows - NA_KH)
        off = r - rs
        q = q_ref[pl.ds(pl.multiple_of(r * GRID_W, GRID_W), GRID_W), :]
        ks = k_ref[pl.ds(pl.multiple_of(rs * GRID_W, GRID_W), n_loc), :]
        vs = v_ref[pl.ds(pl.multiple_of(rs * GRID_W, GRID_W), n_loc), :]
        q4 = _stack_heads(q, NA_DIM)
        s_loc = _dot_nt(q4, ks) + bias_ref[off]
        s_ctx = _dot_nt(q4, kc)
        m = jnp.maximum(jnp.max(s_loc, axis=-1, keepdims=True), jnp.max(s_ctx, axis=-1, keepdims=True))
        p_loc = jnp.exp(s_loc - m)
        p_ctx = jnp.exp(s_ctx - m)
        denom = jnp.sum(p_loc, axis=-1, keepdims=True) + jnp.sum(p_ctx, axis=-1, keepdims=True)
        o4 = (_dot(p_loc.astype(BF16), vs) + _dot(p_ctx.astype(BF16), vc)) * (1.0 / denom)
        o_ref[pl.ds(pl.multiple_of(r * GRID_W, GRID_W), GRID_W), :] = _unstack_heads(o4, GRID_W, NA_DIM).astype(BF16)
        return carry

    lax.fori_loop(0, rows, body, 0, unroll=4)


def _na_bias_table(rpb, rows):
    kh = NA_KH
    r_all = np.arange(rows)
    offs = r_all - np.clip(r_all - kh // 2, 0, rows - kh)
    n_off = int(offs.max()) + 1
    qc = np.arange(GRID_W)
    kcol = np.arange(GRID_W)
    cs = np.clip(qc - NA_KW // 2, 0, GRID_W - NA_KW)
    in_win = (kcol[None, :] >= cs[:, None]) & (kcol[None, :] < cs[:, None] + NA_KW)
    dc = np.clip(kcol[None, :] - qc[:, None] + NA_KW - 1, 0, 2 * NA_KW - 2)
    onehot = (dc[None] == np.arange(2 * NA_KW - 1)[:, None, None]).astype(np.float32)
    toep = jnp.einsum("hdc,cqk->hdqk", rpb.astype(F32), jnp.asarray(onehot), precision=lax.Precision.HIGHEST)
    toep = jnp.where(in_win[None, None], toep, -jnp.inf)
    per_off = [toep[:, NA_KH - 1 - off:2 * NA_KH - 1 - off] for off in range(n_off)]
    bias = jnp.stack(per_off, axis=0)
    return bias.transpose(0, 1, 3, 2, 4).reshape(n_off, NA_HEADS * GRID_W, kh * GRID_W)


def _na_attention(q, k, v, kc, vc, bias):
    b, t, w = q.shape
    tc = kc.shape[1]
    rows = t // GRID_W
    assert rows >= NA_KH and t % GRID_W == 0
    seq = lambda n: pl.BlockSpec((None, n, w), lambda i: (i, 0, 0))
    return pl.pallas_call(
        functools.partial(_na_kernel, rows=rows),
        grid=(b,),
        in_specs=[seq(t), seq(t), seq(t), seq(tc), seq(tc),
                  pl.BlockSpec(bias.shape, lambda i: (0, 0, 0))],
        out_specs=seq(t),
        out_shape=jax.ShapeDtypeStruct((b, t, w), BF16),
        compiler_params=_params(("parallel",)),
        name="na_attention",
    )(q, k, v, kc, vc, bias)


def _softmax_pv(scores, values):
    m = functools.reduce(jnp.maximum, [jnp.max(s, axis=-1, keepdims=True) for s in scores])
    ps = [jnp.exp(s - m) for s in scores]
    denom = functools.reduce(jnp.add, [jnp.sum(p, axis=-1, keepdims=True) for p in ps])
    o = functools.reduce(jnp.add, [_dot(p.astype(BF16), v) for p, v in zip(ps, values)])
    return o * (1.0 / denom)


def _attn_sliced_kernel(*refs, n_parts, heads, dh, dv):
    q_ref, o_ref = refs[0], refs[-1]
    for h in range(heads):
        q = q_ref[:, dh * h:dh * (h + 1)]
        scores = [_dot_nt(q, refs[1 + 2 * p][:, dh * h:dh * (h + 1)]) for p in range(n_parts)]
        values = [refs[2 + 2 * p][:, dv * h:dv * (h + 1)] for p in range(n_parts)]
        o_ref[:, dv * h:dv * (h + 1)] = _softmax_pv(scores, values).astype(o_ref.dtype)


def _attn_masked_kernel(q_ref, k_ref, v_ref, o_ref, *, width):
    n = q_ref.shape[0]
    q4 = _stack_heads(q_ref[...], width)
    o4 = _softmax_pv([_dot_nt(q4, k_ref[...])], [v_ref[...]])
    o_ref[...] = _unstack_heads(o4, n, width).astype(o_ref.dtype)


def _attention(q, parts, heads, dh, dv, tq):
    b, t, wq = q.shape
    tq = min(tq, t)
    flat = [a for kv in parts for a in kv]
    kv_spec = lambda a: pl.BlockSpec((None,) + a.shape[1:], lambda i, j: (i, 0, 0))
    return pl.pallas_call(
        functools.partial(_attn_sliced_kernel, n_parts=len(parts), heads=heads, dh=dh, dv=dv),
        grid=(b, t // tq),
        in_specs=[pl.BlockSpec((None, tq, wq), lambda i, j: (i, j, 0))] + [kv_spec(a) for a in flat],
        out_specs=pl.BlockSpec((None, tq, heads * dv), lambda i, j: (i, j, 0)),
        out_shape=jax.ShapeDtypeStruct((b, t, heads * dv), BF16),
        compiler_params=_params(("parallel", "parallel")),
        name="mla_attention_%d" % len(parts),
    )(q, *flat)


def _attention_masked(q, k, v, width):
    b, t, w = q.shape
    spec = lambda a: pl.BlockSpec((None,) + a.shape[1:], lambda i: (i, 0, 0))
    return pl.pallas_call(
        functools.partial(_attn_masked_kernel, width=width),
        grid=(b,),
        in_specs=[spec(q), spec(k), spec(v)],
        out_specs=spec(q),
        out_shape=jax.ShapeDtypeStruct((b, t, w), BF16),
        compiler_params=_params(("parallel",)),
        name="ctx_na_attention",
    )(q, k, v)


GLA_SAFE_DECAY = 60.0
GLA_EXACT_CHUNK = 16


def _gla_masks(c):
    ri = lax.broadcasted_iota(jnp.int32, (c, c), 0)
    ci = lax.broadcasted_iota(jnp.int32, (c, c), 1)
    ti = lax.broadcasted_iota(jnp.int32, (c, GLA_HEADS * c), 0)
    si = lax.broadcasted_iota(jnp.int32, (c, GLA_HEADS * c), 1) % c
    lk = lax.broadcasted_iota(jnp.int32, (c, GLA_KW), 1) // GLA_DK
    lv = lax.broadcasted_iota(jnp.int32, (c, GLA_WIDTH), 1) // GLA_DV
    head_v = lax.broadcasted_iota(jnp.int32, (GLA_WIDTH, GLA_KW), 0) // GLA_DV
    head_k = lax.broadcasted_iota(jnp.int32, (GLA_WIDTH, GLA_KW), 1) // GLA_DK
    return {
        "tri": ((ci <= ri).astype(BF16), (ci >= ri).astype(BF16)),
        "allowed": (si <= ti, si >= ti),
        "hm_k": [lk == h for h in range(GLA_HEADS)],
        "hm_v": [lv == h for h in range(GLA_HEADS)],
        "hm_s": head_v == head_k,
        "expand": (lax.broadcasted_iota(jnp.int32, (GLA_KW, GLA_WIDTH), 0) // GLA_DK
                   == lax.broadcasted_iota(jnp.int32, (GLA_KW, GLA_WIDTH), 1) // GLA_DV).astype(BF16),
    }


def _gla_intra_pairwise(q, k, v, bcum, expand, reverse):
    c = q.shape[0]
    row = lax.broadcasted_iota(jnp.int32, (c, GLA_WIDTH), 0)
    outs = []
    for t in range(c):
        w = jnp.exp(jnp.minimum(bcum[t:t + 1, :] - bcum, 0.0))
        a = _dot(((q[t:t + 1, :] * k) * w).astype(BF16), expand)
        keep = (row >= t) if reverse else (row <= t)
        outs.append(jnp.sum(jnp.where(keep, a * v, 0.0), axis=0, keepdims=True))
    return jnp.concatenate(outs, axis=0)


def _gla_chunk(q, k, v, g, st_ref, m, reverse, pairwise):
    c = q.shape[0]
    d = 1 if reverse else 0
    end_row = 0 if reverse else c - 1
    g_hi, g_lo = _split(g)
    tri = m["tri"][d]
    bcum = _dot(tri, g_hi) + _dot(tri, g_lo)
    bend = bcum[end_row:end_row + 1, :]
    qe = (q * jnp.exp(bcum)).astype(BF16)
    ke = (k * jnp.exp(bend - bcum)).astype(BF16)
    st = st_ref[...]
    o = _dot_nt(qe, st.astype(BF16))
    if pairwise:
        o = o + _gla_intra_pairwise(q, k, v, bcum, m["expand"], reverse)
    else:
        kt = k * jnp.exp(-bcum)
        kst = jnp.concatenate([jnp.where(hm, kt, 0.0) for hm in m["hm_k"]], axis=0).astype(BF16)
        a = jnp.where(m["allowed"][d], _dot_nt(qe, kst), 0.0)
        vbd = jnp.concatenate([jnp.where(hm, v, 0.0) for hm in m["hm_v"]], axis=0).astype(BF16)
        o = o + _dot(a.astype(BF16), vbd)
    upd = _dot_tn(v.astype(BF16), ke)
    st_ref[...] = st * jnp.exp(bend) + jnp.where(m["hm_s"], upd, 0.0)
    return o


def _gla_kernel(q_ref, k_ref, v_ref, gf_ref, gb_ref, go_ref,
                qc_ref, kc_ref, vc_ref, gfc_ref, gbc_ref, goc_ref, onorm_ref, seg_ref,
                *rest, ctx_out):
    if ctx_out:
        ox_ref, oc_ref, acc_ref, accc_ref, sf_ref, sb_ref = rest
    else:
        ox_ref, acc_ref, sf_ref, sb_ref = rest
        oc_ref = accc_ref = None
    t, tc = q_ref.shape[0], qc_ref.shape[0]

    def sweep(qr, kr, vr, gfr, gbr, dst, total, c, m, pairwise):
        count = total // c

        def body(j, carry):
            lo_f = pl.multiple_of(j * c, c)
            lo_b = pl.multiple_of((count - 1 - j) * c, c)
            of = _gla_chunk(qr[pl.ds(lo_f, c), :], kr[pl.ds(lo_f, c), :], vr[pl.ds(lo_f, c), :],
                            gfr[pl.ds(lo_f, c), :], sf_ref, m, False, pairwise)
            ob = _gla_chunk(qr[pl.ds(lo_b, c), :], kr[pl.ds(lo_b, c), :], vr[pl.ds(lo_b, c), :],
                            gbr[pl.ds(lo_b, c), :], sb_ref, m, True, pairwise)
            if dst is not None:
                dst[pl.ds(lo_f, c), :] += of
                dst[pl.ds(lo_b, c), :] += ob
            return carry
        lax.fori_loop(0, count, body, 0, unroll=1 if pairwise else 4)

    def scans(c, pairwise):
        m = _gla_masks(c)
        sf_ref[...] = jnp.zeros_like(sf_ref)
        sb_ref[...] = jnp.zeros_like(sb_ref)
        acc_ref[...] = jnp.zeros_like(acc_ref)
        if ctx_out:
            accc_ref[...] = jnp.zeros_like(accc_ref)
        sweep(qc_ref, kc_ref, vc_ref, gfc_ref, gbc_ref, accc_ref, tc, c, m, pairwise)
        sweep(q_ref, k_ref, v_ref, gf_ref, gb_ref, acc_ref, t, c, m, pairwise)

    def min_chunk_decay(gr, total):
        def body(j, low):
            lo = pl.multiple_of(j * GLA_CHUNK, GLA_CHUNK)
            return jnp.minimum(low, jnp.sum(gr[pl.ds(lo, GLA_CHUNK), :], axis=0, keepdims=True))
        return lax.fori_loop(0, total // GLA_CHUNK, body, jnp.zeros((1, GLA_KW), F32))

    low = functools.reduce(jnp.minimum, [min_chunk_decay(gf_ref, t), min_chunk_decay(gb_ref, t),
                                         min_chunk_decay(gfc_ref, tc), min_chunk_decay(gbc_ref, tc)])
    safe = jnp.min(low) >= -GLA_SAFE_DECAY
    pl.when(safe)(functools.partial(scans, GLA_CHUNK, False))
    pl.when(jnp.logical_not(safe))(functools.partial(scans, GLA_EXACT_CHUNK, True))

    def finish(acc, gate, out, total):
        tr = min(256, total)

        def body(i, carry):
            lo = pl.multiple_of(i * tr, tr)
            o = acc[pl.ds(lo, tr), :]
            ms = _dot((o * o).astype(BF16), seg_ref[...]) * (1.0 / GLA_DV)
            gt = gate[pl.ds(lo, tr), :]
            y = o * lax.rsqrt(ms + EPS) * onorm_ref[...] * (gt * jax.nn.sigmoid(gt))
            out[pl.ds(lo, tr), :] = y.astype(out.dtype)
            return carry
        lax.fori_loop(0, total // tr, body, 0)

    finish(acc_ref, go_ref, ox_ref, t)
    if ctx_out:
        finish(accc_ref, goc_ref, oc_ref, tc)


def _gla(lat, ctx, onorm, seg64, ctx_out):
    b, t, _ = lat[0].shape
    tc = ctx[0].shape[1]
    spec = lambda a: pl.BlockSpec((None,) + a.shape[1:], lambda i: (i, 0, 0))
    full = lambda a: pl.BlockSpec(a.shape, lambda i: (0,) * a.ndim)
    out_shape = [jax.ShapeDtypeStruct((b, t, GLA_WIDTH), BF16)]
    scratch = [pltpu.VMEM((t, GLA_WIDTH), F32)]
    if ctx_out:
        out_shape.append(jax.ShapeDtypeStruct((b, tc, GLA_WIDTH), BF16))
        scratch.append(pltpu.VMEM((tc, GLA_WIDTH), F32))
    scratch += [pltpu.VMEM((GLA_WIDTH, GLA_KW), F32)] * 2
    outs = pl.pallas_call(
        functools.partial(_gla_kernel, ctx_out=ctx_out),
        grid=(b,),
        in_specs=[spec(a) for a in lat] + [spec(a) for a in ctx] + [full(onorm), full(seg64)],
        out_specs=[spec(s) for s in out_shape],
        out_shape=out_shape,
        scratch_shapes=scratch,
        compiler_params=_params(("parallel",)),
        name="gla_ctx_out" if ctx_out else "gla_last",
    )(*lat, *ctx, onorm, seg64)
    return outs if ctx_out else (outs[0], None)


def _outproj_kernel(a_ref, b_ref, c_ref, x_ref, gt_ref, sc_ref, sh_ref, gffn_ref,
                    wa_ref, wb_ref, wc_ref, wrh_ref, wrl_ref, br_ref, xo_ref, h_ref, lg_ref):
    mix = _dot(a_ref[...], wa_ref[...]) + _dot(b_ref[...], wb_ref[...]) + _dot(c_ref[...], wc_ref[...])
    xn = x_ref[...] + gt_ref[...] * mix
    xo_ref[...] = xn
    h = _rms(xn) * gffn_ref[...]
    h = h * (1.0 + sc_ref[...]) + sh_ref[...]
    _store_row_tiled(h_ref, h)
    h_hi, h_lo = _split(h)
    wrh = wrh_ref[...]
    lg_ref[...] = _dot(h_hi, wrh) + _dot(h_lo, wrh) + _dot(h_hi, wrl_ref[...]) + br_ref[...]


def _outproj(a, bm, c, x, gt, sc, sh, lw):
    b, t, d = x.shape
    tm = min(512, t)
    row = lambda w: pl.BlockSpec((None, tm, w), lambda i, j: (i, j, 0))
    vec = pl.BlockSpec((None, 1, d), lambda i, j: (i, 0, 0))
    full = lambda arr: pl.BlockSpec(arr.shape, lambda i, j: (0,) * arr.ndim)
    consts = [lw["gffn"], lw["wo_a"], lw["wo_b"], lw["wo_c"], lw["wr_hi"], lw["wr_lo"], lw["br"]]
    return pl.pallas_call(
        _outproj_kernel,
        grid=(b, t // tm),
        in_specs=[row(NA_WIDTH), row(MLA_WIDTH), row(GLA_WIDTH), row(d), vec, vec, vec] + [full(w) for w in consts],
        out_specs=[row(d), pl.BlockSpec((None, tm * V7X_SUBLANES, V7X_LANES), lambda i, j: (i, j, 0)), row(V7X_LANES)],
        out_shape=[jax.ShapeDtypeStruct((b, t, d), F32),
                   jax.ShapeDtypeStruct((b, t * V7X_SUBLANES, V7X_LANES), F32),
                   jax.ShapeDtypeStruct((b, t, V7X_LANES), F32)],
        compiler_params=_params(("parallel", "parallel")),
        name="outproj_router",
    )(a, bm, c, x, gt, sc, sh, *consts)


def _moe_kernel(be_ref, tok_hbm, pair_hbm, h_hbm, w1_ref, b1_ref, w2_ref, b2_ref, y_hbm,
                g0, g1, g2, s0, s1, s2, x0, x1, x2, y0, y1, y2, w1b_ref, w2b_ref, gisem, sisem, gsem, ssem):
    i = pl.program_id(0)
    last = be_ref[pl.num_programs(0)] - 1
    sub = V7X_SUBLANES
    blk = x0.shape[0] // sub
    f = w2_ref.shape[0]
    gbuf, sbuf, xbuf, ybuf = (g0, g1, g2), (s0, s1, s2), (x0, x1, x2), (y0, y1, y2)

    def load_tok(t, s):
        return pltpu.make_async_copy(tok_hbm.at[pl.ds(pl.multiple_of(t * blk, blk), blk)], gbuf[s], gisem.at[s])

    def load_pair(t, s):
        return pltpu.make_async_copy(pair_hbm.at[pl.ds(pl.multiple_of(t * blk, blk), blk)], sbuf[s], sisem.at[s])

    def tile(ref, first_row):
        if not isinstance(first_row, int):
            first_row = pl.multiple_of(first_row, sub)
        return ref.at[pl.ds(first_row, sub)]

    def start_gathers(s):
        for r in range(blk):
            pltpu.make_async_copy(tile(h_hbm, gbuf[s][r]), tile(xbuf[s], r * sub), gsem.at[s]).start(
                priority=r % DMA_QUEUES)

    def wait_gathers(s):
        for r in range(blk):
            pltpu.make_async_copy(tile(h_hbm, 0), tile(xbuf[s], r * sub), gsem.at[s]).wait()

    def start_scatters(s):
        for r in range(blk):
            pltpu.make_async_copy(tile(ybuf[s], r * sub), tile(y_hbm, sbuf[s][r]), ssem.at[s]).start(
                priority=r % DMA_QUEUES)

    def wait_scatters(s):
        for r in range(blk):
            pltpu.make_async_copy(tile(ybuf[s], r * sub), tile(y_hbm, 0), ssem.at[s]).wait()

    @pl.when(i == 0)
    def _():
        for yb in ybuf:
            yb[...] = jnp.zeros_like(yb)
        first_spare = y_hbm.shape[0] - MOE_RING * blk * sub
        for s in range(MOE_RING - 1):
            for r in range(blk):
                pltpu.make_async_copy(tile(ybuf[s], r * sub), tile(y_hbm, first_spare + (s * blk + r) * sub),
                                      ssem.at[s]).start()
        load_tok(1, 0).start()
        load_tok(2, 1).start()
        load_tok(3, 2).start()
        load_pair(0, 2).start()
        load_tok(1, 0).wait()
        start_gathers(0)
        load_tok(2, 1).wait()
        start_gathers(1)

    e = be_ref[i]
    prev = be_ref[jnp.maximum(i - 1, 0)]

    @pl.when((i == 0) | (e != prev))
    def _():
        rows = 128

        def cast1(j, carry):
            lo = pl.multiple_of(j * rows, rows)
            w1b_ref[pl.ds(lo, rows), :] = w1_ref[pl.ds(lo, rows), :].astype(BF16)
            return carry

        def cast2(j, carry):
            lo = pl.multiple_of(j * rows, rows)
            w2b_ref[pl.ds(lo, rows), :] = w2_ref[pl.ds(lo, rows), :].astype(BF16)
            return carry
        lax.fori_loop(0, w1_ref.shape[0] // rows, cast1, 0)
        lax.fori_loop(0, w2_ref.shape[0] // rows, cast2, 0)

    def step(a):
        b, z = (a + 1) % MOE_RING, (a + 2) % MOE_RING

        load_tok(i + 3, z).wait()
        load_pair(i, z).wait()
        load_tok(i + 4, a).start()
        load_pair(i + 1, a).start()
        wait_gathers(a)
        wait_scatters(a)
        start_gathers(z)
        start_scatters(z)

        hb = _dot(_load_row_tiled(xbuf[a]).astype(BF16), w1b_ref[...]) + b1_ref[...]
        glu = jnp.minimum(hb[:, :f], SWIGLU_LIMIT)
        lin = jnp.clip(hb[:, f:], -SWIGLU_LIMIT, SWIGLU_LIMIT)
        act = (lin + 1.0) * (glu * jax.nn.sigmoid(SWIGLU_ALPHA * glu))
        _store_row_tiled(ybuf[a], _dot(act.astype(BF16), w2b_ref[...]) + b2_ref[...])

        @pl.when(i == last)
        def _():
            load_tok(i + 4, a).wait()
            load_pair(i + 1, a).wait()
            wait_gathers(b)
            wait_gathers(z)
            wait_scatters(b)
            wait_scatters(z)
            start_scatters(a)
            wait_scatters(a)

    for a in range(MOE_RING):
        pl.when((i % MOE_RING == a) & (i <= last))(functools.partial(step, a))


def _moe_blocks(h, tok_tab, pair_tab, blk_e, n_rows_out, w1, b1, w2, b2, layer):
    d = V7X_SUBLANES * V7X_LANES
    n_blk = blk_e.shape[0] - 1
    f2 = w1.shape[-1]
    f = w2.shape[-2]
    grid_spec = pltpu.PrefetchScalarGridSpec(
        num_scalar_prefetch=1,
        grid=(n_blk,),
        in_specs=[
            pl.BlockSpec(memory_space=pl.ANY),
            pl.BlockSpec(memory_space=pl.ANY),
            pl.BlockSpec(memory_space=pl.ANY),
            pl.BlockSpec((None, None, d, f2), lambda i, be: (layer, be[i], 0, 0)),
            pl.BlockSpec((None, None, 1, f2), lambda i, be: (layer, be[i], 0, 0)),
            pl.BlockSpec((None, None, f, d), lambda i, be: (layer, be[i], 0, 0)),
            pl.BlockSpec((None, None, 1, d), lambda i, be: (layer, be[i], 0, 0)),
        ],
        out_specs=pl.BlockSpec(memory_space=pl.ANY),
        scratch_shapes=[pltpu.SMEM((MOE_BLOCK,), jnp.int32)] * (2 * MOE_RING)
        + [pltpu.VMEM((MOE_BLOCK * V7X_SUBLANES, V7X_LANES), F32)] * (2 * MOE_RING) + [
            pltpu.VMEM((d, f2), BF16),
            pltpu.VMEM((f, d), BF16),
        ] + [pltpu.SemaphoreType.DMA((MOE_RING,))] * 4,
    )
    return pl.pallas_call(
        _moe_kernel,
        grid_spec=grid_spec,
        out_shape=jax.ShapeDtypeStruct((n_rows_out * V7X_SUBLANES, V7X_LANES), F32),
        compiler_params=_params(("arbitrary",), mib=56),
        name="moe_experts",
    )(blk_e, tok_tab, pair_tab, h, w1, b1.reshape(b1.shape[0], b1.shape[1], 1, f2), w2,
      b2.reshape(b2.shape[0], b2.shape[1], 1, d))


def _moe(h, logits, w1, b1, w2, b2, layer):
    n = logits.shape[0]
    nk = n * TOP_K
    top_val, top_idx = lax.top_k(logits, TOP_K)
    gates = jax.nn.softmax(top_val, axis=-1)
    flat_e = top_idx.reshape(-1)
    order = jnp.argsort(flat_e).astype(jnp.int32)
    counts = jnp.bincount(flat_e, length=N_EXPERTS).astype(jnp.int32)
    padded = (counts + MOE_BLOCK - 1) // MOE_BLOCK * MOE_BLOCK
    start = jnp.cumsum(counts) - counts
    pend = jnp.cumsum(padded)
    pstart = pend - padded
    n_blk = -(-nk // MOE_BLOCK) + N_EXPERTS
    blk_id = jnp.arange(-1, n_blk + 3, dtype=jnp.int32)
    blk_lo = blk_id * MOE_BLOCK
    blk_e = jnp.minimum(jnp.sum((pend[None, :] <= blk_lo[:, None]).astype(jnp.int32), axis=1), N_EXPERTS - 1)
    s0 = start[blk_e] + blk_lo - pstart[blk_e]
    left = jnp.where(blk_id >= 0, start[blk_e] + counts[blk_e] - s0, 0)
    r = jnp.arange(MOE_BLOCK, dtype=jnp.int32)[None, :]
    valid = r < left[:, None]
    win = order[jnp.clip(s0[:, None] + r, 0, nk - 1)]
    tok = lax.shift_right_logical(win, 2)
    dst = (win & (TOP_K - 1)) * n + tok
    dst_tab = jnp.where(valid, dst, nk + (blk_id[:, None] % MOE_RING) * MOE_BLOCK + r).reshape(-1)
    tok_tab = jnp.where(valid, tok, 0).reshape(-1)
    n_used = (pend[-1:] // MOE_BLOCK).astype(jnp.int32)
    y4 = _moe_blocks(h, tok_tab * V7X_SUBLANES, dst_tab * V7X_SUBLANES, jnp.concatenate([blk_e[1:n_blk + 1], n_used]),
                     nk + MOE_RING * MOE_BLOCK, w1, b1, w2, b2, layer)
    return y4, gates


def _combine_kernel(x_ref, gt_ref, g_ref, y0_ref, y1_ref, y2_ref, y3_ref, o_ref):
    g = g_ref[...]
    mix = g[:, 0:1] * _load_row_tiled(y0_ref)
    for k, y_ref in enumerate((y1_ref, y2_ref, y3_ref), start=1):
        mix = mix + g[:, k:k + 1] * _load_row_tiled(y_ref)
    o_ref[...] = x_ref[...] + gt_ref[...] * mix


def _combine(x, gt, gates, y4, n_tok, first_tok):
    b, t, d = x.shape
    tm = int(min(512, np.gcd.reduce([t, n_tok, first_tok or n_tok])))
    per_b = t // tm
    plane = n_tok // tm
    first = first_tok // tm
    row = pl.BlockSpec((None, tm, d), lambda i, j: (i, j, 0))
    y_spec = lambda k: pl.BlockSpec((tm * V7X_SUBLANES, V7X_LANES),
                                    lambda i, j: (k * plane + first + i * per_b + j, 0))
    return pl.pallas_call(
        _combine_kernel,
        grid=(b, per_b),
        in_specs=[row, pl.BlockSpec((None, 1, d), lambda i, j: (i, 0, 0)),
                  pl.BlockSpec((None, tm, TOP_K), lambda i, j: (i, j, 0))] + [y_spec(k) for k in range(TOP_K)],
        out_specs=row,
        out_shape=jax.ShapeDtypeStruct((b, t, d), F32),
        compiler_params=_params(("parallel", "parallel")),
        name="moe_combine",
    )(x, gt, gates, y4, y4, y4, y4)


def _rope_tables(t):
    tok = jnp.arange(t)
    row = (tok // GRID_W).astype(F32)[:, None]
    col = (tok % GRID_W).astype(F32)[:, None]
    inv = 1.0 / (ROPE_THETA ** (jnp.arange(0, ROPE_AXIS, 2, dtype=F32) / ROPE_AXIS))
    ang = jnp.concatenate([row * inv, row * inv, col * inv, col * inv], axis=-1)
    cos = jnp.ones((t, V7X_LANES), F32).at[:, MLA_NOPE:MLA_QK].set(jnp.cos(ang))
    sin = jnp.zeros((t, V7X_LANES), F32).at[:, MLA_NOPE:MLA_QK].set(jnp.sin(ang))
    return cos, sin


def _pad_heads(w, per_head, offset=0):
    lead = w.shape[:-1]
    w = w.reshape(lead + (MLA_HEADS, per_head))
    out = jnp.zeros(lead + (MLA_HEADS, V7X_LANES), w.dtype).at[..., offset:offset + per_head].set(w)
    return out.reshape(lead + (MLA_PAD,))


def _layer_weights(l, g_mix, w_in, na_q_norm, na_k_norm, mla_q_a_norm, mla_w_q_b, mla_kv_a_norm, mla_w_kv_b,
                   mla_q_norm, mla_k_norm, gla_w_gate, gla_b_gate, gla_o_norm, w_out, g_ffn, w_router, b_router):
    d = w_in.shape[1]
    sizes = (NA_WIDTH, NA_WIDTH, NA_WIDTH, MLA_Q_RANK, MLA_KV_RANK, MLA_ROPE, GLA_KW, GLA_KW, GLA_WIDTH, GLA_WIDTH,
             GLA_GATE_RANK, GLA_GATE_RANK)
    (wq, wk, wv, wcq, wckv, wkr, wgq, wgk, wgv, wgo, wlf, wlb) = jnp.split(w_in[l], np.cumsum(sizes)[:-1].tolist(), axis=-1)
    small = jnp.zeros((d, V7X_LANES), F32)
    small = small.at[:, S_LF:S_LF + GLA_GATE_RANK].set(wlf).at[:, S_LB:S_LB + GLA_GATE_RANK].set(wlb)
    small = small.at[:, S_KR:S_KR + MLA_ROPE].set(wkr)
    w_packed = jnp.concatenate([wq, wk, wv, wcq, wckv, wgq, wgk, wgv, wgo, small], axis=-1).astype(BF16)
    kvb = mla_w_kv_b[l].reshape(MLA_KV_RANK, MLA_HEADS, MLA_NOPE + MLA_V)
    wg = jnp.zeros((V7X_LANES, 2 * GLA_KW), F32)
    wg = wg.at[S_LF:S_LF + GLA_GATE_RANK, :GLA_KW].set(gla_w_gate[l, 0])
    wg = wg.at[S_LB:S_LB + GLA_GATE_RANK, GLA_KW:].set(gla_w_gate[l, 1])
    wg_hi, wg_lo = _split(wg)
    seg = (np.arange(NA_WIDTH)[:, None] // NA_DIM == np.arange(NA_WIDTH)[None, :] // NA_DIM)
    wr = jnp.zeros((d, V7X_LANES), F32).at[:, :N_EXPERTS].set(w_router[l])
    wr_hi, wr_lo = _split(wr)
    wo = w_out[l].astype(BF16)
    return {
        "gmix": g_mix[l][None, :],
        "w_in": w_packed,
        "naqn": jnp.tile(na_q_norm[l], NA_HEADS)[None, :],
        "nakn": jnp.tile(na_k_norm[l], NA_HEADS)[None, :],
        "qan": mla_q_a_norm[l][None, :],
        "wqb": _pad_heads(mla_w_q_b[l], MLA_QK).astype(BF16),
        "kvan": mla_kv_a_norm[l][None, :],
        "wkk": _pad_heads(kvb[:, :, :MLA_NOPE].reshape(MLA_KV_RANK, -1), MLA_NOPE).astype(BF16),
        "wkv": kvb[:, :, MLA_NOPE:].reshape(MLA_KV_RANK, MLA_WIDTH).astype(BF16),
        "mqn": _pad_heads(jnp.tile(mla_q_norm[l], MLA_HEADS), MLA_QK)[None, :],
        "mkn": _pad_heads(jnp.tile(mla_k_norm[l], MLA_HEADS), MLA_QK)[None, :],
        "wg_hi": wg_hi, "wg_lo": wg_lo,
        "bg": jnp.concatenate([gla_b_gate[l, 0], gla_b_gate[l, 1]])[None, :],
        "seg64": jnp.asarray(seg, BF16),
        "onorm": jnp.tile(gla_o_norm[l], GLA_HEADS)[None, :],
        "gffn": g_ffn[l][None, :],
        "wo_a": wo[:NA_WIDTH], "wo_b": wo[NA_WIDTH:NA_WIDTH + MLA_WIDTH], "wo_c": wo[NA_WIDTH + MLA_WIDTH:],
        "wr_hi": wr_hi, "wr_lo": wr_lo,
        "br": jnp.zeros((1, V7X_LANES), F32).at[0, :N_EXPERTS].set(b_router[l]),
    }


def kernel(x, c, ctx, c_ctx, w_ada, b_ada, g_mix, w_in, na_q_norm, na_k_norm, na_rpb, mla_q_a_norm, mla_w_q_b,
           mla_kv_a_norm, mla_w_kv_b, mla_q_norm, mla_k_norm, gla_w_gate, gla_b_gate, gla_o_norm, w_out, g_ffn,
           w_router, b_router, w_moe1, b_moe1, w_moe2, b_moe2):
    bsz, t, d = x.shape
    assert d == V7X_SUBLANES * V7X_LANES
    tc = ctx.shape[1]
    depth = w_ada.shape[0]
    rows = t // GRID_W
    cos, sin = _rope_tables(t)

    pad = (-(bsz + 1)) % 8
    cvec = jnp.concatenate([c, c_ctx[None, :], jnp.zeros((pad, d), F32)], axis=0)
    mod = _ada(cvec, w_ada, b_ada)

    h_ctx = ctx
    for l in range(depth):
        ctx_out = l < depth - 1
        lw = _layer_weights(l, g_mix, w_in, na_q_norm, na_k_norm, mla_q_a_norm, mla_w_q_b, mla_kv_a_norm,
                            mla_w_kv_b, mla_q_norm, mla_k_norm, gla_w_gate, gla_b_gate, gla_o_norm, w_out, g_ffn,
                            w_router, b_router)
        m_lat = [m[:, None, :] for m in jnp.split(mod[l, :bsz], 6, axis=-1)]
        m_ctx = [jnp.broadcast_to(m[None, :, :], (bsz, 1, d)) for m in jnp.split(mod[l, bsz:bsz + 1], 6, axis=-1)]
        sh1, sc1, gt1, sh2, sc2, gt2 = m_lat
        csh1, csc1, cgt1, csh2, csc2, cgt2 = m_ctx

        (naq, nak, nav, mq, mk, mv, gq, gk, gv, gout, gf, gb) = _inproj(x, sc1, sh1, lw, cos, sin, True)
        (cnaq, cnak, cnav, cmq, cmk, cmv, cgq, cgk, cgv, cgout, cgf, cgb) = _inproj(
            h_ctx, csc1, csh1, lw, cos[:tc], sin[:tc], False)

        bias = _na_bias_table(na_rpb[l], rows)
        a_x = _na_attention(naq, nak, nav, cnak, cnav, bias)
        b_x = _attention(mq, [(mk, mv), (cmk, cmv)], MLA_HEADS, V7X_LANES, MLA_V, 256)
        c_x, c_c = _gla((gq, gk, gv, gf, gb, gout), (cgq, cgk, cgv, cgf, cgb, cgout), lw["onorm"], lw["seg64"], ctx_out)
        x_new, h2, logits = _outproj(a_x, b_x, c_x, x, gt1, sc2, sh2, lw)

        toks = [h2.reshape(bsz * t * V7X_SUBLANES, V7X_LANES)]
        lgs = [logits.reshape(bsz * t, V7X_LANES)]
        if ctx_out:
            a_c = _attention_masked(cnaq, cnak, cnav, NA_DIM)
            b_c = _attention(cmq, [(cmk, cmv)], MLA_HEADS, V7X_LANES, MLA_V, 256)
            hc_new, hc2, clogits = _outproj(a_c, b_c, c_c, h_ctx, cgt1, csc2, csh2, lw)
            toks.append(hc2.reshape(bsz * tc * V7X_SUBLANES, V7X_LANES))
            lgs.append(clogits.reshape(bsz * tc, V7X_LANES))
        n_tok = sum(a.shape[0] for a in lgs)
        y4, gates = _moe(jnp.concatenate(toks, axis=0), jnp.concatenate(lgs, axis=0)[:, :N_EXPERTS],
                         w_moe1, b_moe1, w_moe2, b_moe2, l)
        x = _combine(x_new, gt2, gates[:bsz * t].reshape(bsz, t, TOP_K), y4, n_tok, 0)
        if ctx_out:
            h_ctx = _combine(hc_new, cgt2, gates[bsz * t:].reshape(bsz, tc, TOP_K), y4, n_tok, bsz * t)
    return x
```

```python
import functools

import numpy as np
import jax
import jax.numpy as jnp
from jax import lax
from jax.experimental import pallas as pl
from jax.experimental.pallas import tpu as pltpu

F32 = jnp.float32
BF16 = jnp.bfloat16

V7X_LANES = 128
V7X_VMEM_BYTES = 64 * 1024 * 1024
DMA_QUEUES = 2

EPS = 1e-6
GRID_W = 64
NA_HEADS, NA_DIM, NA_KH, NA_KW = 4, 64, 8, 16
MLA_HEADS, MLA_NOPE, MLA_ROPE, MLA_V = 4, 64, 32, 128
MLA_QK = MLA_NOPE + MLA_ROPE
MLA_Q_RANK, MLA_KV_RANK = 256, 128
ROPE_AXIS = MLA_ROPE // 2
ROPE_THETA = 10000.0
GLA_HEADS, GLA_DK, GLA_DV = 4, 32, 64
GLA_GATE_RANK, GLA_GATE_NORM, GLA_CHUNK = 16, 16.0, 64
NA_WIDTH = NA_HEADS * NA_DIM
MLA_WIDTH = MLA_HEADS * MLA_V
MLA_PAD = MLA_HEADS * V7X_LANES
GLA_KW = GLA_HEADS * GLA_DK
GLA_WIDTH = GLA_HEADS * GLA_DV
N_EXPERTS, TOP_K = 32, 4
SWIGLU_LIMIT, SWIGLU_ALPHA = 7.0, 1.702
MOE_BLOCK = 512
MOE_RING = 3

C_NAQ, C_NAK, C_NAV, C_CQ, C_CKV = 0, 256, 512, 768, 1024
C_GQ, C_GK, C_GV, C_GOUT, C_SMALL = 1152, 1280, 1408, 1664, 1920
IN_PACKED = 2048
S_LF, S_LB, S_KR = 0, 16, 64


def _vmem_limit(mib):
    return min(mib * 1024 * 1024, V7X_VMEM_BYTES - 4 * 1024 * 1024)


def _params(sem, mib=48):
    return pltpu.CompilerParams(dimension_semantics=sem, vmem_limit_bytes=_vmem_limit(mib))


def _dot(a, b):
    return jnp.dot(a, b, preferred_element_type=F32)


def _dot_nt(a, b):
    return lax.dot_general(a, b, (((1,), (1,)), ((), ())), preferred_element_type=F32)


def _dot_tn(a, b):
    return lax.dot_general(a, b, (((0,), (0,)), ((), ())), preferred_element_type=F32)


def _split(x):
    hi = x.astype(BF16)
    lo = (x - hi.astype(F32)).astype(BF16)
    return hi, lo


def _rms(x):
    return x * lax.rsqrt(jnp.mean(x * x, axis=-1, keepdims=True) + EPS)


V7X_SUBLANES = 8


def _store_row_tiled(ref, x):
    n = x.shape[0]
    for j in range(V7X_SUBLANES):
        ref[pl.ds(j, n, stride=V7X_SUBLANES), :] = x[:, V7X_LANES * j:V7X_LANES * (j + 1)]


def _load_row_tiled(ref):
    n = ref.shape[0] // V7X_SUBLANES
    return jnp.concatenate([ref[pl.ds(j, n, stride=V7X_SUBLANES), :] for j in range(V7X_SUBLANES)], axis=1)


def _ada_kernel(c_ref, w_ref, b_ref, o_ref):
    c = c_ref[...]
    s = (c * jax.nn.sigmoid(c)).astype(BF16)
    o_ref[...] = _dot(s, w_ref[...].astype(BF16)) + b_ref[...]


def _ada(cvec, w_ada, b_ada):
    depth, d, n6 = w_ada.shape
    r = cvec.shape[0]
    tn = 1024
    return pl.pallas_call(
        _ada_kernel,
        grid=(depth, n6 // tn),
        in_specs=[
            pl.BlockSpec((r, d), lambda l, j: (0, 0)),
            pl.BlockSpec((None, d, tn), lambda l, j: (l, 0, j)),
            pl.BlockSpec((None, 1, tn), lambda l, j: (l, 0, j)),
        ],
        out_specs=pl.BlockSpec((None, r, tn), lambda l, j: (l, 0, j)),
        out_shape=jax.ShapeDtypeStruct((depth, r, n6), F32),
        compiler_params=_params(("parallel", "parallel")),
        name="ada_modulation",
    )(cvec, w_ada, b_ada.reshape(depth, 1, n6))


def _rope(x, cos, sin):
    lane = lax.broadcasted_iota(jnp.int32, (x.shape[0], V7X_LANES), 1)
    first = (lane & (ROPE_AXIS - 1)) < (ROPE_AXIS // 2)
    outs = []
    for h in range(MLA_HEADS):
        xs = x[:, V7X_LANES * h:V7X_LANES * (h + 1)]
        rot = jnp.where(first, -pltpu.roll(xs, V7X_LANES - ROPE_AXIS // 2, 1), pltpu.roll(xs, ROPE_AXIS // 2, 1))
        outs.append(xs * cos + rot * sin)
    return jnp.concatenate(outs, axis=1)


def _head_rms_padded(x, n_real):
    outs = []
    for h in range(MLA_HEADS):
        xs = x[:, V7X_LANES * h:V7X_LANES * (h + 1)]
        ms = jnp.sum(xs * xs, axis=-1, keepdims=True) * (1.0 / n_real)
        outs.append(xs * lax.rsqrt(ms + EPS))
    return jnp.concatenate(outs, axis=1)


def _log_sigmoid(x):
    return jnp.minimum(x, 0.0) - jnp.log1p(jnp.exp(-jnp.abs(x)))


def _inproj_kernel(x_ref, sc_ref, sh_ref, gmix_ref, w_ref, cos_ref, sin_ref,
                   naqn_ref, nakn_ref, qan_ref, wqb_ref, kvan_ref, wkk_ref, wkv_ref, mqn_ref, mkn_ref,
                   wgh_ref, wgl_ref, bg_ref, seg_ref,
                   naq_o, nak_o, nav_o, mq_o, mk_o, mv_o, gq_o, gk_o, gv_o, gout_o, gf_o, gb_o, *, use_rope):
    x = x_ref[...]
    h = _rms(x) * gmix_ref[...]
    h = h * (1.0 + sc_ref[...]) + sh_ref[...]
    p = _dot(h.astype(BF16), w_ref[...])

    seg = seg_ref[...]
    q = p[:, C_NAQ:C_NAQ + NA_WIDTH]
    k = p[:, C_NAK:C_NAK + NA_WIDTH]
    qss = _dot((q * q).astype(BF16), seg) * (1.0 / NA_DIM)
    kss = _dot((k * k).astype(BF16), seg) * (1.0 / NA_DIM)
    naq_o[...] = (q * lax.rsqrt(qss + EPS) * naqn_ref[...] * (NA_DIM ** -0.5)).astype(BF16)
    nak_o[...] = (k * lax.rsqrt(kss + EPS) * nakn_ref[...]).astype(BF16)
    nav_o[...] = p[:, C_NAV:C_NAV + NA_WIDTH].astype(BF16)

    small = p[:, C_SMALL:C_SMALL + V7X_LANES]
    cq = _rms(p[:, C_CQ:C_CQ + MLA_Q_RANK]) * qan_ref[...]
    mq = _head_rms_padded(_dot(cq.astype(BF16), wqb_ref[...]), MLA_QK) * mqn_ref[...]
    ckv = (_rms(p[:, C_CKV:C_CKV + MLA_KV_RANK]) * kvan_ref[...]).astype(BF16)
    lane = lax.broadcasted_iota(jnp.int32, small.shape, 1)
    kr = jnp.where((lane >= S_KR) & (lane < S_KR + MLA_ROPE), small, 0.0)
    mk = _dot(ckv, wkk_ref[...]) + jnp.concatenate([kr] * MLA_HEADS, axis=1)
    mk = _head_rms_padded(mk, MLA_QK) * mkn_ref[...]
    if use_rope:
        cos, sin = cos_ref[...], sin_ref[...]
        mq = _rope(mq, cos, sin)
        mk = _rope(mk, cos, sin)
    mq_o[...] = (mq * (MLA_QK ** -0.5)).astype(BF16)
    mk_o[...] = mk.astype(BF16)
    mv_o[...] = _dot(ckv, wkv_ref[...]).astype(BF16)

    gq_o[...] = p[:, C_GQ:C_GQ + GLA_KW] * (GLA_DK ** -0.5)
    gk_o[...] = p[:, C_GK:C_GK + GLA_KW]
    gv_o[...] = p[:, C_GV:C_GV + GLA_WIDTH]
    gout_o[...] = p[:, C_GOUT:C_GOUT + GLA_WIDTH]
    s_hi, s_lo = _split(small)
    wgh = wgh_ref[...]
    pre = _dot(s_hi, wgh) + _dot(s_lo, wgh) + _dot(s_hi, wgl_ref[...]) + bg_ref[...]
    ls = _log_sigmoid(pre) * (1.0 / GLA_GATE_NORM)
    gf_o[...] = ls[:, :GLA_KW]
    gb_o[...] = ls[:, GLA_KW:]


def _inproj(x, sc, sh, lw, cos, sin, use_rope):
    b, t, d = x.shape
    tm = min(512, t)
    full = lambda a: pl.BlockSpec(a.shape, lambda i, j: (0,) * a.ndim)
    row = lambda w: pl.BlockSpec((None, tm, w), lambda i, j: (i, j, 0))
    consts = [lw["gmix"], lw["w_in"]]
    tail = [lw["naqn"], lw["nakn"], lw["qan"], lw["wqb"], lw["kvan"], lw["wkk"], lw["wkv"], lw["mqn"], lw["mkn"],
            lw["wg_hi"], lw["wg_lo"], lw["bg"], lw["seg64"]]
    widths = [(NA_WIDTH, BF16)] * 3 + [(MLA_PAD, BF16), (MLA_PAD, BF16), (MLA_WIDTH, BF16),
                                       (GLA_KW, F32), (GLA_KW, F32), (GLA_WIDTH, F32), (GLA_WIDTH, F32),
                                       (GLA_KW, F32), (GLA_KW, F32)]
    return pl.pallas_call(
        functools.partial(_inproj_kernel, use_rope=use_rope),
        grid=(b, t // tm),
        in_specs=[row(d),
                  pl.BlockSpec((None, 1, d), lambda i, j: (i, 0, 0)),
                  pl.BlockSpec((None, 1, d), lambda i, j: (i, 0, 0))]
                 + [full(a) for a in consts]
                 + [pl.BlockSpec((tm, V7X_LANES), lambda i, j: (j, 0))] * 2
                 + [full(a) for a in tail],
        out_specs=[row(w) for w, _ in widths],
        out_shape=[jax.ShapeDtypeStruct((b, t, w), dt) for w, dt in widths],
        compiler_params=_params(("parallel", "parallel")),
        name="inproj_rope" if use_rope else "inproj_ctx",
    )(x, sc, sh, *consts, cos, sin, *tail)


def _stack_heads(q, width):
    lane = lax.broadcasted_iota(jnp.int32, q.shape, 1)
    zero = jnp.zeros_like(q)
    return jnp.concatenate([jnp.where(lane // width == h, q, zero) for h in range(NA_HEADS)], axis=0)


def _unstack_heads(o4, n, width):
    lane = lax.broadcasted_iota(jnp.int32, (n, o4.shape[1]), 1)
    out = jnp.zeros((n, o4.shape[1]), F32)
    for h in range(NA_HEADS):
        out = out + jnp.where(lane // width == h, o4[h * n:(h + 1) * n], 0.0)
    return out


def _na_kernel(q_ref, k_ref, v_ref, kc_ref, vc_ref, bias_ref, o_ref, *, rows):
    kc = kc_ref[...]
    vc = vc_ref[...]
    n_loc = NA_KH * GRID_W

    def body(r, carry):
        rs = jnp.clip(r - NA_KH // 2, 0, rows - NA_KH)
        off = r - rs
        q = q_ref[pl.ds(pl.multiple_of(r * GRID_W, GRID_W), GRID_W), :]
        ks = k_ref[pl.ds(pl.multiple_of(rs * GRID_W, GRID_W), n_loc), :]
        vs = v_ref[pl.ds(pl.multiple_of(rs * GRID_W, GRID_W), n_loc), :]
        q4 = _stack_heads(q, NA_DIM)
        s_loc = _dot_nt(q4, ks) + bias_ref[off]
        s_ctx = _dot_nt(q4, kc)
        m = jnp.maximum(jnp.max(s_loc, axis=-1, keepdims=True), jnp.max(s_ctx, axis=-1, keepdims=True))
        p_loc = jnp.exp(s_loc - m)
        p_ctx = jnp.exp(s_ctx - m)
        denom = jnp.sum(p_loc, axis=-1, keepdims=True) + jnp.sum(p_ctx, axis=-1, keepdims=True)
        o4 = (_dot(p_loc.astype(BF16), vs) + _dot(p_ctx.astype(BF16), vc)) * (1.0 / denom)
        o_ref[pl.ds(pl.multiple_of(r * GRID_W, GRID_W), GRID_W), :] = _unstack_heads(o4, GRID_W, NA_DIM).astype(BF16)
        return carry

    lax.fori_loop(0, rows, body, 0, unroll=4)


def _na_bias_table(rpb, rows):
    kh = NA_KH
    r_all = np.arange(rows)
    offs = r_all - np.clip(r_all - kh // 2, 0, rows - kh)
    n_off = int(offs.max()) + 1
    qc = np.arange(GRID_W)
    kcol = np.arange(GRID_W)
    cs = np.clip(qc - NA_KW // 2, 0, GRID_W - NA_KW)
    in_win = (kcol[None, :] >= cs[:, None]) & (kcol[None, :] < cs[:, None] + NA_KW)
    dc = np.clip(kcol[None, :] - qc[:, None] + NA_KW - 1, 0, 2 * NA_KW - 2)
    onehot = (dc[None] == np.arange(2 * NA_KW - 1)[:, None, None]).astype(np.float32)
    toep = jnp.einsum("hdc,cqk->hdqk", rpb.astype(F32), jnp.asarray(onehot), precision=lax.Precision.HIGHEST)
    toep = jnp.where(in_win[None, None], toep, -jnp.inf)
    per_off = [toep[:, NA_KH - 1 - off:2 * NA_KH - 1 - off] for off in range(n_off)]
    bias = jnp.stack(per_off, axis=0)
    return bias.transpose(0, 1, 3, 2, 4).reshape(n_off, NA_HEADS * GRID_W, kh * GRID_W)


def _na_attention(q, k, v, kc, vc, bias):
    b, t, w = q.shape
    tc = kc.shape[1]
    rows = t // GRID_W
    assert rows >= NA_KH and t % GRID_W == 0
    seq = lambda n: pl.BlockSpec((None, n, w), lambda i: (i, 0, 0))
    return pl.pallas_call(
        functools.partial(_na_kernel, rows=rows),
        grid=(b,),
        in_specs=[seq(t), seq(t), seq(t), seq(tc), seq(tc),
                  pl.BlockSpec(bias.shape, lambda i: (0, 0, 0))],
        out_specs=seq(t),
        out_shape=jax.ShapeDtypeStruct((b, t, w), BF16),
        compiler_params=_params(("parallel",)),
        name="na_attention",
    )(q, k, v, kc, vc, bias)


def _softmax_pv(scores, values):
    m = functools.reduce(jnp.maximum, [jnp.max(s, axis=-1, keepdims=True) for s in scores])
    ps = [jnp.exp(s - m) for s in scores]
    denom = functools.reduce(jnp.add, [jnp.sum(p, axis=-1, keepdims=True) for p in ps])
    o = functools.reduce(jnp.add, [_dot(p.astype(BF16), v) for p, v in zip(ps, values)])
    return o * (1.0 / denom)


def _attn_sliced_kernel(*refs, n_parts, heads, dh, dv):
    q_ref, o_ref = refs[0], refs[-1]
    for h in range(heads):
        q = q_ref[:, dh * h:dh * (h + 1)]
        scores = [_dot_nt(q, refs[1 + 2 * p][:, dh * h:dh * (h + 1)]) for p in range(n_parts)]
        values = [refs[2 + 2 * p][:, dv * h:dv * (h + 1)] for p in range(n_parts)]
        o_ref[:, dv * h:dv * (h + 1)] = _softmax_pv(scores, values).astype(o_ref.dtype)


def _attn_masked_kernel(q_ref, k_ref, v_ref, o_ref, *, width):
    n = q_ref.shape[0]
    q4 = _stack_heads(q_ref[...], width)
    o4 = _softmax_pv([_dot_nt(q4, k_ref[...])], [v_ref[...]])
    o_ref[...] = _unstack_heads(o4, n, width).astype(o_ref.dtype)


def _attention(q, parts, heads, dh, dv, tq):
    b, t, wq = q.shape
    tq = min(tq, t)
    flat = [a for kv in parts for a in kv]
    kv_spec = lambda a: pl.BlockSpec((None,) + a.shape[1:], lambda i, j: (i, 0, 0))
    return pl.pallas_call(
        functools.partial(_attn_sliced_kernel, n_parts=len(parts), heads=heads, dh=dh, dv=dv),
        grid=(b, t // tq),
        in_specs=[pl.BlockSpec((None, tq, wq), lambda i, j: (i, j, 0))] + [kv_spec(a) for a in flat],
        out_specs=pl.BlockSpec((None, tq, heads * dv), lambda i, j: (i, j, 0)),
        out_shape=jax.ShapeDtypeStruct((b, t, heads * dv), BF16),
        compiler_params=_params(("parallel", "parallel")),
        name="mla_attention_%d" % len(parts),
    )(q, *flat)


def _attention_masked(q, k, v, width):
    b, t, w = q.shape
    spec = lambda a: pl.BlockSpec((None,) + a.shape[1:], lambda i: (i, 0, 0))
    return pl.pallas_call(
        functools.partial(_attn_masked_kernel, width=width),
        grid=(b,),
        in_specs=[spec(q), spec(k), spec(v)],
        out_specs=spec(q),
        out_shape=jax.ShapeDtypeStruct((b, t, w), BF16),
        compiler_params=_params(("parallel",)),
        name="ctx_na_attention",
    )(q, k, v)


GLA_SAFE_DECAY = 60.0
GLA_EXACT_CHUNK = 16


def _gla_masks(c):
    ri = lax.broadcasted_iota(jnp.int32, (c, c), 0)
    ci = lax.broadcasted_iota(jnp.int32, (c, c), 1)
    ti = lax.broadcasted_iota(jnp.int32, (c, GLA_HEADS * c), 0)
    si = lax.broadcasted_iota(jnp.int32, (c, GLA_HEADS * c), 1) % c
    lk = lax.broadcasted_iota(jnp.int32, (c, GLA_KW), 1) // GLA_DK
    lv = lax.broadcasted_iota(jnp.int32, (c, GLA_WIDTH), 1) // GLA_DV
    head_v = lax.broadcasted_iota(jnp.int32, (GLA_WIDTH, GLA_KW), 0) // GLA_DV
    head_k = lax.broadcasted_iota(jnp.int32, (GLA_WIDTH, GLA_KW), 1) // GLA_DK
    return {
        "tri": ((ci <= ri).astype(BF16), (ci >= ri).astype(BF16)),
        "allowed": (si <= ti, si >= ti),
        "hm_k": [lk == h for h in range(GLA_HEADS)],
        "hm_v": [lv == h for h in range(GLA_HEADS)],
        "hm_s": head_v == head_k,
        "expand": (lax.broadcasted_iota(jnp.int32, (GLA_KW, GLA_WIDTH), 0) // GLA_DK
                   == lax.broadcasted_iota(jnp.int32, (GLA_KW, GLA_WIDTH), 1) // GLA_DV).astype(BF16),
    }


def _gla_intra_pairwise(q, k, v, bcum, expand, reverse):
    c = q.shape[0]
    row = lax.broadcasted_iota(jnp.int32, (c, GLA_WIDTH), 0)
    outs = []
    for t in range(c):
        w = jnp.exp(jnp.minimum(bcum[t:t + 1, :] - bcum, 0.0))
        a = _dot(((q[t:t + 1, :] * k) * w).astype(BF16), expand)
        keep = (row >= t) if reverse else (row <= t)
        outs.append(jnp.sum(jnp.where(keep, a * v, 0.0), axis=0, keepdims=True))
    return jnp.concatenate(outs, axis=0)


def _gla_chunk(q, k, v, g, st_ref, m, reverse, pairwise):
    c = q.shape[0]
    d = 1 if reverse else 0
    end_row = 0 if reverse else c - 1
    g_hi, g_lo = _split(g)
    tri = m["tri"][d]
    bcum = _dot(tri, g_hi) + _dot(tri, g_lo)
    bend = bcum[end_row:end_row + 1, :]
    qe = (q * jnp.exp(bcum)).astype(BF16)
    ke = (k * jnp.exp(bend - bcum)).astype(BF16)
    st = st_ref[...]
    o = _dot_nt(qe, st.astype(BF16))
    if pairwise:
        o = o + _gla_intra_pairwise(q, k, v, bcum, m["expand"], reverse)
    else:
        kt = k * jnp.exp(-bcum)
        kst = jnp.concatenate([jnp.where(hm, kt, 0.0) for hm in m["hm_k"]], axis=0).astype(BF16)
        a = jnp.where(m["allowed"][d], _dot_nt(qe, kst), 0.0)
        vbd = jnp.concatenate([jnp.where(hm, v, 0.0) for hm in m["hm_v"]], axis=0).astype(BF16)
        o = o + _dot(a.astype(BF16), vbd)
    upd = _dot_tn(v.astype(BF16), ke)
    st_ref[...] = st * jnp.exp(bend) + jnp.where(m["hm_s"], upd, 0.0)
    return o


def _gla_kernel(q_ref, k_ref, v_ref, gf_ref, gb_ref, go_ref,
                qc_ref, kc_ref, vc_ref, gfc_ref, gbc_ref, goc_ref, onorm_ref, seg_ref,
                *rest, ctx_out):
    if ctx_out:
        ox_ref, oc_ref, acc_ref, accc_ref, sf_ref, sb_ref = rest
    else:
        ox_ref, acc_ref, sf_ref, sb_ref = rest
        oc_ref = accc_ref = None
    t, tc = q_ref.shape[0], qc_ref.shape[0]

    def sweep(qr, kr, vr, gfr, gbr, dst, total, c, m, pairwise):
        count = total // c

        def body(j, carry):
            lo_f = pl.multiple_of(j * c, c)
            lo_b = pl.multiple_of((count - 1 - j) * c, c)
            of = _gla_chunk(qr[pl.ds(lo_f, c), :], kr[pl.ds(lo_f, c), :], vr[pl.ds(lo_f, c), :],
                            gfr[pl.ds(lo_f, c), :], sf_ref, m, False, pairwise)
            ob = _gla_chunk(qr[pl.ds(lo_b, c), :], kr[pl.ds(lo_b, c), :], vr[pl.ds(lo_b, c), :],
                            gbr[pl.ds(lo_b, c), :], sb_ref, m, True, pairwise)
            if dst is not None:
                dst[pl.ds(lo_f, c), :] += of
                dst[pl.ds(lo_b, c), :] += ob
            return carry
        lax.fori_loop(0, count, body, 0, unroll=1 if pairwise else 4)

    def scans(c, pairwise):
        m = _gla_masks(c)
        sf_ref[...] = jnp.zeros_like(sf_ref)
        sb_ref[...] = jnp.zeros_like(sb_ref)
        acc_ref[...] = jnp.zeros_like(acc_ref)
        if ctx_out:
            accc_ref[...] = jnp.zeros_like(accc_ref)
        sweep(qc_ref, kc_ref, vc_ref, gfc_ref, gbc_ref, accc_ref, tc, c, m, pairwise)
        sweep(q_ref, k_ref, v_ref, gf_ref, gb_ref, acc_ref, t, c, m, pairwise)

    def min_chunk_decay(gr, total):
        def body(j, low):
            lo = pl.multiple_of(j * GLA_CHUNK, GLA_CHUNK)
            return jnp.minimum(low, jnp.sum(gr[pl.ds(lo, GLA_CHUNK), :], axis=0, keepdims=True))
        return lax.fori_loop(0, total // GLA_CHUNK, body, jnp.zeros((1, GLA_KW), F32))

    low = functools.reduce(jnp.minimum, [min_chunk_decay(gf_ref, t), min_chunk_decay(gb_ref, t),
                                         min_chunk_decay(gfc_ref, tc), min_chunk_decay(gbc_ref, tc)])
    safe = jnp.min(low) >= -GLA_SAFE_DECAY
    pl.when(safe)(functools.partial(scans, GLA_CHUNK, False))
    pl.when(jnp.logical_not(safe))(functools.partial(scans, GLA_EXACT_CHUNK, True))

    def finish(acc, gate, out, total):
        tr = min(256, total)

        def body(i, carry):
            lo = pl.multiple_of(i * tr, tr)
            o = acc[pl.ds(lo, tr), :]
            ms = _dot((o * o).astype(BF16), seg_ref[...]) * (1.0 / GLA_DV)
            gt = gate[pl.ds(lo, tr), :]
            y = o * lax.rsqrt(ms + EPS) * onorm_ref[...] * (gt * jax.nn.sigmoid(gt))
            out[pl.ds(lo, tr), :] = y.astype(out.dtype)
            return carry
        lax.fori_loop(0, total // tr, body, 0)

    finish(acc_ref, go_ref, ox_ref, t)
    if ctx_out:
        finish(accc_ref, goc_ref, oc_ref, tc)


def _gla(lat, ctx, onorm, seg64, ctx_out):
    b, t, _ = lat[0].shape
    tc = ctx[0].shape[1]
    spec = lambda a: pl.BlockSpec((None,) + a.shape[1:], lambda i: (i, 0, 0))
    full = lambda a: pl.BlockSpec(a.shape, lambda i: (0,) * a.ndim)
    out_shape = [jax.ShapeDtypeStruct((b, t, GLA_WIDTH), BF16)]
    scratch = [pltpu.VMEM((t, GLA_WIDTH), F32)]
    if ctx_out:
        out_shape.append(jax.ShapeDtypeStruct((b, tc, GLA_WIDTH), BF16))
        scratch.append(pltpu.VMEM((tc, GLA_WIDTH), F32))
    scratch += [pltpu.VMEM((GLA_WIDTH, GLA_KW), F32)] * 2
    outs = pl.pallas_call(
        functools.partial(_gla_kernel, ctx_out=ctx_out),
        grid=(b,),
        in_specs=[spec(a) for a in lat] + [spec(a) for a in ctx] + [full(onorm), full(seg64)],
        out_specs=[spec(s) for s in out_shape],
        out_shape=out_shape,
        scratch_shapes=scratch,
        compiler_params=_params(("parallel",)),
        name="gla_ctx_out" if ctx_out else "gla_last",
    )(*lat, *ctx, onorm, seg64)
    return outs if ctx_out else (outs[0], None)


def _outproj_kernel(a_ref, b_ref, c_ref, x_ref, gt_ref, sc_ref, sh_ref, gffn_ref,
                    wa_ref, wb_ref, wc_ref, wrh_ref, wrl_ref, br_ref, xo_ref, h_ref, lg_ref):
    mix = _dot(a_ref[...], wa_ref[...]) + _dot(b_ref[...], wb_ref[...]) + _dot(c_ref[...], wc_ref[...])
    xn = x_ref[...] + gt_ref[...] * mix
    xo_ref[...] = xn
    h = _rms(xn) * gffn_ref[...]
    h = h * (1.0 + sc_ref[...]) + sh_ref[...]
    _store_row_tiled(h_ref, h)
    h_hi, h_lo = _split(h)
    wrh = wrh_ref[...]
    lg_ref[...] = _dot(h_hi, wrh) + _dot(h_lo, wrh) + _dot(h_hi, wrl_ref[...]) + br_ref[...]


def _outproj(a, bm, c, x, gt, sc, sh, lw):
    b, t, d = x.shape
    tm = min(512, t)
    row = lambda w: pl.BlockSpec((None, tm, w), lambda i, j: (i, j, 0))
    vec = pl.BlockSpec((None, 1, d), lambda i, j: (i, 0, 0))
    full = lambda arr: pl.BlockSpec(arr.shape, lambda i, j: (0,) * arr.ndim)
    consts = [lw["gffn"], lw["wo_a"], lw["wo_b"], lw["wo_c"], lw["wr_hi"], lw["wr_lo"], lw["br"]]
    return pl.pallas_call(
        _outproj_kernel,
        grid=(b, t // tm),
        in_specs=[row(NA_WIDTH), row(MLA_WIDTH), row(GLA_WIDTH), row(d), vec, vec, vec] + [full(w) for w in consts],
        out_specs=[row(d), pl.BlockSpec((None, tm * V7X_SUBLANES, V7X_LANES), lambda i, j: (i, j, 0)), row(V7X_LANES)],
        out_shape=[jax.ShapeDtypeStruct((b, t, d), F32),
                   jax.ShapeDtypeStruct((b, t * V7X_SUBLANES, V7X_LANES), F32),
                   jax.ShapeDtypeStruct((b, t, V7X_LANES), F32)],
        compiler_params=_params(("parallel", "parallel")),
        name="outproj_router",
    )(a, bm, c, x, gt, sc, sh, *consts)


def _moe_kernel(be_ref, tok_hbm, pair_hbm, h_hbm, w1_ref, b1_ref, w2_ref, b2_ref, y_hbm,
                g0, g1, g2, s0, s1, s2, x0, x1, x2, y0, y1, y2, w1b_ref, w2b_ref, gisem, sisem, gsem, ssem):
    i = pl.program_id(0)
    last = be_ref[pl.num_programs(0)] - 1
    sub = V7X_SUBLANES
    blk = x0.shape[0] // sub
    f = w2_ref.shape[0]
    gbuf, sbuf, xbuf, ybuf = (g0, g1, g2), (s0, s1, s2), (x0, x1, x2), (y0, y1, y2)

    def load_tok(t, s):
        return pltpu.make_async_copy(tok_hbm.at[pl.ds(pl.multiple_of(t * blk, blk), blk)], gbuf[s], gisem.at[s])

    def load_pair(t, s):
        return pltpu.make_async_copy(pair_hbm.at[pl.ds(pl.multiple_of(t * blk, blk), blk)], sbuf[s], sisem.at[s])

    def tile(ref, first_row):
        if not isinstance(first_row, int):
            first_row = pl.multiple_of(first_row, sub)
        return ref.at[pl.ds(first_row, sub)]

    def start_gathers(s):
        for r in range(blk):
            pltpu.make_async_copy(tile(h_hbm, gbuf[s][r]), tile(xbuf[s], r * sub), gsem.at[s]).start(
                priority=r % DMA_QUEUES)

    def wait_gathers(s):
        for r in range(blk):
            pltpu.make_async_copy(tile(h_hbm, 0), tile(xbuf[s], r * sub), gsem.at[s]).wait()

    def start_scatters(s):
        for r in range(blk):
            pltpu.make_async_copy(tile(ybuf[s], r * sub), tile(y_hbm, sbuf[s][r]), ssem.at[s]).start(
                priority=r % DMA_QUEUES)

    def wait_scatters(s):
        for r in range(blk):
            pltpu.make_async_copy(tile(ybuf[s], r * sub), tile(y_hbm, 0), ssem.at[s]).wait()

    @pl.when(i == 0)
    def _():
        for yb in ybuf:
            yb[...] = jnp.zeros_like(yb)
        first_spare = y_hbm.shape[0] - MOE_RING * blk * sub
        for s in range(MOE_RING - 1):
            for r in range(blk):
                pltpu.make_async_copy(tile(ybuf[s], r * sub), tile(y_hbm, first_spare + (s * blk + r) * sub),
                                      ssem.at[s]).start()
        load_tok(1, 0).start()
        load_tok(2, 1).start()
        load_tok(3, 2).start()
        load_pair(0, 2).start()
        load_tok(1, 0).wait()
        start_gathers(0)
        load_tok(2, 1).wait()
        start_gathers(1)

    e = be_ref[i]
    prev = be_ref[jnp.maximum(i - 1, 0)]

    @pl.when((i == 0) | (e != prev))
    def _():
        rows = 128

        def cast1(j, carry):
            lo = pl.multiple_of(j * rows, rows)
            w1b_ref[pl.ds(lo, rows), :] = w1_ref[pl.ds(lo, rows), :].astype(BF16)
            return carry

        def cast2(j, carry):
            lo = pl.multiple_of(j * rows, rows)
            w2b_ref[pl.ds(lo, rows), :] = w2_ref[pl.ds(lo, rows), :].astype(BF16)
            return carry
        lax.fori_loop(0, w1_ref.shape[0] // rows, cast1, 0)
        lax.fori_loop(0, w2_ref.shape[0] // rows, cast2, 0)

    def step(a):
        b, z = (a + 1) % MOE_RING, (a + 2) % MOE_RING

        load_tok(i + 3, z).wait()
        load_pair(i, z).wait()
        load_tok(i + 4, a).start()
        load_pair(i + 1, a).start()
        wait_gathers(a)
        wait_scatters(a)
        start_gathers(z)
        start_scatters(z)

        hb = _dot(_load_row_tiled(xbuf[a]).astype(BF16), w1b_ref[...]) + b1_ref[...]
        glu = jnp.minimum(hb[:, :f], SWIGLU_LIMIT)
        lin = jnp.clip(hb[:, f:], -SWIGLU_LIMIT, SWIGLU_LIMIT)
        act = (lin + 1.0) * (glu * jax.nn.sigmoid(SWIGLU_ALPHA * glu))
        _store_row_tiled(ybuf[a], _dot(act.astype(BF16), w2b_ref[...]) + b2_ref[...])

        @pl.when(i == last)
        def _():
            load_tok(i + 4, a).wait()
            load_pair(i + 1, a).wait()
            wait_gathers(b)
            wait_gathers(z)
            wait_scatters(b)
            wait_scatters(z)
            start_scatters(a)
            wait_scatters(a)

    for a in range(MOE_RING):
        pl.when((i % MOE_RING == a) & (i <= last))(functools.partial(step, a))


def _moe_blocks(h, tok_tab, pair_tab, blk_e, n_rows_out, w1, b1, w2, b2, layer):
    d = V7X_SUBLANES * V7X_LANES
    n_blk = blk_e.shape[0] - 1
    f2 = w1.shape[-1]
    f = w2.shape[-2]
    grid_spec = pltpu.PrefetchScalarGridSpec(
        num_scalar_prefetch=1,
        grid=(n_blk,),
        in_specs=[
            pl.BlockSpec(memory_space=pl.ANY),
            pl.BlockSpec(memory_space=pl.ANY),
            pl.BlockSpec(memory_space=pl.ANY),
            pl.BlockSpec((None, None, d, f2), lambda i, be: (layer, be[i], 0, 0)),
            pl.BlockSpec((None, None, 1, f2), lambda i, be: (layer, be[i], 0, 0)),
            pl.BlockSpec((None, None, f, d), lambda i, be: (layer, be[i], 0, 0)),
            pl.BlockSpec((None, None, 1, d), lambda i, be: (layer, be[i], 0, 0)),
        ],
        out_specs=pl.BlockSpec(memory_space=pl.ANY),
        scratch_shapes=[pltpu.SMEM((MOE_BLOCK,), jnp.int32)] * (2 * MOE_RING)
        + [pltpu.VMEM((MOE_BLOCK * V7X_SUBLANES, V7X_LANES), F32)] * (2 * MOE_RING) + [
            pltpu.VMEM((d, f2), BF16),
            pltpu.VMEM((f, d), BF16),
        ] + [pltpu.SemaphoreType.DMA((MOE_RING,))] * 4,
    )
    return pl.pallas_call(
        _moe_kernel,
        grid_spec=grid_spec,
        out_shape=jax.ShapeDtypeStruct((n_rows_out * V7X_SUBLANES, V7X_LANES), F32),
        compiler_params=_params(("arbitrary",), mib=56),
        name="moe_experts",
    )(blk_e, tok_tab, pair_tab, h, w1, b1.reshape(b1.shape[0], b1.shape[1], 1, f2), w2,
      b2.reshape(b2.shape[0], b2.shape[1], 1, d))


def _moe(h, logits, w1, b1, w2, b2, layer):
    n = logits.shape[0]
    nk = n * TOP_K
    top_val, top_idx = lax.top_k(logits, TOP_K)
    gates = jax.nn.softmax(top_val, axis=-1)
    flat_e = top_idx.reshape(-1)
    order = jnp.argsort(flat_e).astype(jnp.int32)
    counts = jnp.bincount(flat_e, length=N_EXPERTS).astype(jnp.int32)
    padded = (counts + MOE_BLOCK - 1) // MOE_BLOCK * MOE_BLOCK
    start = jnp.cumsum(counts) - counts
    pend = jnp.cumsum(padded)
    pstart = pend - padded
    n_blk = -(-nk // MOE_BLOCK) + N_EXPERTS
    blk_id = jnp.arange(-1, n_blk + 3, dtype=jnp.int32)
    blk_lo = blk_id * MOE_BLOCK
    blk_e = jnp.minimum(jnp.sum((pend[None, :] <= blk_lo[:, None]).astype(jnp.int32), axis=1), N_EXPERTS - 1)
    s0 = start[blk_e] + blk_lo - pstart[blk_e]
    left = jnp.where(blk_id >= 0, start[blk_e] + counts[blk_e] - s0, 0)
    r = jnp.arange(MOE_BLOCK, dtype=jnp.int32)[None, :]
    valid = r < left[:, None]
    win = order[jnp.clip(s0[:, None] + r, 0, nk - 1)]
    tok = lax.shift_right_logical(win, 2)
    dst = (win & (TOP_K - 1)) * n + tok
    dst_tab = jnp.where(valid, dst, nk + (blk_id[:, None] % MOE_RING) * MOE_BLOCK + r).reshape(-1)
    tok_tab = jnp.where(valid, tok, 0).reshape(-1)
    n_used = (pend[-1:] // MOE_BLOCK).astype(jnp.int32)
    y4 = _moe_blocks(h, tok_tab * V7X_SUBLANES, dst_tab * V7X_SUBLANES, jnp.concatenate([blk_e[1:n_blk + 1], n_used]),
                     nk + MOE_RING * MOE_BLOCK, w1, b1, w2, b2, layer)
    return y4, gates


def _combine_kernel(x_ref, gt_ref, g_ref, y0_ref, y1_ref, y2_ref, y3_ref, o_ref):
    g = g_ref[...]
    mix = g[:, 0:1] * _load_row_tiled(y0_ref)
    for k, y_ref in enumerate((y1_ref, y2_ref, y3_ref), start=1):
        mix = mix + g[:, k:k + 1] * _load_row_tiled(y_ref)
    o_ref[...] = x_ref[...] + gt_ref[...] * mix


def _combine(x, gt, gates, y4, n_tok, first_tok):
    b, t, d = x.shape
    tm = int(min(512, np.gcd.reduce([t, n_tok, first_tok or n_tok])))
    per_b = t // tm
    plane = n_tok // tm
    first = first_tok // tm
    row = pl.BlockSpec((None, tm, d), lambda i, j: (i, j, 0))
    y_spec = lambda k: pl.BlockSpec((tm * V7X_SUBLANES, V7X_LANES),
                                    lambda i, j: (k * plane + first + i * per_b + j, 0))
    return pl.pallas_call(
        _combine_kernel,
        grid=(b, per_b),
        in_specs=[row, pl.BlockSpec((None, 1, d), lambda i, j: (i, 0, 0)),
                  pl.BlockSpec((None, tm, TOP_K), lambda i, j: (i, j, 0))] + [y_spec(k) for k in range(TOP_K)],
        out_specs=row,
        out_shape=jax.ShapeDtypeStruct((b, t, d), F32),
        compiler_params=_params(("parallel", "parallel")),
        name="moe_combine",
    )(x, gt, gates, y4, y4, y4, y4)


def _rope_tables(t):
    tok = jnp.arange(t)
    row = (tok // GRID_W).astype(F32)[:, None]
    col = (tok % GRID_W).astype(F32)[:, None]
    inv = 1.0 / (ROPE_THETA ** (jnp.arange(0, ROPE_AXIS, 2, dtype=F32) / ROPE_AXIS))
    ang = jnp.concatenate([row * inv, row * inv, col * inv, col * inv], axis=-1)
    cos = jnp.ones((t, V7X_LANES), F32).at[:, MLA_NOPE:MLA_QK].set(jnp.cos(ang))
    sin = jnp.zeros((t, V7X_LANES), F32).at[:, MLA_NOPE:MLA_QK].set(jnp.sin(ang))
    return cos, sin


def _pad_heads(w, per_head, offset=0):
    lead = w.shape[:-1]
    w = w.reshape(lead + (MLA_HEADS, per_head))
    out = jnp.zeros(lead + (MLA_HEADS, V7X_LANES), w.dtype).at[..., offset:offset + per_head].set(w)
    return out.reshape(lead + (MLA_PAD,))


def _layer_weights(l, g_mix, w_in, na_q_norm, na_k_norm, mla_q_a_norm, mla_w_q_b, mla_kv_a_norm, mla_w_kv_b,
                   mla_q_norm, mla_k_norm, gla_w_gate, gla_b_gate, gla_o_norm, w_out, g_ffn, w_router, b_router):
    d = w_in.shape[1]
    sizes = (NA_WIDTH, NA_WIDTH, NA_WIDTH, MLA_Q_RANK, MLA_KV_RANK, MLA_ROPE, GLA_KW, GLA_KW, GLA_WIDTH, GLA_WIDTH,
             GLA_GATE_RANK, GLA_GATE_RANK)
    (wq, wk, wv, wcq, wckv, wkr, wgq, wgk, wgv, wgo, wlf, wlb) = jnp.split(w_in[l], np.cumsum(sizes)[:-1].tolist(), axis=-1)
    small = jnp.zeros((d, V7X_LANES), F32)
    small = small.at[:, S_LF:S_LF + GLA_GATE_RANK].set(wlf).at[:, S_LB:S_LB + GLA_GATE_RANK].set(wlb)
    small = small.at[:, S_KR:S_KR + MLA_ROPE].set(wkr)
    w_packed = jnp.concatenate([wq, wk, wv, wcq, wckv, wgq, wgk, wgv, wgo, small], axis=-1).astype(BF16)
    kvb = mla_w_kv_b[l].reshape(MLA_KV_RANK, MLA_HEADS, MLA_NOPE + MLA_V)
    wg = jnp.zeros((V7X_LANES, 2 * GLA_KW), F32)
    wg = wg.at[S_LF:S_LF + GLA_GATE_RANK, :GLA_KW].set(gla_w_gate[l, 0])
    wg = wg.at[S_LB:S_LB + GLA_GATE_RANK, GLA_KW:].set(gla_w_gate[l, 1])
    wg_hi, wg_lo = _split(wg)
    seg = (np.arange(NA_WIDTH)[:, None] // NA_DIM == np.arange(NA_WIDTH)[None, :] // NA_DIM)
    wr = jnp.zeros((d, V7X_LANES), F32).at[:, :N_EXPERTS].set(w_router[l])
    wr_hi, wr_lo = _split(wr)
    wo = w_out[l].astype(BF16)
    return {
        "gmix": g_mix[l][None, :],
        "w_in": w_packed,
        "naqn": jnp.tile(na_q_norm[l], NA_HEADS)[None, :],
        "nakn": jnp.tile(na_k_norm[l], NA_HEADS)[None, :],
        "qan": mla_q_a_norm[l][None, :],
        "wqb": _pad_heads(mla_w_q_b[l], MLA_QK).astype(BF16),
        "kvan": mla_kv_a_norm[l][None, :],
        "wkk": _pad_heads(kvb[:, :, :MLA_NOPE].reshape(MLA_KV_RANK, -1), MLA_NOPE).astype(BF16),
        "wkv": kvb[:, :, MLA_NOPE:].reshape(MLA_KV_RANK, MLA_WIDTH).astype(BF16),
        "mqn": _pad_heads(jnp.tile(mla_q_norm[l], MLA_HEADS), MLA_QK)[None, :],
        "mkn": _pad_heads(jnp.tile(mla_k_norm[l], MLA_HEADS), MLA_QK)[None, :],
        "wg_hi": wg_hi, "wg_lo": wg_lo,
        "bg": jnp.concatenate([gla_b_gate[l, 0], gla_b_gate[l, 1]])[None, :],
        "seg64": jnp.asarray(seg, BF16),
        "onorm": jnp.tile(gla_o_norm[l], GLA_HEADS)[None, :],
        "gffn": g_ffn[l][None, :],
        "wo_a": wo[:NA_WIDTH], "wo_b": wo[NA_WIDTH:NA_WIDTH + MLA_WIDTH], "wo_c": wo[NA_WIDTH + MLA_WIDTH:],
        "wr_hi": wr_hi, "wr_lo": wr_lo,
        "br": jnp.zeros((1, V7X_LANES), F32).at[0, :N_EXPERTS].set(b_router[l]),
    }


def kernel(x, c, ctx, c_ctx, w_ada, b_ada, g_mix, w_in, na_q_norm, na_k_norm, na_rpb, mla_q_a_norm, mla_w_q_b,
           mla_kv_a_norm, mla_w_kv_b, mla_q_norm, mla_k_norm, gla_w_gate, gla_b_gate, gla_o_norm, w_out, g_ffn,
           w_router, b_router, w_moe1, b_moe1, w_moe2, b_moe2):
    bsz, t, d = x.shape
    assert d == V7X_SUBLANES * V7X_LANES
    tc = ctx.shape[1]
    depth = w_ada.shape[0]
    rows = t // GRID_W
    cos, sin = _rope_tables(t)

    pad = (-(bsz + 1)) % 8
    cvec = jnp.concatenate([c, c_ctx[None, :], jnp.zeros((pad, d), F32)], axis=0)
    mod = _ada(cvec, w_ada, b_ada)

    h_ctx = ctx
    for l in range(depth):
        ctx_out = l < depth - 1
        lw = _layer_weights(l, g_mix, w_in, na_q_norm, na_k_norm, mla_q_a_norm, mla_w_q_b, mla_kv_a_norm,
                            mla_w_kv_b, mla_q_norm, mla_k_norm, gla_w_gate, gla_b_gate, gla_o_norm, w_out, g_ffn,
                            w_router, b_router)
        m_lat = [m[:, None, :] for m in jnp.split(mod[l, :bsz], 6, axis=-1)]
        m_ctx = [jnp.broadcast_to(m[None, :, :], (bsz, 1, d)) for m in jnp.split(mod[l, bsz:bsz + 1], 6, axis=-1)]
        sh1, sc1, gt1, sh2, sc2, gt2 = m_lat
        csh1, csc1, cgt1, csh2, csc2, cgt2 = m_ctx

        (naq, nak, nav, mq, mk, mv, gq, gk, gv, gout, gf, gb) = _inproj(x, sc1, sh1, lw, cos, sin, True)
        (cnaq, cnak, cnav, cmq, cmk, cmv, cgq, cgk, cgv, cgout, cgf, cgb) = _inproj(
            h_ctx, csc1, csh1, lw, cos[:tc], sin[:tc], False)

        bias = _na_bias_table(na_rpb[l], rows)
        a_x = _na_attention(naq, nak, nav, cnak, cnav, bias)
        b_x = _attention(mq, [(mk, mv), (cmk, cmv)], MLA_HEADS, V7X_LANES, MLA_V, 512)
        c_x, c_c = _gla((gq, gk, gv, gf, gb, gout), (cgq, cgk, cgv, cgf, cgb, cgout), lw["onorm"], lw["seg64"], ctx_out)
        x_new, h2, logits = _outproj(a_x, b_x, c_x, x, gt1, sc2, sh2, lw)

        toks = [h2.reshape(bsz * t * V7X_SUBLANES, V7X_LANES)]
        lgs = [logits.reshape(bsz * t, V7X_LANES)]
        if ctx_out:
            a_c = _attention_masked(cnaq, cnak, cnav, NA_DIM)
            b_c = _attention(cmq, [(cmk, cmv)], MLA_HEADS, V7X_LANES, MLA_V, 256)
            hc_new, hc2, clogits = _outproj(a_c, b_c, c_c, h_ctx, cgt1, csc2, csh2, lw)
            toks.append(hc2.reshape(bsz * tc * V7X_SUBLANES, V7X_LANES))
            lgs.append(clogits.reshape(bsz * tc, V7X_LANES))
        n_tok = sum(a.shape[0] for a in lgs)
        y4, gates = _moe(jnp.concatenate(toks, axis=0), jnp.concatenate(lgs, axis=0)[:, :N_EXPERTS],
                         w_moe1, b_moe1, w_moe2, b_moe2, l)
        x = _combine(x_new, gt2, gates[:bsz * t].reshape(bsz, t, TOP_K), y4, n_tok, 0)
        if ctx_out:
            h_ctx = _combine(hc_new, cgt2, gates[bsz * t:].reshape(bsz, tc, TOP_K), y4, n_tok, bsz * t)
    return x
```
